```python
import jax, jax.numpy as jnp
from jax import lax
import numpy as np

D_MODEL = 1024
BATCH = 8
SEQ = 8192
DEPTH = 4

N_BRANCH = 4
BRANCH_WIDTH = D_MODEL // 2
CHUNK = 128
SG_GROUPS = 4
SG_GROUP_DIM = BRANCH_WIDTH // SG_GROUPS
CONV_WIDTH = 3
MLA_HEADS = 8
MLA_NOPE_DIM = 64
MLA_ROPE_DIM = 32
MLA_V_DIM = BRANCH_WIDTH // MLA_HEADS
MLA_Q_RANK = 256
MLA_KV_RANK = 128
ROPE_THETA = 10000.0
SB_HEADS = 8
SB_HEAD_DIM = BRANCH_WIDTH // SB_HEADS
Q_BLOCK = 128
D_FF = 4 * D_MODEL
NORM_EPS = 1e-6

COLS_SG = 2 * BRANCH_WIDTH
COLS_CONV = 3 * BRANCH_WIDTH
COLS_MLA = MLA_Q_RANK + MLA_KV_RANK + MLA_ROPE_DIM
COLS_SB = 3 * SB_HEADS * SB_HEAD_DIM
COLS_GATE = N_BRANCH * D_MODEL
SPLIT_POINTS = (COLS_SG, COLS_SG + COLS_CONV, COLS_SG + COLS_CONV + COLS_MLA,
                COLS_SG + COLS_CONV + COLS_MLA + COLS_SB)
IN_COLS = COLS_SG + COLS_CONV + COLS_MLA + COLS_SB + COLS_GATE

kernel_name = 'hybrid_gated_parallel_mixer_trunk'


def _rmsnorm(x, g):
    x32 = x.astype(jnp.float32)
    y = x32 * lax.rsqrt(jnp.mean(x32 * x32, axis=-1, keepdims=True) + NORM_EPS)
    return y.astype(x.dtype) * g


def _layernorm(x, g):
    x32 = x.astype(jnp.float32)
    xc = x32 - jnp.mean(x32, axis=-1, keepdims=True)
    y = xc * lax.rsqrt(jnp.mean(xc * xc, axis=-1, keepdims=True) + NORM_EPS)
    return y.astype(x.dtype) * g


def _rope(x, cos, sin):
    x1, x2 = jnp.split(x, 2, axis=-1)
    return jnp.concatenate([x1 * cos - x2 * sin, x2 * cos + x1 * sin], axis=-1)


def _to_blocks(t):
    b, s, h, d = t.shape
    return t.reshape(b, s // Q_BLOCK, Q_BLOCK, h, d).transpose(1, 0, 2, 3, 4)


def _from_blocks(o):
    nb, b, qb, h, d = o.shape
    return o.transpose(1, 0, 2, 3, 4).reshape(b, nb * qb, h * d)


def _spatial_gating(z, norm_g, w_s, b_s):
    bsz, seq, _ = z.shape
    u, v = jnp.split(jax.nn.gelu(z), 2, axis=-1)
    v = _layernorm(v, norm_g).reshape(bsz, seq // CHUNK, CHUNK, SG_GROUPS, SG_GROUP_DIM)
    causal = jnp.tril(jnp.ones((CHUNK, CHUNK), dtype=w_s.dtype))
    s = jnp.einsum('gts,bnsgc->bntgc', w_s * causal, v) + b_s.T[:, :, None]
    return u * s.reshape(bsz, seq, BRANCH_WIDTH)


def _short_conv(z, conv_w):
    gate_b, gate_c, xv = jnp.split(z, 3, axis=-1)
    t = gate_c * xv
    y = lax.conv_general_dilated(t, conv_w[:, None, :], window_strides=(1,),
                                 padding=[(CONV_WIDTH - 1, 0)],
                                 dimension_numbers=('NWC', 'WIO', 'NWC'),
                                 feature_group_count=BRANCH_WIDTH)
    return gate_b * y


def _latent_attention(z, cos, sin, q_norm_g, w_uq, kv_norm_g, w_ukv):
    bsz, seq, _ = z.shape
    c_q, c_kv, k_pe = jnp.split(z, [MLA_Q_RANK, MLA_Q_RANK + MLA_KV_RANK], axis=-1)
    q = (_rmsnorm(c_q, q_norm_g) @ w_uq).reshape(bsz, seq, MLA_HEADS, MLA_NOPE_DIM + MLA_ROPE_DIM)
    q_nope, q_pe = jnp.split(q, [MLA_NOPE_DIM], axis=-1)
    q_pe = _rope(q_pe, cos[:, :, None, :], sin[:, :, None, :])
    kv = (_rmsnorm(c_kv, kv_norm_g) @ w_ukv).reshape(bsz, seq, MLA_HEADS, MLA_NOPE_DIM + MLA_V_DIM)
    k_nope, v = jnp.split(kv, [MLA_NOPE_DIM], axis=-1)
    k_pe = _rope(k_pe, cos, sin)
    scale = (MLA_NOPE_DIM + MLA_ROPE_DIM) ** -0.5
    k_pos = jnp.arange(seq)

    def block(args):
        i, qn, qp = args
        q_pos = i * Q_BLOCK + jnp.arange(Q_BLOCK)
        s = (jnp.einsum('bqhd,bkhd->bhqk', qn, k_nope, preferred_element_type=jnp.float32)
             + jnp.einsum('bqhr,bkr->bhqk', qp, k_pe, preferred_element_type=jnp.float32)) * scale
        s = jnp.where(k_pos[None, :] <= q_pos[:, None], s, -jnp.inf)
        p = jax.nn.softmax(s, axis=-1).astype(v.dtype)
        return jnp.einsum('bhqk,bkhd->bqhd', p, v)

    out = lax.map(block, (jnp.arange(seq // Q_BLOCK), _to_blocks(q_nope), _to_blocks(q_pe)))
    return _from_blocks(out)


def _stick_breaking(z):
    bsz, seq, _ = z.shape
    q, k, v = [t.reshape(bsz, seq, SB_HEADS, SB_HEAD_DIM) for t in jnp.split(z, 3, axis=-1)]
    scale = SB_HEAD_DIM ** -0.5
    k_pos = jnp.arange(seq)

    def block(args):
        i, qb = args
        q_pos = i * Q_BLOCK + jnp.arange(Q_BLOCK)
        logits = jnp.einsum('bqhd,bkhd->bhqk', qb, k, preferred_element_type=jnp.float32) * scale
        mask = k_pos[None, :] < q_pos[:, None]
        log_1m = jnp.where(mask, jax.nn.log_sigmoid(-logits), 0.0)
        rev = lax.cumsum(log_1m, axis=3, reverse=True)
        between = jnp.concatenate([rev[..., 1:], jnp.zeros_like(rev[..., :1])], axis=-1)
        a = jnp.where(mask, jnp.exp(jax.nn.log_sigmoid(logits) + between), 0.0)
        return jnp.einsum('bhqk,bkhd->bqhd', a.astype(v.dtype), v)

    out = lax.map(block, (jnp.arange(seq // Q_BLOCK), _to_blocks(q)))
    return _from_blocks(out)


def _token_mixing(h, cos, sin, w_in, sg_norm_g, sg_w, sg_b, conv_w,
                  q_norm_g, w_uq, kv_norm_g, w_ukv, w_branch, w_out):
    bsz, seq, _ = h.shape
    z = h @ w_in
    z_sg, z_conv, z_mla, z_sb, z_gate = jnp.split(z, list(SPLIT_POINTS), axis=-1)
    ys = jnp.stack([
        _spatial_gating(z_sg, sg_norm_g, sg_w, sg_b),
        _short_conv(z_conv, conv_w),
        _latent_attention(z_mla, cos, sin, q_norm_g, w_uq, kv_norm_g, w_ukv),
        _stick_breaking(z_sb),
    ], axis=2)
    up = jnp.einsum('bsnw,nwd->bsnd', ys, w_branch)
    gates = jax.nn.sigmoid(z_gate).reshape(bsz, seq, N_BRANCH, D_MODEL)
    merged = jnp.sum(gates * up, axis=2)
    return merged @ w_out


def _squared_relu_mlp(h, w1, w2):
    return jnp.square(jax.nn.relu(h @ w1)) @ w2


def _fwd_setup_inputs(seed: int = 0) -> dict:
    key = jax.random.key(seed)
    ks = jax.random.split(key, 21)
    f32 = jnp.float32

    def nrm(k, shape, scale):
        return jax.random.normal(k, shape, f32) * scale

    def gain(k, shape):
        return 1.0 + 0.02 * jax.random.normal(k, shape, f32)

    start = jax.random.randint(ks[2], (BATCH, 1), 0, 4096, dtype=jnp.int32)
    positions = start + jnp.arange(SEQ, dtype=jnp.int32)[None, :]
    return {
        'x': nrm(ks[0], (BATCH, SEQ, D_MODEL), 1.0),
        'c': nrm(ks[1], (BATCH, D_MODEL), 1.0),
        'positions': positions,
        'ada_w': nrm(ks[3], (DEPTH, D_MODEL, 6 * D_MODEL), 0.5 * D_MODEL ** -0.5),
        'ada_b': nrm(ks[4], (DEPTH, 6 * D_MODEL), 0.02),
        'norm1_g': gain(ks[5], (DEPTH, D_MODEL)),
        'norm2_g': gain(ks[6], (DEPTH, D_MODEL)),
        'w_in': nrm(ks[7], (DEPTH, D_MODEL, IN_COLS), D_MODEL ** -0.5),
        'sg_norm_g': gain(ks[8], (DEPTH, BRANCH_WIDTH)),
        'sg_w': nrm(ks[9], (DEPTH, SG_GROUPS, CHUNK, CHUNK), CHUNK ** -0.5),
        'sg_b': gain(ks[10], (DEPTH, SG_GROUPS, CHUNK)),
        'conv_w': nrm(ks[11], (DEPTH, CONV_WIDTH, BRANCH_WIDTH), CONV_WIDTH ** -0.5),
        'mla_q_norm_g': gain(ks[12], (DEPTH, MLA_Q_RANK)),
        'mla_w_uq': nrm(ks[13], (DEPTH, MLA_Q_RANK, MLA_HEADS * (MLA_NOPE_DIM + MLA_ROPE_DIM)), MLA_Q_RANK ** -0.5),
        'mla_kv_norm_g': gain(ks[14], (DEPTH, MLA_KV_RANK)),
        'mla_w_ukv': nrm(ks[15], (DEPTH, MLA_KV_RANK, MLA_HEADS * (MLA_NOPE_DIM + MLA_V_DIM)), MLA_KV_RANK ** -0.5),
        'w_branch': nrm(ks[16], (DEPTH, N_BRANCH, BRANCH_WIDTH, D_MODEL), BRANCH_WIDTH ** -0.5),
        'w_out': nrm(ks[17], (DEPTH, D_MODEL, D_MODEL), D_MODEL ** -0.5),
        'mlp_w1': nrm(ks[18], (DEPTH, D_MODEL, D_FF), D_MODEL ** -0.5),
        'mlp_w2': nrm(ks[19], (DEPTH, D_FF, D_MODEL), 0.5 * D_FF ** -0.5),
        'final_norm_g': gain(ks[20], (D_MODEL,)),
    }


def _fwd_reference(x, c, positions, ada_w, ada_b, norm1_g, norm2_g, w_in, sg_norm_g, sg_w, sg_b,
              conv_w, mla_q_norm_g, mla_w_uq, mla_kv_norm_g, mla_w_ukv, w_branch, w_out,
              mlp_w1, mlp_w2, final_norm_g):
    inv_freq = ROPE_THETA ** (-jnp.arange(0, MLA_ROPE_DIM, 2, dtype=jnp.float32) / MLA_ROPE_DIM)
    ang = positions.astype(jnp.float32)[..., None] * inv_freq
    cos = jnp.cos(ang).astype(x.dtype)
    sin = jnp.sin(ang).astype(x.dtype)
    c_act = jax.nn.silu(c)
    for l in range(DEPTH):
        mod = c_act @ ada_w[l] + ada_b[l]
        sh1, sc1, g1, sh2, sc2, g2 = jnp.split(mod[:, None, :], 6, axis=-1)
        h = _rmsnorm(x, norm1_g[l]) * (1.0 + sc1) + sh1
        x = x + g1 * _token_mixing(h, cos, sin, w_in[l], sg_norm_g[l], sg_w[l], sg_b[l], conv_w[l],
                                   mla_q_norm_g[l], mla_w_uq[l], mla_kv_norm_g[l], mla_w_ukv[l],
                                   w_branch[l], w_out[l])
        h = _rmsnorm(x, norm2_g[l]) * (1.0 + sc2) + sh2
        x = x + g2 * _squared_relu_mlp(h, mlp_w1[l], mlp_w2[l])
    return _rmsnorm(x, final_norm_g)


import jax as _jax
import jax.numpy as _jnp

TWIN_FORMAT = 'train_step'
FWD_PARAMS = ['x', 'c', 'positions', 'ada_w', 'ada_b', 'norm1_g', 'norm2_g', 'w_in', 'sg_norm_g', 'sg_w', 'sg_b', 'conv_w', 'mla_q_norm_g', 'mla_w_uq', 'mla_kv_norm_g', 'mla_w_ukv', 'w_branch', 'w_out', 'mlp_w1', 'mlp_w2', 'final_norm_g']
TWIN_WEIGHTS = ['ada_w', 'ada_b', 'norm1_g', 'norm2_g', 'w_in', 'sg_norm_g', 'sg_w', 'sg_b', 'conv_w', 'mla_q_norm_g', 'mla_w_uq', 'mla_kv_norm_g', 'mla_w_ukv', 'w_branch', 'w_out', 'mlp_w1', 'mlp_w2', 'final_norm_g']
TWIN_DIFF_INPUT = 'x'
TWIN_INPUTS = ['x', 'c', 'positions', 'ada_w', 'ada_b', 'norm1_g', 'norm2_g', 'w_in', 'sg_norm_g', 'sg_w', 'sg_b', 'conv_w', 'mla_q_norm_g', 'mla_w_uq', 'mla_kv_norm_g', 'mla_w_ukv', 'w_branch', 'w_out', 'mlp_w1', 'mlp_w2', 'final_norm_g', 'loss_target', 'm_ada_w', 'm_ada_b', 'm_norm1_g', 'm_norm2_g', 'm_w_in', 'm_sg_norm_g', 'm_sg_w', 'm_sg_b', 'm_conv_w', 'm_mla_q_norm_g', 'm_mla_w_uq', 'm_mla_kv_norm_g', 'm_mla_w_ukv', 'm_w_branch', 'm_w_out', 'm_mlp_w1', 'm_mlp_w2', 'm_final_norm_g', 'v_ada_w', 'v_ada_b', 'v_norm1_g', 'v_norm2_g', 'v_w_in', 'v_sg_norm_g', 'v_sg_w', 'v_sg_b', 'v_conv_w', 'v_mla_q_norm_g', 'v_mla_w_uq', 'v_mla_kv_norm_g', 'v_mla_w_ukv', 'v_w_branch', 'v_w_out', 'v_mlp_w1', 'v_mlp_w2', 'v_final_norm_g']
TWIN_OUTPUTS = ['loss', 'grad_x', 'grad_ada_w', 'grad_ada_b', 'grad_norm1_g', 'grad_norm2_g', 'grad_w_in', 'grad_sg_norm_g', 'grad_sg_w', 'grad_sg_b', 'grad_conv_w', 'grad_mla_q_norm_g', 'grad_mla_w_uq', 'grad_mla_kv_norm_g', 'grad_mla_w_ukv', 'grad_w_branch', 'grad_w_out', 'grad_mlp_w1', 'grad_mlp_w2', 'grad_final_norm_g', 'delta_ada_w', 'delta_ada_b', 'delta_norm1_g', 'delta_norm2_g', 'delta_w_in', 'delta_sg_norm_g', 'delta_sg_w', 'delta_sg_b', 'delta_conv_w', 'delta_mla_q_norm_g', 'delta_mla_w_uq', 'delta_mla_kv_norm_g', 'delta_mla_w_ukv', 'delta_w_branch', 'delta_w_out', 'delta_mlp_w1', 'delta_mlp_w2', 'delta_final_norm_g', 'new_m_ada_w', 'new_m_ada_b', 'new_m_norm1_g', 'new_m_norm2_g', 'new_m_w_in', 'new_m_sg_norm_g', 'new_m_sg_w', 'new_m_sg_b', 'new_m_conv_w', 'new_m_mla_q_norm_g', 'new_m_mla_w_uq', 'new_m_mla_kv_norm_g', 'new_m_mla_w_ukv', 'new_m_w_branch', 'new_m_w_out', 'new_m_mlp_w1', 'new_m_mlp_w2', 'new_m_final_norm_g', 'new_v_ada_w', 'new_v_ada_b', 'new_v_norm1_g', 'new_v_norm2_g', 'new_v_w_in', 'new_v_sg_norm_g', 'new_v_sg_w', 'new_v_sg_b', 'new_v_conv_w', 'new_v_mla_q_norm_g', 'new_v_mla_w_uq', 'new_v_mla_kv_norm_g', 'new_v_mla_w_ukv', 'new_v_w_branch', 'new_v_w_out', 'new_v_mlp_w1', 'new_v_mlp_w2', 'new_v_final_norm_g']
TWIN_LEAF_KINDS = {'loss': 'loss', 'grad_x': 'grad_x', 'grad_ada_w': 'grad_w', 'grad_ada_b': 'grad_w', 'grad_norm1_g': 'grad_w', 'grad_norm2_g': 'grad_w', 'grad_w_in': 'grad_w', 'grad_sg_norm_g': 'grad_w', 'grad_sg_w': 'grad_w', 'grad_sg_b': 'grad_w', 'grad_conv_w': 'grad_w', 'grad_mla_q_norm_g': 'grad_w', 'grad_mla_w_uq': 'grad_w', 'grad_mla_kv_norm_g': 'grad_w', 'grad_mla_w_ukv': 'grad_w', 'grad_w_branch': 'grad_w', 'grad_w_out': 'grad_w', 'grad_mlp_w1': 'grad_w', 'grad_mlp_w2': 'grad_w', 'grad_final_norm_g': 'grad_w', 'delta_ada_w': 'delta_w', 'delta_ada_b': 'delta_w', 'delta_norm1_g': 'delta_w', 'delta_norm2_g': 'delta_w', 'delta_w_in': 'delta_w', 'delta_sg_norm_g': 'delta_w', 'delta_sg_w': 'delta_w', 'delta_sg_b': 'delta_w', 'delta_conv_w': 'delta_w', 'delta_mla_q_norm_g': 'delta_w', 'delta_mla_w_uq': 'delta_w', 'delta_mla_kv_norm_g': 'delta_w', 'delta_mla_w_ukv': 'delta_w', 'delta_w_branch': 'delta_w', 'delta_w_out': 'delta_w', 'delta_mlp_w1': 'delta_w', 'delta_mlp_w2': 'delta_w', 'delta_final_norm_g': 'delta_w', 'new_m_ada_w': 'new_m', 'new_m_ada_b': 'new_m', 'new_m_norm1_g': 'new_m', 'new_m_norm2_g': 'new_m', 'new_m_w_in': 'new_m', 'new_m_sg_norm_g': 'new_m', 'new_m_sg_w': 'new_m', 'new_m_sg_b': 'new_m', 'new_m_conv_w': 'new_m', 'new_m_mla_q_norm_g': 'new_m', 'new_m_mla_w_uq': 'new_m', 'new_m_mla_kv_norm_g': 'new_m', 'new_m_mla_w_ukv': 'new_m', 'new_m_w_branch': 'new_m', 'new_m_w_out': 'new_m', 'new_m_mlp_w1': 'new_m', 'new_m_mlp_w2': 'new_m', 'new_m_final_norm_g': 'new_m', 'new_v_ada_w': 'new_v', 'new_v_ada_b': 'new_v', 'new_v_norm1_g': 'new_v', 'new_v_norm2_g': 'new_v', 'new_v_w_in': 'new_v', 'new_v_sg_norm_g': 'new_v', 'new_v_sg_w': 'new_v', 'new_v_sg_b': 'new_v', 'new_v_conv_w': 'new_v', 'new_v_mla_q_norm_g': 'new_v', 'new_v_mla_w_uq': 'new_v', 'new_v_mla_kv_norm_g': 'new_v', 'new_v_mla_w_ukv': 'new_v', 'new_v_w_branch': 'new_v', 'new_v_w_out': 'new_v', 'new_v_mlp_w1': 'new_v', 'new_v_mlp_w2': 'new_v', 'new_v_final_norm_g': 'new_v'}


def _forward(args):
    return _fwd_reference(*[args[k] for k in FWD_PARAMS])


def _output_shape():
    def fwd():
        inp = _fwd_setup_inputs(0)
        return _fwd_reference(*[inp[k] for k in FWD_PARAMS])
    out = _jax.eval_shape(fwd)
    return out.shape, out.dtype

N_MICROBATCH = 1
ADAM_LR = 0.001
ADAM_B1 = 0.9
ADAM_B2 = 0.999
ADAM_EPS = 1e-08
ADAM_WD = 0.01
ADAM_STEP = 10
PER_EXAMPLE_BATCH_AXIS = {'x': 0, 'c': 0, 'positions': 0, 'loss_target': 0}
SHARED_INPUTS = []
_WEIGHT_DTYPES = {'ada_w': _jnp.float32, 'ada_b': _jnp.float32, 'norm1_g': _jnp.float32, 'norm2_g': _jnp.float32, 'w_in': _jnp.float32, 'sg_norm_g': _jnp.float32, 'sg_w': _jnp.float32, 'sg_b': _jnp.float32, 'conv_w': _jnp.float32, 'mla_q_norm_g': _jnp.float32, 'mla_w_uq': _jnp.float32, 'mla_kv_norm_g': _jnp.float32, 'mla_w_ukv': _jnp.float32, 'w_branch': _jnp.float32, 'w_out': _jnp.float32, 'mlp_w1': _jnp.float32, 'mlp_w2': _jnp.float32, 'final_norm_g': _jnp.float32}
MOMENT_SCALE = {'ada_w': 8.728333e-02, 'ada_b': 1.490602e-01, 'norm1_g': 1.040912e-01, 'norm2_g': 5.815633e-02, 'w_in': 3.780257e-02, 'sg_norm_g': 2.820986e-02, 'sg_w': 2.870262e-02, 'sg_b': 4.109954e-02, 'conv_w': 7.258257e-02, 'mla_q_norm_g': 1.455837e-02, 'mla_w_uq': 8.126446e-03, 'mla_kv_norm_g': 4.515246e-02, 'mla_w_ukv': 1.593735e-02, 'w_branch': 3.505008e-02, 'w_out': 7.052203e-02, 'mlp_w1': 3.008306e-02, 'mlp_w2': 1.066768e-01, 'final_norm_g': 6.419163e+01}


def _to_microbatches(a, axis):
    t = _jnp.moveaxis(a, axis, 0)
    t = t.reshape((N_MICROBATCH, t.shape[0] // N_MICROBATCH) + t.shape[1:])
    return _jnp.moveaxis(t, 1, axis + 1)


def setup_inputs(seed: int = 0) -> dict:
    inp = _fwd_setup_inputs(seed)
    key = _jax.random.fold_in(_jax.random.key(seed), 7919)
    shape, _ = _output_shape()
    out = dict(inp)
    out["loss_target"] = _jax.random.normal(_jax.random.fold_in(key, 0), shape, _jnp.float32)
    for i, name in enumerate(TWIN_WEIGHTS):
        w = inp[name].astype(_jnp.float32)
        if MOMENT_SCALE is None:
            s = _jnp.sqrt(_jnp.mean(_jnp.square(w)) + 1e-30)
        else:
            s = MOMENT_SCALE[name]
        km, kv = _jax.random.split(_jax.random.fold_in(key, i + 1))
        out[name] = w
        out["m_" + name] = s * _jax.random.normal(km, w.shape, _jnp.float32)
        out["v_" + name] = (s * s) * _jax.random.uniform(kv, w.shape, _jnp.float32, 0.5, 1.5)
    if N_MICROBATCH > 1:
        for name, axis in PER_EXAMPLE_BATCH_AXIS.items():
            out[name] = _to_microbatches(out[name], axis)
    return {'x': out['x'], 'c': out['c'], 'positions': out['positions'], 'ada_w': out['ada_w'], 'ada_b': out['ada_b'], 'norm1_g': out['norm1_g'], 'norm2_g': out['norm2_g'], 'w_in': out['w_in'], 'sg_norm_g': out['sg_norm_g'], 'sg_w': out['sg_w'], 'sg_b': out['sg_b'], 'conv_w': out['conv_w'], 'mla_q_norm_g': out['mla_q_norm_g'], 'mla_w_uq': out['mla_w_uq'], 'mla_kv_norm_g': out['mla_kv_norm_g'], 'mla_w_ukv': out['mla_w_ukv'], 'w_branch': out['w_branch'], 'w_out': out['w_out'], 'mlp_w1': out['mlp_w1'], 'mlp_w2': out['mlp_w2'], 'final_norm_g': out['final_norm_g'], 'loss_target': out['loss_target'], 'm_ada_w': out['m_ada_w'], 'm_ada_b': out['m_ada_b'], 'm_norm1_g': out['m_norm1_g'], 'm_norm2_g': out['m_norm2_g'], 'm_w_in': out['m_w_in'], 'm_sg_norm_g': out['m_sg_norm_g'], 'm_sg_w': out['m_sg_w'], 'm_sg_b': out['m_sg_b'], 'm_conv_w': out['m_conv_w'], 'm_mla_q_norm_g': out['m_mla_q_norm_g'], 'm_mla_w_uq': out['m_mla_w_uq'], 'm_mla_kv_norm_g': out['m_mla_kv_norm_g'], 'm_mla_w_ukv': out['m_mla_w_ukv'], 'm_w_branch': out['m_w_branch'], 'm_w_out': out['m_w_out'], 'm_mlp_w1': out['m_mlp_w1'], 'm_mlp_w2': out['m_mlp_w2'], 'm_final_norm_g': out['m_final_norm_g'], 'v_ada_w': out['v_ada_w'], 'v_ada_b': out['v_ada_b'], 'v_norm1_g': out['v_norm1_g'], 'v_norm2_g': out['v_norm2_g'], 'v_w_in': out['v_w_in'], 'v_sg_norm_g': out['v_sg_norm_g'], 'v_sg_w': out['v_sg_w'], 'v_sg_b': out['v_sg_b'], 'v_conv_w': out['v_conv_w'], 'v_mla_q_norm_g': out['v_mla_q_norm_g'], 'v_mla_w_uq': out['v_mla_w_uq'], 'v_mla_kv_norm_g': out['v_mla_kv_norm_g'], 'v_mla_w_ukv': out['v_mla_w_ukv'], 'v_w_branch': out['v_w_branch'], 'v_w_out': out['v_w_out'], 'v_mlp_w1': out['v_mlp_w1'], 'v_mlp_w2': out['v_mlp_w2'], 'v_final_norm_g': out['v_final_norm_g']}


def _loss(weights, diff, rest, loss_target):
    with _jax.named_scope("forward"):
        args = {**rest, TWIN_DIFF_INPUT: diff, **{k: w.astype(_WEIGHT_DTYPES[k]) for k, w in weights.items()}}
        y = _forward(args)
    with _jax.named_scope("loss_head"):
        err = _jnp.square(y.astype(_jnp.float32) - loss_target)
        return 0.5 * _jnp.sum(_jnp.mean(err, axis=-1)) if err.ndim else 0.5 * err


def _adamw(w, g, m, v):
    m = ADAM_B1 * m + (1.0 - ADAM_B1) * g
    v = ADAM_B2 * v + (1.0 - ADAM_B2) * _jnp.square(g)
    m_hat = m / (1.0 - ADAM_B1 ** ADAM_STEP)
    v_hat = v / (1.0 - ADAM_B2 ** ADAM_STEP)
    delta = -ADAM_LR * (m_hat / (_jnp.sqrt(v_hat) + ADAM_EPS) + ADAM_WD * w)
    return delta, m, v


def reference(x, c, positions, ada_w, ada_b, norm1_g, norm2_g, w_in, sg_norm_g, sg_w, sg_b, conv_w, mla_q_norm_g, mla_w_uq, mla_kv_norm_g, mla_w_ukv, w_branch, w_out, mlp_w1, mlp_w2, final_norm_g, loss_target, m_ada_w, m_ada_b, m_norm1_g, m_norm2_g, m_w_in, m_sg_norm_g, m_sg_w, m_sg_b, m_conv_w, m_mla_q_norm_g, m_mla_w_uq, m_mla_kv_norm_g, m_mla_w_ukv, m_w_branch, m_w_out, m_mlp_w1, m_mlp_w2, m_final_norm_g, v_ada_w, v_ada_b, v_norm1_g, v_norm2_g, v_w_in, v_sg_norm_g, v_sg_w, v_sg_b, v_conv_w, v_mla_q_norm_g, v_mla_w_uq, v_mla_kv_norm_g, v_mla_w_ukv, v_w_branch, v_w_out, v_mlp_w1, v_mlp_w2, v_final_norm_g):
    given = dict(x=x, c=c, positions=positions, ada_w=ada_w, ada_b=ada_b, norm1_g=norm1_g, norm2_g=norm2_g, w_in=w_in, sg_norm_g=sg_norm_g, sg_w=sg_w, sg_b=sg_b, conv_w=conv_w, mla_q_norm_g=mla_q_norm_g, mla_w_uq=mla_w_uq, mla_kv_norm_g=mla_kv_norm_g, mla_w_ukv=mla_w_ukv, w_branch=w_branch, w_out=w_out, mlp_w1=mlp_w1, mlp_w2=mlp_w2, final_norm_g=final_norm_g, loss_target=loss_target, m_ada_w=m_ada_w, m_ada_b=m_ada_b, m_norm1_g=m_norm1_g, m_norm2_g=m_norm2_g, m_w_in=m_w_in, m_sg_norm_g=m_sg_norm_g, m_sg_w=m_sg_w, m_sg_b=m_sg_b, m_conv_w=m_conv_w, m_mla_q_norm_g=m_mla_q_norm_g, m_mla_w_uq=m_mla_w_uq, m_mla_kv_norm_g=m_mla_kv_norm_g, m_mla_w_ukv=m_mla_w_ukv, m_w_branch=m_w_branch, m_w_out=m_w_out, m_mlp_w1=m_mlp_w1, m_mlp_w2=m_mlp_w2, m_final_norm_g=m_final_norm_g, v_ada_w=v_ada_w, v_ada_b=v_ada_b, v_norm1_g=v_norm1_g, v_norm2_g=v_norm2_g, v_w_in=v_w_in, v_sg_norm_g=v_sg_norm_g, v_sg_w=v_sg_w, v_sg_b=v_sg_b, v_conv_w=v_conv_w, v_mla_q_norm_g=v_mla_q_norm_g, v_mla_w_uq=v_mla_w_uq, v_mla_kv_norm_g=v_mla_kv_norm_g, v_mla_w_ukv=v_mla_w_ukv, v_w_branch=v_w_branch, v_w_out=v_w_out, v_mlp_w1=v_mlp_w1, v_mlp_w2=v_mlp_w2, v_final_norm_g=v_final_norm_g)
    weights = {n: given[n] for n in TWIN_WEIGHTS}
    shared = {n: given[n] for n in SHARED_INPUTS}
    per_example = {n: given[n] for n in ['x', 'c', 'positions']}
    grad_fn = _jax.value_and_grad(_loss, argnums=(0, 1))

    def one_microbatch(ex, loss_target):
        ex = dict(ex)
        diff = ex.pop(TWIN_DIFF_INPUT)
        return grad_fn(weights, diff, {**shared, **ex}, loss_target)

    if N_MICROBATCH == 1:
        loss, (grad_w, grad_x) = one_microbatch(per_example, given["loss_target"])
    else:
        def body(carry, xs):
            loss_sum, grad_sum = carry
            l_k, (gw_k, gx_k) = one_microbatch(xs[0], xs[1])
            with _jax.named_scope("update"):
                return (loss_sum + l_k, _jax.tree.map(_jnp.add, grad_sum, gw_k)), gx_k

        init = (_jnp.zeros((), _jnp.float32), _jax.tree.map(_jnp.zeros_like, weights))
        (loss, grad_w), grad_x = _jax.lax.scan(body, init, (per_example, given["loss_target"]))
    with _jax.named_scope("update"):
        delta_w, new_m, new_v = {}, {}, {}
        for n in TWIN_WEIGHTS:
            delta_w[n], new_m[n], new_v[n] = _adamw(weights[n], grad_w[n], given["m_" + n], given["v_" + n])
    return (loss, grad_x, *[grad_w[n] for n in TWIN_WEIGHTS], *[delta_w[n] for n in TWIN_WEIGHTS],
            *[new_m[n] for n in TWIN_WEIGHTS], *[new_v[n] for n in TWIN_WEIGHTS])
```

```python
import functools
import math

import jax
import jax.numpy as jnp
from jax import lax
from jax.experimental import pallas as pl
from jax.experimental.pallas import tpu as pltpu

F32 = jnp.float32
BF16 = jnp.bfloat16

D_MODEL = 1024
WIDTH = 512
CHUNK = 128
SG_GROUPS = 4
HEADS = 8
MLA_NOPE = 64
MLA_ROPE = 32
MLA_V = 64
MLA_Q_RANK = 256
MLA_KV_RANK = 128
MLA_QK_PAD = 128
SB_DIM = 64
ROPE_THETA = 10000.0
NORM_EPS = 1e-6
N_BRANCH = 4
D_FF = 4096
Z_SG, Z_CONV, Z_MLA, Z_SB, Z_GATE, Z_COLS = 0, 1024, 2560, 3072, 4608, 8704
IN_COLS = 8608
MLA_COLS = MLA_Q_RANK + MLA_KV_RANK + MLA_ROPE

ADAM_LR, ADAM_B1, ADAM_B2, ADAM_EPS, ADAM_WD, ADAM_STEP = 0.001, 0.9, 0.999, 1e-08, 0.01, 10

N_DEV = 8
AXES = ("x", "y", "c")
MESH = pl.DeviceIdType.MESH
VMEM_LIMIT_V7X = 56 * 1024 * 1024
ATT_BLOCK = 256
NEG_BIG = -1e30


def _cparams(sem=None):
    return pltpu.CompilerParams(dimension_semantics=sem, vmem_limit_bytes=VMEM_LIMIT_V7X)


def _div_tile(n, pref):
    if n <= pref:
        return n
    t = (pref // 128) * 128
    while t >= 128:
        if n % t == 0:
            return t
        t -= 128
    raise ValueError(f"no tile for {n}")


def _row_tile(t):
    return min(512, t)


def _dot(a, b, dims):
    return lax.dot_general(a.astype(BF16), b.astype(BF16), (dims, ((), ())), preferred_element_type=F32)


@jax.custom_vjp
def _mm(a, b):
    return _dot(a, b, ((1,), (0,)))


def _mm_fwd(a, b):
    return _mm(a, b), (a, b)


def _mm_bwd(res, g):
    a, b = res
    return _dot(g, b, ((1,), (1,))).astype(a.dtype), _dot(a, g, ((0,), (0,))).astype(b.dtype)


_mm.defvjp(_mm_fwd, _mm_bwd)


def _rms(x, g):
    return (x * lax.rsqrt(jnp.mean(x * x, axis=-1, keepdims=True) + NORM_EPS)) * g


def _normmod(x, g, sc, sh):
    return _rms(x, g) * (1.0 + sc) + sh


def _split_dot(m01, x):
    hi = x.astype(BF16)
    lo = (x - hi.astype(F32)).astype(BF16)
    return (jnp.dot(m01, hi, preferred_element_type=F32) + jnp.dot(m01, lo, preferred_element_type=F32))


def normmod_fwd(x, g, sc, sh):
    t, d = x.shape
    tt = _row_tile(t)

    def body(x_ref, g_ref, sc_ref, sh_ref, h_ref):
        h_ref[...] = _normmod(x_ref[...], g_ref[...], sc_ref[...], sh_ref[...]).astype(BF16)

    row = pl.BlockSpec((tt, d), lambda i: (i, 0))
    vec = pl.BlockSpec((1, d), lambda i: (0, 0))
    return pl.pallas_call(
        body, grid=(t // tt,), in_specs=[row, vec, vec, vec], out_specs=row,
        out_shape=jax.ShapeDtypeStruct((t, d), BF16), name="normmod_fwd", compiler_params=_cparams(("parallel",)),
    )(x, g, sc, sh)


def normmod_bwd(x, g, sc, sh, dh, dres):
    t, d = x.shape
    tt = _row_tile(t)

    def body(x_ref, g_ref, sc_ref, sh_ref, dh_ref, dres_ref, dx_ref, dg_ref, dsc_ref, dsh_ref):
        _, vjp = jax.vjp(_normmod, x_ref[...], g_ref[...], sc_ref[...], sh_ref[...])
        dx, dg, dsc, dsh = vjp(dh_ref[...])
        dx_ref[...] = dx + dres_ref[...]

        @pl.when(pl.program_id(0) == 0)
        def _():
            dg_ref[...] = jnp.zeros_like(dg_ref)
            dsc_ref[...] = jnp.zeros_like(dsc_ref)
            dsh_ref[...] = jnp.zeros_like(dsh_ref)

        dg_ref[...] += dg
        dsc_ref[...] += dsc
        dsh_ref[...] += dsh

    row = pl.BlockSpec((tt, d), lambda i: (i, 0))
    vec = pl.BlockSpec((1, d), lambda i: (0, 0))
    vs = jax.ShapeDtypeStruct((1, d), F32)
    return pl.pallas_call(
        body, grid=(t // tt,), in_specs=[row, vec, vec, vec, row, row], out_specs=[row, vec, vec, vec],
        out_shape=[jax.ShapeDtypeStruct((t, d), F32), vs, vs, vs], name="normmod_bwd",
        compiler_params=_cparams(("arbitrary",)),
    )(x, g, sc, sh, dh, dres)


def resid_bwd(dxn, m, g):
    t, d = dxn.shape
    tt = _row_tile(t)

    def body(dx_ref, m_ref, g_ref, dm_ref, dg_ref):
        dx = dx_ref[...]
        dm_ref[...] = (dx * g_ref[...]).astype(BF16)

        @pl.when(pl.program_id(0) == 0)
        def _():
            dg_ref[...] = jnp.zeros_like(dg_ref)

        dg_ref[...] += jnp.sum(dx * m_ref[...], axis=0, keepdims=True)

    row = pl.BlockSpec((tt, d), lambda i: (i, 0))
    vec = pl.BlockSpec((1, d), lambda i: (0, 0))
    return pl.pallas_call(
        body, grid=(t // tt,), in_specs=[row, row, vec], out_specs=[row, vec],
        out_shape=[jax.ShapeDtypeStruct((t, d), BF16), jax.ShapeDtypeStruct((1, d), F32)], name="resid_bwd",
        compiler_params=_cparams(("arbitrary",)),
    )(dxn, m, g)


def loss_head(x, gf, tgt):
    t, d = x.shape
    tt = _row_tile(t)

    def f(xv, gv, tv):
        y = _rms(xv, gv)
        return 0.5 * jnp.sum(jnp.mean(jnp.square(y - tv), axis=-1))

    def body(x_ref, g_ref, t_ref, loss_ref, dx_ref, dg_ref):
        val, vjp = jax.vjp(lambda xv, gv: f(xv, gv, t_ref[...]), x_ref[...], g_ref[...])
        dx, dg = vjp(jnp.ones((), F32))
        dx_ref[...] = dx

        @pl.when(pl.program_id(0) == 0)
        def _():
            dg_ref[...] = jnp.zeros_like(dg_ref)
            loss_ref[...] = jnp.zeros_like(loss_ref)

        dg_ref[...] += dg
        loss_ref[...] += jnp.full(loss_ref.shape, val, F32)

    row = pl.BlockSpec((tt, d), lambda i: (i, 0))
    vec = pl.BlockSpec((1, d), lambda i: (0, 0))
    lsp = pl.BlockSpec((1, 128), lambda i: (0, 0))
    return pl.pallas_call(
        body, grid=(t // tt,), in_specs=[row, vec, row], out_specs=[lsp, row, vec],
        out_shape=[jax.ShapeDtypeStruct((1, 128), F32), jax.ShapeDtypeStruct((t, d), F32),
                   jax.ShapeDtypeStruct((1, d), F32)],
        name="loss_head", compiler_params=_cparams(("arbitrary",)),
    )(x, gf, tgt)


def matmul(a, b, *, name, out_dtypes=(F32,), epi=None, epi_in=(), tm=1024, tn=512, tk=1024):
    m, k = a.shape
    n = b.shape[1]
    tm, tn, tk = min(tm, m), _div_tile(n, tn), _div_tile(k, tk)
    nk = k // tk
    n_epi, n_out = len(epi_in), len(out_dtypes)

    def body(*refs):
        a_ref, b_ref = refs[:2]
        e_refs = refs[2:2 + n_epi]
        o_refs = refs[2 + n_epi:2 + n_epi + n_out]
        acc_ref = refs[-1]
        kk = pl.program_id(2)

        @pl.when(kk == 0)
        def _():
            acc_ref[...] = jnp.zeros_like(acc_ref)

        acc_ref[...] += jnp.dot(a_ref[...].astype(BF16), b_ref[...].astype(BF16), preferred_element_type=F32)

        @pl.when(kk == nk - 1)
        def _():
            acc = acc_ref[...]
            outs = (acc,) if epi is None else epi(acc, *[r[...] for r in e_refs])
            for o_ref, o in zip(o_refs, outs):
                o_ref[...] = o.astype(o_ref.dtype)

    in_specs = [pl.BlockSpec((tm, tk), lambda i, j, kk: (i, kk)), pl.BlockSpec((tk, tn), lambda i, j, kk: (kk, j))]
    for _, kind in epi_in:
        in_specs.append(pl.BlockSpec((tm, tn), lambda i, j, kk: (i, j)) if kind == "mn"
                        else pl.BlockSpec((1, tn), lambda i, j, kk: (0, j)))
    out_spec = pl.BlockSpec((tm, tn), lambda i, j, kk: (i, j))
    outs = pl.pallas_call(
        body, grid=(m // tm, n // tn, nk), in_specs=in_specs, out_specs=[out_spec] * n_out,
        out_shape=[jax.ShapeDtypeStruct((m, n), dt) for dt in out_dtypes],
        scratch_shapes=[pltpu.VMEM((tm, tn), F32)], name=name,
        compiler_params=_cparams(("parallel", "parallel", "arbitrary")),
    )(a, b, *[arr for arr, _ in epi_in])
    return outs[0] if n_out == 1 else outs


def matmul_tn(a, b, *, name, tko=1024, tn=512, tt=1024):
    t, ko = a.shape
    n = b.shape[1]
    tko, tn, tt = _div_tile(ko, tko), _div_tile(n, tn), min(tt, t)
    nt = t // tt

    def body(a_ref, b_ref, o_ref, acc_ref):
        s = pl.program_id(2)

        @pl.when(s == 0)
        def _():
            acc_ref[...] = jnp.zeros_like(acc_ref)

        acc_ref[...] += _dot(a_ref[...], b_ref[...], ((0,), (0,)))

        @pl.when(s == nt - 1)
        def _():
            o_ref[...] = acc_ref[...]

    return pl.pallas_call(
        body, grid=(ko // tko, n // tn, nt),
        in_specs=[pl.BlockSpec((tt, tko), lambda i, j, s: (s, i)), pl.BlockSpec((tt, tn), lambda i, j, s: (s, j))],
        out_specs=pl.BlockSpec((tko, tn), lambda i, j, s: (i, j)), out_shape=jax.ShapeDtypeStruct((ko, n), F32),
        scratch_shapes=[pltpu.VMEM((tko, tn), F32)], name=name,
        compiler_params=_cparams(("parallel", "parallel", "arbitrary")),
    )(a, b)


def merge_fwd(ys, w_branch, z):
    t = ys[0].shape[0]
    d = D_MODEL
    tm, tn = min(512, t), 512
    gate0 = Z_GATE // tn

    def body(y0, y1, y2, y3, w_ref, g0, g1, g2, g3, up_ref, mg_ref):
        acc = jnp.zeros((tm, tn), F32)
        for n, (y_ref, g_ref) in enumerate(zip((y0, y1, y2, y3), (g0, g1, g2, g3))):
            up = jnp.dot(y_ref[...], w_ref[n], preferred_element_type=F32)
            up_ref[n] = up
            acc = acc + jax.nn.sigmoid(g_ref[...]) * up
        mg_ref[...] = acc.astype(BF16)

    ysp = pl.BlockSpec((tm, WIDTH), lambda i, j: (i, 0))
    gate_specs = [pl.BlockSpec((tm, tn), functools.partial(lambda i, j, n: (i, gate0 + n * (d // tn) + j), n=n))
                  for n in range(N_BRANCH)]
    return pl.pallas_call(
        body, grid=(t // tm, d // tn),
        in_specs=[ysp, ysp, ysp, ysp, pl.BlockSpec((N_BRANCH, WIDTH, tn), lambda i, j: (0, 0, j))] + gate_specs,
        out_specs=[pl.BlockSpec((N_BRANCH, tm, tn), lambda i, j: (0, i, j)), pl.BlockSpec((tm, tn), lambda i, j: (i, j))],
        out_shape=[jax.ShapeDtypeStruct((N_BRANCH, t, d), F32), jax.ShapeDtypeStruct((t, d), BF16)],
        name="merge_fwd", compiler_params=_cparams(("parallel", "parallel")),
    )(*ys, w_branch, z, z, z, z)


def merge_bwd(dmerged, up, z):
    t, d = dmerged.shape
    tm, tn = min(512, t), 512
    gate0 = Z_GATE // tn
    nj = d // tn

    def body(dm_ref, up_ref, g_ref, dzg_ref, dup_ref):
        dm = dm_ref[...]
        s = jax.nn.sigmoid(g_ref[...])
        dzg_ref[...] = (dm * up_ref[0] * s * (1.0 - s)).astype(BF16)
        dup_ref[0] = (dm * s).astype(BF16)

    blk = pl.BlockSpec((1, tm, tn), lambda i, j, n: (n, i, j))
    return pl.pallas_call(
        body, grid=(t // tm, nj, N_BRANCH),
        in_specs=[pl.BlockSpec((tm, tn), lambda i, j, n: (i, j)), blk,
                  pl.BlockSpec((tm, tn), lambda i, j, n: (i, gate0 + n * nj + j))],
        out_specs=[pl.BlockSpec((tm, tn), lambda i, j, n: (i, n * nj + j)), blk],
        out_shape=[jax.ShapeDtypeStruct((t, N_BRANCH * d), BF16), jax.ShapeDtypeStruct((N_BRANCH, t, d), BF16)],
        name="merge_bwd", compiler_params=_cparams(("parallel", "parallel", "arbitrary")),
    )(dmerged, up, z)


def _sg_tile(zsg, g, w, bt):
    tt = zsg.shape[0]
    a = jax.nn.gelu(zsg)
    u, v = a[:, :WIDTH], a[:, WIDTH:]
    vc = v - jnp.mean(v, axis=-1, keepdims=True)
    v = (vc * lax.rsqrt(jnp.mean(vc * vc, axis=-1, keepdims=True) + NORM_EPS)) * g
    row = lax.broadcasted_iota(jnp.int32, (CHUNK, CHUNK), 0)
    col = lax.broadcasted_iota(jnp.int32, (CHUNK, CHUNK), 1)
    cols = []
    for gi in range(SG_GROUPS):
        wc = jnp.where(row >= col, w[gi], 0.0)
        bias = bt[:, gi:gi + 1]
        rows = [_mm(wc, v[ch * CHUNK:(ch + 1) * CHUNK, gi * CHUNK:(gi + 1) * CHUNK]) + bias for ch in range(tt // CHUNK)]
        cols.append(jnp.concatenate(rows, axis=0) if len(rows) > 1 else rows[0])
    return u * jnp.concatenate(cols, axis=1)


def sg_fwd(z, g, w, bt):
    t = z.shape[0]
    tt = _row_tile(t)

    def body(z_ref, g_ref, w_ref, b_ref, y_ref):
        y_ref[...] = _sg_tile(z_ref[...], g_ref[...], w_ref[...], b_ref[...]).astype(BF16)

    return pl.pallas_call(
        body, grid=(t // tt,),
        in_specs=[pl.BlockSpec((tt, 2 * WIDTH), lambda i: (i, 0)), pl.BlockSpec((1, WIDTH), lambda i: (0, 0)),
                  pl.BlockSpec((SG_GROUPS, CHUNK, CHUNK), lambda i: (0, 0, 0)), pl.BlockSpec((CHUNK, SG_GROUPS), lambda i: (0, 0))],
        out_specs=pl.BlockSpec((tt, WIDTH), lambda i: (i, 0)), out_shape=jax.ShapeDtypeStruct((t, WIDTH), BF16),
        name="sg_fwd", compiler_params=_cparams(("parallel",)),
    )(z, g, w, bt)


def sg_bwd(z, g, w, bt, dy):
    t = z.shape[0]
    tt = _row_tile(t)

    def body(z_ref, g_ref, w_ref, b_ref, dy_ref, dz_ref, dg_ref, dw_ref, db_ref):
        _, vjp = jax.vjp(_sg_tile, z_ref[...], g_ref[...], w_ref[...], b_ref[...])
        dz, dg, dw, db = vjp(dy_ref[...])
        dz_ref[...] = dz.astype(BF16)

        @pl.when(pl.program_id(0) == 0)
        def _():
            dg_ref[...] = jnp.zeros_like(dg_ref)
            dw_ref[...] = jnp.zeros_like(dw_ref)
            db_ref[...] = jnp.zeros_like(db_ref)

        dg_ref[...] += dg
        dw_ref[...] += dw
        db_ref[...] += db

    gs = pl.BlockSpec((1, WIDTH), lambda i: (0, 0))
    ws = pl.BlockSpec((SG_GROUPS, CHUNK, CHUNK), lambda i: (0, 0, 0))
    bs = pl.BlockSpec((CHUNK, SG_GROUPS), lambda i: (0, 0))
    return pl.pallas_call(
        body, grid=(t // tt,),
        in_specs=[pl.BlockSpec((tt, 2 * WIDTH), lambda i: (i, 0)), gs, ws, bs, pl.BlockSpec((tt, WIDTH), lambda i: (i, 0))],
        out_specs=[pl.BlockSpec((tt, 2 * WIDTH), lambda i: (i, 0)), gs, ws, bs],
        out_shape=[jax.ShapeDtypeStruct((t, 2 * WIDTH), BF16), jax.ShapeDtypeStruct((1, WIDTH), F32),
                   jax.ShapeDtypeStruct((SG_GROUPS, CHUNK, CHUNK), F32), jax.ShapeDtypeStruct((CHUNK, SG_GROUPS), F32)],
        name="sg_bwd", compiler_params=_cparams(("arbitrary",)),
    )(z, g, w, bt, dy)


HALO = 8


def _shift_down(halo, cur, k):
    tt = cur.shape[0]
    ext = jnp.concatenate([halo, cur], axis=0)
    return ext[HALO - k:HALO - k + tt]


def _shift_up(cur, halo, k):
    tt = cur.shape[0]
    ext = jnp.concatenate([cur, halo], axis=0)
    return ext[k:k + tt]


def conv_fwd(z, cw):
    t = z.shape[0]
    tt = _row_tile(t)
    cb = Z_CONV // WIDTH
    hb = tt // HALO

    def body(gb_ref, gc_ref, xv_ref, hgc_ref, hxv_ref, w_ref, y_ref):
        first = (pl.program_id(0) == 0)
        tcur = gc_ref[...] * xv_ref[...]
        thalo = jnp.where(first, 0.0, hgc_ref[...] * hxv_ref[...])
        w = w_ref[...]
        y = w[2:3] * tcur + w[1:2] * _shift_down(thalo, tcur, 1) + w[0:1] * _shift_down(thalo, tcur, 2)
        y_ref[...] = (gb_ref[...] * y).astype(BF16)

    def cur(off):
        return pl.BlockSpec((tt, WIDTH), lambda i: (i, cb + off))

    def prev(off):
        return pl.BlockSpec((HALO, WIDTH), lambda i: (jnp.maximum(i * hb - 1, 0), cb + off))

    return pl.pallas_call(
        body, grid=(t // tt,),
        in_specs=[cur(0), cur(1), cur(2), prev(1), prev(2), pl.BlockSpec((3, WIDTH), lambda i: (0, 0))],
        out_specs=pl.BlockSpec((tt, WIDTH), lambda i: (i, 0)), out_shape=jax.ShapeDtypeStruct((t, WIDTH), BF16),
        name="conv_fwd", compiler_params=_cparams(("parallel",)),
    )(z, z, z, z, z, cw)


def conv_bwd(z, cw, dy):
    t = z.shape[0]
    tt = _row_tile(t)
    nt = t // tt
    cb = Z_CONV // WIDTH
    hb = tt // HALO

    def body(gb_ref, gc_ref, xv_ref, hgc_ref, hxv_ref, ngb_ref, dy_ref, ndy_ref, w_ref, dz_ref, dw_ref):
        i = pl.program_id(0)
        gb, gc, xv = gb_ref[...], gc_ref[...], xv_ref[...]
        tcur = gc * xv
        thalo = jnp.where(i == 0, 0.0, hgc_ref[...] * hxv_ref[...])
        t1, t2 = _shift_down(thalo, tcur, 1), _shift_down(thalo, tcur, 2)
        w = w_ref[...]
        y = w[2:3] * tcur + w[1:2] * t1 + w[0:1] * t2
        dy = dy_ref[...]
        dyc = dy * gb
        dyn = jnp.where(i == nt - 1, 0.0, ndy_ref[...] * ngb_ref[...])
        dt = w[2:3] * dyc + w[1:2] * _shift_up(dyc, dyn, 1) + w[0:1] * _shift_up(dyc, dyn, 2)
        dz_ref[...] = jnp.concatenate([dy * y, dt * xv, dt * gc], axis=1).astype(BF16)

        @pl.when(i == 0)
        def _():
            dw_ref[...] = jnp.zeros_like(dw_ref)

        dw_ref[...] += jnp.concatenate([jnp.sum(dyc * t2, axis=0, keepdims=True), jnp.sum(dyc * t1, axis=0, keepdims=True),
                                        jnp.sum(dyc * tcur, axis=0, keepdims=True)], axis=0)

    def cur(off):
        return pl.BlockSpec((tt, WIDTH), lambda i: (i, cb + off))

    def prev(off):
        return pl.BlockSpec((HALO, WIDTH), lambda i: (jnp.maximum(i * hb - 1, 0), cb + off))

    last = t // HALO - 1
    nxt_z = pl.BlockSpec((HALO, WIDTH), lambda i: (jnp.minimum((i + 1) * hb, last), cb))
    nxt_dy = pl.BlockSpec((HALO, WIDTH), lambda i: (jnp.minimum((i + 1) * hb, last), 0))
    return pl.pallas_call(
        body, grid=(nt,),
        in_specs=[cur(0), cur(1), cur(2), prev(1), prev(2), nxt_z, pl.BlockSpec((tt, WIDTH), lambda i: (i, 0)), nxt_dy,
                  pl.BlockSpec((3, WIDTH), lambda i: (0, 0))],
        out_specs=[pl.BlockSpec((tt, 3 * WIDTH), lambda i: (i, 0)), pl.BlockSpec((3, WIDTH), lambda i: (0, 0))],
        out_shape=[jax.ShapeDtypeStruct((t, 3 * WIDTH), BF16), jax.ShapeDtypeStruct((3, WIDTH), F32)],
        name="conv_bwd", compiler_params=_cparams(("arbitrary",)),
    )(z, z, z, z, z, z, dy, dy, cw)


def _mla_prep(zm, cos, sin, gq, gkv, wuq, wukv):
    tt = zm.shape[0]
    cq, ckv, kpe = zm[:, :MLA_Q_RANK], zm[:, MLA_Q_RANK:MLA_Q_RANK + MLA_KV_RANK], zm[:, MLA_Q_RANK + MLA_KV_RANK:MLA_COLS]
    q = _mm(_rms(cq, gq), wuq)
    kv = _mm(_rms(ckv, gkv), wukv)
    half = MLA_ROPE // 2

    def rope(xr):
        x1, x2 = xr[:, :half], xr[:, half:]
        return jnp.concatenate([x1 * cos - x2 * sin, x2 * cos + x1 * sin], axis=-1)

    kpe = rope(kpe)
    pad = jnp.zeros((tt, MLA_QK_PAD - MLA_NOPE - MLA_ROPE), F32)
    dq, dkv = MLA_NOPE + MLA_ROPE, MLA_NOPE + MLA_V
    qs, ks, vs = [], [], []
    for h in range(HEADS):
        qs.append(jnp.concatenate([q[:, h * dq:h * dq + MLA_NOPE], rope(q[:, h * dq + MLA_NOPE:(h + 1) * dq]), pad], axis=-1))
        ks.append(jnp.concatenate([kv[:, h * dkv:h * dkv + MLA_NOPE], kpe, pad], axis=-1))
        vs.append(kv[:, h * dkv + MLA_NOPE:(h + 1) * dkv])
    return qs, ks, vs


def _mla_prep_specs(tt):
    zs = pl.BlockSpec((tt, WIDTH), lambda i: (i, Z_MLA // WIDTH))
    cs = pl.BlockSpec((tt, MLA_ROPE // 2), lambda i: (i, 0))
    return [zs, cs, cs, pl.BlockSpec((1, MLA_Q_RANK), lambda i: (0, 0)), pl.BlockSpec((1, MLA_KV_RANK), lambda i: (0, 0)),
            pl.BlockSpec((MLA_Q_RANK, HEADS * (MLA_NOPE + MLA_ROPE)), lambda i: (0, 0)),
            pl.BlockSpec((MLA_KV_RANK, HEADS * (MLA_NOPE + MLA_V)), lambda i: (0, 0))]


def mla_prep_fwd(z, cos, sin, gq, gkv, wuq, wukv):
    t = z.shape[0]
    tt = _row_tile(t)

    def body(z_ref, c_ref, s_ref, gq_ref, gkv_ref, wq_ref, wkv_ref, q_ref, k_ref, v_ref):
        qs, ks, vs = _mla_prep(z_ref[...], c_ref[...], s_ref[...], gq_ref[...], gkv_ref[...],
                               wq_ref[...].astype(F32), wkv_ref[...].astype(F32))
        for h in range(HEADS):
            q_ref[h] = qs[h].astype(BF16)
            k_ref[h] = ks[h].astype(BF16)
            v_ref[h] = vs[h].astype(BF16)

    hs = pl.BlockSpec((HEADS, tt, MLA_QK_PAD), lambda i: (0, i, 0))
    vsp = pl.BlockSpec((HEADS, tt, MLA_V), lambda i: (0, i, 0))
    return pl.pallas_call(
        body, grid=(t // tt,), in_specs=_mla_prep_specs(tt), out_specs=[hs, hs, vsp],
        out_shape=[jax.ShapeDtypeStruct((HEADS, t, MLA_QK_PAD), BF16), jax.ShapeDtypeStruct((HEADS, t, MLA_QK_PAD), BF16),
                   jax.ShapeDtypeStruct((HEADS, t, MLA_V), BF16)],
        name="mla_prep_fwd", compiler_params=_cparams(("parallel",)),
    )(z, cos, sin, gq, gkv, wuq, wukv)


def mla_prep_bwd(z, cos, sin, gq, gkv, wuq, wukv, dq, dk, dv):
    t = z.shape[0]
    tt = _row_tile(t)

    def body(z_ref, c_ref, s_ref, gq_ref, gkv_ref, wq_ref, wkv_ref, dq_ref, dk_ref, dv_ref,
             dz_ref, dgq_ref, dgkv_ref, dwq_ref, dwkv_ref):
        cos_v, sin_v = c_ref[...], s_ref[...]
        _, vjp = jax.vjp(lambda zm, a, b, wq, wkv: _mla_prep(zm, cos_v, sin_v, a, b, wq, wkv),
                         z_ref[...], gq_ref[...], gkv_ref[...], wq_ref[...].astype(F32), wkv_ref[...].astype(F32))
        cot = ([dq_ref[h] for h in range(HEADS)], [dk_ref[h] for h in range(HEADS)], [dv_ref[h] for h in range(HEADS)])
        dz, dgq, dgkv, dwq, dwkv = vjp(cot)
        dz_ref[...] = dz.astype(BF16)

        @pl.when(pl.program_id(0) == 0)
        def _():
            for r in (dgq_ref, dgkv_ref, dwq_ref, dwkv_ref):
                r[...] = jnp.zeros_like(r)

        dgq_ref[...] += dgq
        dgkv_ref[...] += dgkv
        dwq_ref[...] += dwq
        dwkv_ref[...] += dwkv

    specs = _mla_prep_specs(tt)
    hs = pl.BlockSpec((HEADS, tt, MLA_QK_PAD), lambda i: (0, i, 0))
    vsp = pl.BlockSpec((HEADS, tt, MLA_V), lambda i: (0, i, 0))
    nq, nkv = HEADS * (MLA_NOPE + MLA_ROPE), HEADS * (MLA_NOPE + MLA_V)
    return pl.pallas_call(
        body, grid=(t // tt,), in_specs=specs + [hs, hs, vsp],
        out_specs=[pl.BlockSpec((tt, WIDTH), lambda i: (i, 0)), specs[3], specs[4], specs[5], specs[6]],
        out_shape=[jax.ShapeDtypeStruct((t, WIDTH), BF16), jax.ShapeDtypeStruct((1, MLA_Q_RANK), F32),
                   jax.ShapeDtypeStruct((1, MLA_KV_RANK), F32), jax.ShapeDtypeStruct((MLA_Q_RANK, nq), F32),
                   jax.ShapeDtypeStruct((MLA_KV_RANK, nkv), F32)],
        name="mla_prep_bwd", compiler_params=_cparams(("arbitrary",)),
    )(z, cos, sin, gq, gkv, wuq, wukv, dq, dk, dv)


def _key_blocks(a, tk):
    h, t, d = a.shape
    return a.reshape(h, t // tk, tk, d).transpose(0, 1, 3, 2)


def _att_masks(kb, qi, blk):
    krow = kb * blk + lax.broadcasted_iota(jnp.int32, (blk, blk), 0)
    qcol = qi * blk + lax.broadcasted_iota(jnp.int32, (blk, blk), 1)
    return krow, qcol


def softmax_att_fwd(qt, k, vtb, scale):
    hh, dk, t = qt.shape
    dv = vtb.shape[2]
    blk = min(ATT_BLOCK, t)

    def body(qt_ref, k_ref, vt_ref, o_ref, lse_ref):
        qi = pl.program_id(1)
        qtv = qt_ref[0]

        def step(kb, carry, masked):
            m, l, acc = carry
            ks = pl.multiple_of(kb * blk, blk)
            s = jnp.dot(k_ref[0, pl.ds(ks, blk), :], qtv, preferred_element_type=F32) * scale
            if masked:
                krow, qcol = _att_masks(kb, qi, blk)
                s = jnp.where(krow <= qcol, s, NEG_BIG)
            m_new = jnp.maximum(m, jnp.max(s, axis=0, keepdims=True))
            p = jnp.exp(s - m_new)
            alpha = jnp.exp(m - m_new)
            l = alpha * l + jnp.sum(p, axis=0, keepdims=True)
            acc = alpha * acc + jnp.dot(vt_ref[0, kb], p.astype(BF16), preferred_element_type=F32)
            return m_new, l, acc

        init = (jnp.full((1, blk), NEG_BIG, F32), jnp.zeros((1, blk), F32), jnp.zeros((dv, blk), F32))
        carry = lax.fori_loop(0, qi, lambda kb, c: step(kb, c, False), init)
        m, l, acc = step(qi, carry, True)
        o_ref[0] = acc / l
        lse_ref[0] = m + jnp.log(l)

    return pl.pallas_call(
        body, grid=(hh, t // blk),
        in_specs=[pl.BlockSpec((1, dk, blk), lambda h, i: (h, 0, i)), pl.BlockSpec((1, t, dk), lambda h, i: (h, 0, 0)),
                  pl.BlockSpec((1, t // blk, dv, blk), lambda h, i: (h, 0, 0, 0))],
        out_specs=[pl.BlockSpec((1, dv, blk), lambda h, i: (h, 0, i)), pl.BlockSpec((1, 1, blk), lambda h, i: (h, 0, i))],
        out_shape=[jax.ShapeDtypeStruct((hh, dv, t), F32), jax.ShapeDtypeStruct((hh, 1, t), F32)],
        name="softmax_att_fwd", compiler_params=_cparams(("parallel", "arbitrary")),
    )(qt, k, vtb)


def softmax_att_bwd(qt, q, k, ktb, v, ot, lse, do, dot_, scale):
    hh, dk, t = qt.shape
    dv = v.shape[2]
    blk = min(ATT_BLOCK, t)

    def body(qt_ref, q_ref, k_ref, kt_ref, v_ref, ot_ref, lse_ref, do_ref, dot_ref, dqt_ref, dk_ref, dv_ref):
        qi = pl.program_id(1)

        @pl.when(qi == 0)
        def _():
            dk_ref[...] = jnp.zeros_like(dk_ref)
            dv_ref[...] = jnp.zeros_like(dv_ref)

        qtv, qv, dov, dotv = qt_ref[0], q_ref[0], do_ref[0], dot_ref[0]
        lse_v = lse_ref[0]
        delta = jnp.sum(dotv.astype(F32) * ot_ref[0], axis=0, keepdims=True)

        def step(kb, dqt, masked):
            ks = pl.multiple_of(kb * blk, blk)
            s = jnp.dot(k_ref[0, pl.ds(ks, blk), :], qtv, preferred_element_type=F32) * scale
            if masked:
                krow, qcol = _att_masks(kb, qi, blk)
                s = jnp.where(krow <= qcol, s, NEG_BIG)
            p = jnp.exp(s - lse_v)
            dp = jnp.dot(v_ref[0, pl.ds(ks, blk), :], dotv, preferred_element_type=F32)
            ds = (p * (dp - delta) * scale).astype(BF16)
            dk_ref[0, pl.ds(ks, blk), :] += jnp.dot(ds, qv, preferred_element_type=F32)
            dv_ref[0, pl.ds(ks, blk), :] += jnp.dot(p.astype(BF16), dov, preferred_element_type=F32)
            return dqt + jnp.dot(kt_ref[0, kb], ds, preferred_element_type=F32)

        dqt = lax.fori_loop(0, qi, lambda kb, c: step(kb, c, False), jnp.zeros((dk, blk), F32))
        dqt_ref[0] = step(qi, dqt, True)

    nkb = t // blk
    qts = pl.BlockSpec((1, dk, blk), lambda h, i: (h, 0, i))
    qs = pl.BlockSpec((1, blk, dk), lambda h, i: (h, i, 0))
    kfull = pl.BlockSpec((1, t, dk), lambda h, i: (h, 0, 0))
    vfull = pl.BlockSpec((1, t, dv), lambda h, i: (h, 0, 0))
    vts = pl.BlockSpec((1, dv, blk), lambda h, i: (h, 0, i))
    return pl.pallas_call(
        body, grid=(hh, nkb),
        in_specs=[qts, qs, kfull, pl.BlockSpec((1, nkb, dk, blk), lambda h, i: (h, 0, 0, 0)), vfull, vts,
                  pl.BlockSpec((1, 1, blk), lambda h, i: (h, 0, i)), pl.BlockSpec((1, blk, dv), lambda h, i: (h, i, 0)), vts],
        out_specs=[qts, kfull, vfull],
        out_shape=[jax.ShapeDtypeStruct((hh, dk, t), F32), jax.ShapeDtypeStruct((hh, t, dk), F32),
                   jax.ShapeDtypeStruct((hh, t, dv), F32)],
        name="softmax_att_bwd", compiler_params=_cparams(("parallel", "arbitrary")),
    )(qt, q, k, ktb, v, ot, lse, do, dot_)


def _softplus(z):
    return jnp.maximum(z, 0.0) + jnp.log1p(jnp.exp(-jnp.abs(z)))


def _strict_upper(blk):
    r = lax.broadcasted_iota(jnp.int32, (blk, blk), 0)
    c = lax.broadcasted_iota(jnp.int32, (blk, blk), 1)
    return jnp.where(c > r, 1.0, 0.0).astype(BF16)


def _strict_lower(blk):
    r = lax.broadcasted_iota(jnp.int32, (blk, blk), 0)
    c = lax.broadcasted_iota(jnp.int32, (blk, blk), 1)
    return jnp.where(c < r, 1.0, 0.0).astype(BF16)


def stick_att_fwd(qt, k, vtb, scale):
    hh, dk, t = qt.shape
    dv = vtb.shape[2]
    blk = min(ATT_BLOCK, t)
    nkb = t // blk

    def body(qt_ref, k_ref, vt_ref, o_ref, rs_ref):
        qi = pl.program_id(1)
        qtv = qt_ref[0]
        upper = _strict_upper(blk)

        def step(kb, carry, masked):
            r, acc = carry
            ks = pl.multiple_of(kb * blk, blk)
            z = jnp.dot(k_ref[0, pl.ds(ks, blk), :], qtv, preferred_element_type=F32) * scale
            sp = _softplus(z)
            l = -sp
            if masked:
                krow, qcol = _att_masks(kb, qi, blk)
                mask = krow < qcol
                l = jnp.where(mask, l, 0.0)
            rs_ref[0, kb] = r
            a = jnp.exp(z - sp + r + _split_dot(upper, l))
            if masked:
                a = jnp.where(mask, a, 0.0)
            acc = acc + jnp.dot(vt_ref[0, kb], a.astype(BF16), preferred_element_type=F32)
            return r + jnp.sum(l, axis=0, keepdims=True), acc

        carry = step(qi, (jnp.zeros((1, blk), F32), jnp.zeros((dv, blk), F32)), True)
        _, acc = lax.fori_loop(0, qi, lambda i, c: step(qi - 1 - i, c, False), carry)
        o_ref[0] = acc

    return pl.pallas_call(
        body, grid=(hh, nkb),
        in_specs=[pl.BlockSpec((1, dk, blk), lambda h, i: (h, 0, i)), pl.BlockSpec((1, t, dk), lambda h, i: (h, 0, 0)),
                  pl.BlockSpec((1, nkb, dv, blk), lambda h, i: (h, 0, 0, 0))],
        out_specs=[pl.BlockSpec((1, dv, blk), lambda h, i: (h, 0, i)), pl.BlockSpec((1, nkb, 1, blk), lambda h, i: (h, 0, 0, i))],
        out_shape=[jax.ShapeDtypeStruct((hh, dv, t), F32), jax.ShapeDtypeStruct((hh, nkb, 1, t), F32)],
        name="stick_att_fwd", compiler_params=_cparams(("parallel", "arbitrary")),
    )(qt, k, vtb)


def stick_att_bwd(qt, q, k, ktb, v, rs, do, dot_, scale):
    hh, dk, t = qt.shape
    dv = v.shape[2]
    blk = min(ATT_BLOCK, t)
    nkb = t // blk

    def body(qt_ref, q_ref, k_ref, kt_ref, v_ref, rs_ref, do_ref, dot_ref, dqt_ref, dk_ref, dv_ref):
        qi = pl.program_id(1)

        @pl.when(qi == 0)
        def _():
            dk_ref[...] = jnp.zeros_like(dk_ref)
            dv_ref[...] = jnp.zeros_like(dv_ref)

        qtv, qv, dov, dotv = qt_ref[0], q_ref[0], do_ref[0], dot_ref[0]
        upper, lower = _strict_upper(blk), _strict_lower(blk)

        def step(kb, carry, masked):
            pre, dqt = carry
            ks = pl.multiple_of(kb * blk, blk)
            z = jnp.dot(k_ref[0, pl.ds(ks, blk), :], qtv, preferred_element_type=F32) * scale
            sp = _softplus(z)
            l = -sp
            if masked:
                krow, qcol = _att_masks(kb, qi, blk)
                mask = krow < qcol
                l = jnp.where(mask, l, 0.0)
            la = z - sp
            a = jnp.exp(la + rs_ref[0, kb] + _split_dot(upper, l))
            if masked:
                a = jnp.where(mask, a, 0.0)
            beta = jnp.exp(la)
            g = a * jnp.dot(v_ref[0, pl.ds(ks, blk), :], dotv, preferred_element_type=F32)
            dz = g * (1.0 - beta) - beta * (pre + _split_dot(lower, g))
            if masked:
                dz = jnp.where(mask, dz, 0.0)
            dzb = (dz * scale).astype(BF16)
            dk_ref[0, pl.ds(ks, blk), :] += jnp.dot(dzb, qv, preferred_element_type=F32)
            dv_ref[0, pl.ds(ks, blk), :] += jnp.dot(a.astype(BF16), dov, preferred_element_type=F32)
            return pre + jnp.sum(g, axis=0, keepdims=True), dqt + jnp.dot(kt_ref[0, kb], dzb, preferred_element_type=F32)

        carry = lax.fori_loop(0, qi, lambda kb, c: step(kb, c, False), (jnp.zeros((1, blk), F32), jnp.zeros((dk, blk), F32)))
        dqt_ref[0] = step(qi, carry, True)[1]

    qts = pl.BlockSpec((1, dk, blk), lambda h, i: (h, 0, i))
    qs = pl.BlockSpec((1, blk, dk), lambda h, i: (h, i, 0))
    kfull = pl.BlockSpec((1, t, dk), lambda h, i: (h, 0, 0))
    vfull = pl.BlockSpec((1, t, dv), lambda h, i: (h, 0, 0))
    return pl.pallas_call(
        body, grid=(hh, nkb),
        in_specs=[qts, qs, kfull, pl.BlockSpec((1, nkb, dk, blk), lambda h, i: (h, 0, 0, 0)), vfull,
                  pl.BlockSpec((1, nkb, 1, blk), lambda h, i: (h, 0, 0, i)), pl.BlockSpec((1, blk, dv), lambda h, i: (h, i, 0)),
                  pl.BlockSpec((1, dv, blk), lambda h, i: (h, 0, i))],
        out_specs=[qts, kfull, vfull],
        out_shape=[jax.ShapeDtypeStruct((hh, dk, t), F32), jax.ShapeDtypeStruct((hh, t, dk), F32),
                   jax.ShapeDtypeStruct((hh, t, dv), F32)],
        name="stick_att_bwd", compiler_params=_cparams(("parallel", "arbitrary")),
    )(qt, q, k, ktb, v, rs, do, dot_)


def mod_fwd(c_all, ada_w, ada_b_cols):
    nl, d, n = ada_w.shape

    def body(c_ref, w_ref, b_ref, o_ref):
        act = jax.nn.silu(c_ref[...])
        o_ref[0] = jnp.dot(act, w_ref[0], precision=lax.Precision.HIGHEST, preferred_element_type=F32) + b_ref[0]

    return pl.pallas_call(
        body, grid=(nl,),
        in_specs=[pl.BlockSpec((N_DEV, d), lambda l: (0, 0)), pl.BlockSpec((1, d, n), lambda l: (l, 0, 0)),
                  pl.BlockSpec((1, 1, n), lambda l: (l, 0, 0))],
        out_specs=pl.BlockSpec((1, N_DEV, n), lambda l: (l, 0, 0)), out_shape=jax.ShapeDtypeStruct((nl, N_DEV, n), F32),
        name="mod_fwd", compiler_params=_cparams(("parallel",)),
    )(c_all, ada_w, ada_b_cols)


def ada_w_grad(c_all_t, dmod_cols):
    d = c_all_t.shape[0]
    nl, _, n = dmod_cols.shape

    def body(c_ref, dm_ref, o_ref):
        act = jax.nn.silu(c_ref[...])
        dm = dm_ref[0]
        acc = act[:, 0:1] * dm[0:1, :]
        for e in range(1, N_DEV):
            acc = acc + act[:, e:e + 1] * dm[e:e + 1, :]
        o_ref[0] = acc

    return pl.pallas_call(
        body, grid=(nl,),
        in_specs=[pl.BlockSpec((d, N_DEV), lambda l: (0, 0)), pl.BlockSpec((1, N_DEV, n), lambda l: (l, 0, 0))],
        out_specs=pl.BlockSpec((1, d, n), lambda l: (l, 0, 0)), out_shape=jax.ShapeDtypeStruct((nl, d, n), F32),
        name="ada_w_grad", compiler_params=_cparams(("parallel",)),
    )(c_all_t, dmod_cols)


LANES = 1024
ADAM_ROWS = 256


def adamw(slots, w, m, v, *, name):
    ns, r, c = slots.shape
    tr = ADAM_ROWS if r % ADAM_ROWS == 0 else r

    def body(s_ref, w_ref, m_ref, v_ref, g_ref, d_ref, mo_ref, vo_ref):
        g = s_ref[0].astype(F32)
        for i in range(1, ns):
            g = g + s_ref[i].astype(F32)
        mn = ADAM_B1 * m_ref[...] + (1.0 - ADAM_B1) * g
        vn = ADAM_B2 * v_ref[...] + (1.0 - ADAM_B2) * jnp.square(g)
        m_hat = mn / (1.0 - ADAM_B1 ** ADAM_STEP)
        v_hat = vn / (1.0 - ADAM_B2 ** ADAM_STEP)
        g_ref[...] = g
        d_ref[...] = -ADAM_LR * (m_hat / (jnp.sqrt(v_hat) + ADAM_EPS) + ADAM_WD * w_ref[...])
        mo_ref[...] = mn
        vo_ref[...] = vn

    row = pl.BlockSpec((tr, c), lambda i: (i, 0))
    return pl.pallas_call(
        body, grid=(r // tr,), in_specs=[pl.BlockSpec((ns, tr, c), lambda i: (0, i, 0)), row, row, row],
        out_specs=[row] * 4, out_shape=[jax.ShapeDtypeStruct((r, c), F32)] * 4, name=name,
        compiler_params=_cparams(("parallel",)),
    )(slots, w, m, v)


def _exchange(src, *, gather, name):
    shape = src.shape if gather else src.shape[1:]

    def body(src_ref, out_ref, send_sems, recv_sems, local_sem):
        mx, my, mc = lax.axis_index("x"), lax.axis_index("y"), lax.axis_index("c")
        me = 4 * mx + 2 * my + mc

        def part(idx):
            return src_ref if gather else src_ref.at[idx]

        local = pltpu.make_async_copy(part(me), out_ref.at[me], local_sem)
        local.start()
        peers = []
        for k in range(1, N_DEV):
            px = 1 - mx if k & 4 else mx
            py = 1 - my if k & 2 else my
            pc = 1 - mc if k & 1 else mc
            peers.append(((px, py, pc), 4 * px + 2 * py + pc))

        def copy(k, slot):
            dev, pid = peers[k]
            return pltpu.make_async_remote_copy(src_ref=part(pid), dst_ref=out_ref.at[slot], send_sem=send_sems.at[k],
                                                recv_sem=recv_sems.at[k], device_id=dev, device_id_type=MESH)

        for k in range(N_DEV - 1):
            copy(k, me).start()
        for k in range(N_DEV - 1):
            copy(k, me).wait_send()
            copy(k, peers[k][1]).wait_recv()
        local.wait()

    return pl.pallas_call(
        body, in_specs=[pl.BlockSpec(memory_space=pl.ANY)], out_specs=pl.BlockSpec(memory_space=pl.ANY),
        out_shape=jax.ShapeDtypeStruct((N_DEV,) + tuple(shape), src.dtype),
        scratch_shapes=[pltpu.SemaphoreType.DMA((N_DEV - 1,)), pltpu.SemaphoreType.DMA((N_DEV - 1,)), pltpu.SemaphoreType.DMA(())],
        name=name, compiler_params=pltpu.CompilerParams(has_side_effects=True),
    )(src)


def _padded(n, row_mult):
    unit = LANES * row_mult
    return -(-n // unit) * unit


def _pack(parts, dtype, row_mult):
    flat = jnp.concatenate([p.reshape(-1).astype(dtype) for p in parts])
    total = _padded(flat.shape[0], row_mult)
    return jnp.pad(flat, (0, total - flat.shape[0])).reshape(total // LANES, LANES)


def _pack_blocks(parts, dtype, row_mult):
    cat = jnp.concatenate([p.astype(dtype) for p in parts], axis=1)
    total = _padded(cat.shape[1], row_mult)
    return jnp.pad(cat, ((0, 0), (0, total - cat.shape[1]))).reshape(N_DEV, total // LANES, LANES)


def _unpack(flat2d, shapes, lead=()):
    flat = flat2d.reshape(lead + (-1,))
    out, off = [], 0
    for shp in shapes:
        n = math.prod(shp)
        out.append(flat[..., off:off + n].reshape(lead + tuple(shp)))
        off += n
    return out


def _to_blocks(full, ax):
    shp = full.shape
    g = full.reshape(shp[:ax] + (N_DEV, shp[ax] // N_DEV) + shp[ax + 1:])
    return jnp.moveaxis(g, ax, 0).reshape(N_DEV, -1)


def _from_blocks(seg, ax):
    g = jnp.moveaxis(seg, 0, ax)
    shp = g.shape
    return g.reshape(shp[:ax] + (shp[ax] * shp[ax + 1],) + shp[ax + 2:])


def _to_heads(a, dh):
    return a.reshape(a.shape[0], -1, dh).transpose(1, 0, 2)


def _from_heads(a):
    return a.transpose(1, 0, 2).reshape(a.shape[1], -1)


def _from_heads_t(a):
    return a.transpose(2, 0, 1).reshape(a.shape[2], -1)


MLA_SCALE = (MLA_NOPE + MLA_ROPE) ** -0.5
SB_SCALE = SB_DIM ** -0.5


def _resid_epi(acc, x, g):
    return x + g * acc, acc


def _layer_fwd(x, mod, p, cos, sin):
    t = x.shape[0]
    blk = min(ATT_BLOCK, t)
    sh1, sc1, g1, sh2, sc2, g2 = [mod[i:i + 1] for i in range(6)]
    h = normmod_fwd(x, p["norm1_g"], sc1, sh1)
    z = matmul(h, p["w_in"], name="mm_in")
    y_sg = sg_fwd(z, p["sg_norm_g"], p["sg_w"], p["sg_bt"])
    y_cv = conv_fwd(z, p["conv_w"])
    mq, mk, mv = mla_prep_fwd(z, cos, sin, p["gq"], p["gkv"], p["wuq"], p["wukv"])
    mot, lse = softmax_att_fwd(mq.transpose(0, 2, 1), mk, _key_blocks(mv, blk), MLA_SCALE)
    y_mla = _from_heads_t(mot).astype(BF16)
    sq, sk, sv = [_to_heads(z[:, Z_SB + i * WIDTH:Z_SB + (i + 1) * WIDTH], SB_DIM).astype(BF16) for i in range(3)]
    sot, rs = stick_att_fwd(sq.transpose(0, 2, 1), sk, _key_blocks(sv, blk), SB_SCALE)
    y_sb = _from_heads_t(sot).astype(BF16)
    ys = (y_sg, y_cv, y_mla, y_sb)
    up, merged = merge_fwd(ys, p["w_branch"], z)
    x1, out = matmul(merged, p["w_out"], name="mm_out", out_dtypes=(F32, F32), epi=_resid_epi, epi_in=((x, "mn"), (g1, "n")))
    h2 = normmod_fwd(x1, p["norm2_g"], sc2, sh2)
    a, r = matmul(h2, p["w1"], name="mm_up", out_dtypes=(F32, BF16),
                  epi=lambda acc: (acc, jnp.square(jnp.maximum(acc, 0.0))))
    x2, mo = matmul(r, p["w2"], name="mm_down", out_dtypes=(F32, F32), epi=_resid_epi, epi_in=((x1, "mn"), (g2, "n")))
    saved = dict(x=x, h=h, z=z, ys=ys, mq=mq, mk=mk, mv=mv, mot=mot, lse=lse, sq=sq, sk=sk, sv=sv, rs=rs, up=up,
                 merged=merged, out=out, x1=x1, h2=h2, a=a, r=r, mo=mo)
    return x2, saved


def _layer_bwd(dx2, s, mod, p, cos, sin):
    t = dx2.shape[0]
    blk = min(ATT_BLOCK, t)
    sh1, sc1, g1, sh2, sc2, g2 = [mod[i:i + 1] for i in range(6)]
    g = {}
    dm, dg2 = resid_bwd(dx2, s["mo"], g2)
    da = matmul(dm, p["w2_t"], name="mm_down_dx", out_dtypes=(BF16,), epi=lambda acc, a: (acc * (2.0 * jnp.maximum(a, 0.0)),),
                epi_in=((s["a"], "mn"),))
    g["mlp_w2"] = matmul_tn(s["r"], dm, name="mm_down_dw")
    dh2 = matmul(da, p["w1_t"], name="mm_up_dx")
    g["mlp_w1"] = matmul_tn(s["h2"], da, name="mm_up_dw")
    dx1, g["norm2_g"], dsc2, dsh2 = normmod_bwd(s["x1"], p["norm2_g"], sc2, sh2, dh2, dx2)
    dout, dg1 = resid_bwd(dx1, s["out"], g1)
    dmerged = matmul(dout, p["w_out_t"], name="mm_out_dx")
    g["w_out"] = matmul_tn(s["merged"], dout, name="mm_out_dw")
    dzg, dup = merge_bwd(dmerged, s["up"], s["z"])
    dys = [matmul(dup[n], p["w_branch_t"][n], name="mm_branch_dx") for n in range(N_BRANCH)]
    g["w_branch"] = jnp.stack([matmul_tn(s["ys"][n], dup[n], name="mm_branch_dw") for n in range(N_BRANCH)])
    z = s["z"]
    dz_sg, g["sg_norm_g"], g["sg_w"], dbt = sg_bwd(z, p["sg_norm_g"], p["sg_w"], p["sg_bt"], dys[0])
    g["sg_b"] = dbt.T
    dz_cv, g["conv_w"] = conv_bwd(z, p["conv_w"], dys[1])
    do = _to_heads(dys[2], MLA_V).astype(BF16)
    dqt, dk, dv = softmax_att_bwd(s["mq"].transpose(0, 2, 1), s["mq"], s["mk"], _key_blocks(s["mk"], blk), s["mv"], s["mot"],
                                  s["lse"], do, do.transpose(0, 2, 1), MLA_SCALE)
    dz_mla, g["mla_q_norm_g"], g["mla_kv_norm_g"], g["mla_w_uq"], g["mla_w_ukv"] = mla_prep_bwd(
        z, cos, sin, p["gq"], p["gkv"], p["wuq"], p["wukv"], dqt.transpose(0, 2, 1), dk, dv)
    do = _to_heads(dys[3], SB_DIM).astype(BF16)
    dqt, dk, dv = stick_att_bwd(s["sq"].transpose(0, 2, 1), s["sq"], s["sk"], _key_blocks(s["sk"], blk), s["sv"], s["rs"],
                                do, do.transpose(0, 2, 1), SB_SCALE)
    dz_sb = jnp.concatenate([_from_heads_t(dqt), _from_heads(dk), _from_heads(dv)], axis=1).astype(BF16)
    dz = jnp.concatenate([dz_sg, dz_cv, dz_mla, dz_sb, dzg], axis=1)
    dh = matmul(dz, p["w_in_t"], name="mm_in_dx", tk=2176)
    dw_in = matmul_tn(s["h"], dz, name="mm_in_dw")
    g["w_in"] = jnp.concatenate([dw_in[:, :Z_MLA + MLA_COLS], dw_in[:, Z_SB:]], axis=1)
    dx, g["norm1_g"], dsc1, dsh1 = normmod_bwd(s["x"], p["norm1_g"], sc1, sh1, dh, dx1)
    g["mla_q_norm_g"], g["mla_kv_norm_g"] = g["mla_q_norm_g"][0], g["mla_kv_norm_g"][0]
    g["norm1_g"], g["norm2_g"], g["sg_norm_g"] = g["norm1_g"][0], g["norm2_g"][0], g["sg_norm_g"][0]
    g["ada_b"] = jnp.concatenate([dsh1, dsc1, dg1, dsh2, dsc2, dg2], axis=1)[0]
    return dx, g


WEIGHT_NAMES = ("ada_w", "ada_b", "norm1_g", "norm2_g", "w_in", "sg_norm_g", "sg_w", "sg_b", "conv_w", "mla_q_norm_g",
                "mla_w_uq", "mla_kv_norm_g", "mla_w_ukv", "w_branch", "w_out", "mlp_w1", "mlp_w2", "final_norm_g")
SHARDED = (("w_in", 2), ("mla_w_uq", 2), ("mla_w_ukv", 2), ("w_branch", 3), ("w_out", 1), ("mlp_w1", 2), ("mlp_w2", 1),
           ("conv_w", 2))
REPLICATED = ("ada_b", "norm1_g", "norm2_g", "sg_norm_g", "sg_w", "sg_b", "mla_q_norm_g", "mla_kv_norm_g", "final_norm_g")


def kernel(x, c, positions, ada_w, ada_b, norm1_g, norm2_g, w_in, sg_norm_g, sg_w, sg_b, conv_w, mla_q_norm_g, mla_w_uq, mla_kv_norm_g, mla_w_ukv, w_branch, w_out, mlp_w1, mlp_w2, final_norm_g, loss_target, m_ada_w, m_ada_b, m_norm1_g, m_norm2_g, m_w_in, m_sg_norm_g, m_sg_w, m_sg_b, m_conv_w, m_mla_q_norm_g, m_mla_w_uq, m_mla_kv_norm_g, m_mla_w_ukv, m_w_branch, m_w_out, m_mlp_w1, m_mlp_w2, m_final_norm_g, v_ada_w, v_ada_b, v_norm1_g, v_norm2_g, v_w_in, v_sg_norm_g, v_sg_w, v_sg_b, v_conv_w, v_mla_q_norm_g, v_mla_w_uq, v_mla_kv_norm_g, v_mla_w_ukv, v_w_branch, v_w_out, v_mlp_w1, v_mlp_w2, v_final_norm_g):
    given = dict(locals())
    w = {n: given[n] for n in WEIGHT_NAMES}
    m = {n: given["m_" + n] for n in WEIGHT_NAMES}
    v = {n: given["v_" + n] for n in WEIGHT_NAMES}
    xs, tgt = x[0], loss_target[0]
    nl = ada_w.shape[0]
    me = 4 * lax.axis_index("x") + 2 * lax.axis_index("y") + lax.axis_index("c")

    inv_freq = ROPE_THETA ** (-jnp.arange(0, MLA_ROPE, 2, dtype=F32) / MLA_ROPE)
    ang = positions[0].astype(F32)[:, None] * inv_freq
    cos, sin = jnp.cos(ang), jnp.sin(ang)

    got = _exchange(_pack([c, conv_w], F32, 8), gather=True, name="gather_c_conv")
    c_all, conv_blocks = _unpack(got, [(D_MODEL,), conv_w.shape], lead=(N_DEV,))
    conv_full = _from_blocks(conv_blocks, 2)
    ncol = ada_w.shape[2]
    ada_b_cols = lax.dynamic_slice_in_dim(ada_b, me * ncol, ncol, axis=1)[:, None, :]
    mod_cols = mod_fwd(c_all, ada_w, ada_b_cols)
    got = _exchange(mod_cols.reshape(nl * N_DEV, ncol), gather=True, name="gather_mod").reshape(N_DEV, nl, N_DEV, ncol)
    mod = jnp.moveaxis(lax.dynamic_index_in_dim(got, me, axis=2, keepdims=False), 0, 1).reshape(nl, 6, D_MODEL)

    names = [n for n, _ in SHARDED if n != "conv_w"]
    got = _exchange(_pack([w[n] for n in names], BF16, 16), gather=True, name="gather_weights")
    full = {n: _from_blocks(seg, ax) for (n, ax), seg in
            zip(SHARDED, _unpack(got, [w[n].shape for n in names], lead=(N_DEV,)))}
    cut = Z_MLA + MLA_COLS
    w_in_pad = jnp.concatenate([full["w_in"][:, :, :cut], jnp.zeros((nl, D_MODEL, Z_SB - cut), BF16), full["w_in"][:, :, cut:]], axis=2)
    layers = []
    for l in range(nl):
        layers.append(dict(
            norm1_g=norm1_g[l][None], norm2_g=norm2_g[l][None], sg_norm_g=sg_norm_g[l][None], sg_w=sg_w[l], sg_bt=sg_b[l].T,
            conv_w=conv_full[l], gq=mla_q_norm_g[l][None], gkv=mla_kv_norm_g[l][None], wuq=full["mla_w_uq"][l],
            wukv=full["mla_w_ukv"][l], w_in=w_in_pad[l], w_in_t=w_in_pad[l].T, w_branch=full["w_branch"][l],
            w_branch_t=full["w_branch"][l].transpose(0, 2, 1), w_out=full["w_out"][l], w_out_t=full["w_out"][l].T,
            w1=full["mlp_w1"][l], w1_t=full["mlp_w1"][l].T, w2=full["mlp_w2"][l], w2_t=full["mlp_w2"][l].T))

    saved = []
    xc = xs
    for l in range(nl):
        xc, s = _layer_fwd(xc, mod[l], layers[l], cos, sin)
        saved.append(s)
    loss_part, dx, dgf = loss_head(xc, final_norm_g[None], tgt)
    loss = lax.psum(loss_part[0, 0], AXES)
    grads = [None] * nl
    for l in reversed(range(nl)):
        dx, grads[l] = _layer_bwd(dx, saved[l], mod[l], layers[l], cos, sin)
    gfull = {n: jnp.stack([grads[l][n] for l in range(nl)]) for n in grads[0]}
    gfull["final_norm_g"] = dgf[0]

    out = {}

    def update(group, slots, name):
        shapes = [w[n].shape for n in group]
        flats = adamw(slots, *[_pack([d[n] for n in group], F32, slots.shape[1]) for d in (w, m, v)], name=name)
        for kind, flat in zip(("grad", "delta", "new_m", "new_v"), flats):
            for n, arr in zip(group, _unpack(flat, shapes)):
                out[kind, n] = arr

    group = [n for n, _ in SHARDED]
    send = _pack_blocks([_to_blocks(gfull[n], ax) for n, ax in SHARDED], BF16, ADAM_ROWS)
    update(group, _exchange(send, gather=False, name="exchange_grads"), "adamw_sharded")
    parts = _exchange(_pack([gfull[n] for n in REPLICATED], F32, 8), gather=True, name="gather_small_grads")
    update(list(REPLICATED), parts, "adamw_replicated")
    dmod_all = _unpack(parts, [w["ada_b"].shape], lead=(N_DEV,))[0]
    dmod_cols = jnp.moveaxis(lax.dynamic_slice_in_dim(dmod_all, me * ncol, ncol, axis=2), 0, 1)
    g_ada = ada_w_grad(c_all.T, dmod_cols)
    update(["ada_w"], _pack([g_ada], F32, ADAM_ROWS)[None], "adamw_ada_w")

    res = [loss, dx[None]]
    for kind in ("grad", "delta", "new_m", "new_v"):
        res += [out[kind, n] for n in WEIGHT_NAMES]
    return tuple(res)
```

```python
import functools
import math

import jax
import jax.numpy as jnp
from jax import lax
from jax.experimental import pallas as pl
from jax.experimental.pallas import tpu as pltpu

F32 = jnp.float32
BF16 = jnp.bfloat16

D_MODEL = 1024
WIDTH = 512
CHUNK = 128
SG_GROUPS = 4
HEADS = 8
MLA_NOPE = 64
MLA_ROPE = 32
MLA_V = 64
MLA_Q_RANK = 256
MLA_KV_RANK = 128
MLA_QK_PAD = 128
SB_DIM = 64
ROPE_THETA = 10000.0
NORM_EPS = 1e-6
N_BRANCH = 4
D_FF = 4096
Z_SG, Z_CONV, Z_MLA, Z_SB, Z_GATE, Z_COLS = 0, 1024, 2560, 3072, 4608, 8704
IN_COLS = 8608
MLA_COLS = MLA_Q_RANK + MLA_KV_RANK + MLA_ROPE

ADAM_LR, ADAM_B1, ADAM_B2, ADAM_EPS, ADAM_WD, ADAM_STEP = 0.001, 0.9, 0.999, 1e-08, 0.01, 10

N_DEV = 8
AXES = ("x", "y", "c")
MESH = pl.DeviceIdType.MESH
VMEM_LIMIT_V7X = 56 * 1024 * 1024
ATT_BQ = 1024
ATT_BK = 256
NEG_BIG = -1e30


def _cparams(sem=None):
    return pltpu.CompilerParams(dimension_semantics=sem, vmem_limit_bytes=VMEM_LIMIT_V7X)


def _div_tile(n, pref):
    if n <= pref:
        return n
    t = (pref // 128) * 128
    while t >= 128:
        if n % t == 0:
            return t
        t -= 128
    raise ValueError(f"no tile for {n}")


def _row_tile(t):
    return min(512, t)


def _dot(a, b, dims):
    return lax.dot_general(a.astype(BF16), b.astype(BF16), (dims, ((), ())), preferred_element_type=F32)


@jax.custom_vjp
def _mm(a, b):
    return _dot(a, b, ((1,), (0,)))


def _mm_fwd(a, b):
    return _mm(a, b), (a, b)


def _mm_bwd(res, g):
    a, b = res
    return _dot(g, b, ((1,), (1,))).astype(a.dtype), _dot(a, g, ((0,), (0,))).astype(b.dtype)


_mm.defvjp(_mm_fwd, _mm_bwd)


def _rms(x, g):
    return (x * lax.rsqrt(jnp.mean(x * x, axis=-1, keepdims=True) + NORM_EPS)) * g


def _normmod(x, g, sc, sh):
    return _rms(x, g) * (1.0 + sc) + sh


def _split_dot(m01, x):
    hi = x.astype(BF16)
    lo = (x - hi.astype(F32)).astype(BF16)
    return (jnp.dot(m01, hi, preferred_element_type=F32) + jnp.dot(m01, lo, preferred_element_type=F32))


def normmod_fwd(x, g, sc, sh):
    t, d = x.shape
    tt = _row_tile(t)

    def body(x_ref, g_ref, sc_ref, sh_ref, h_ref):
        h_ref[...] = _normmod(x_ref[...], g_ref[...], sc_ref[...], sh_ref[...]).astype(BF16)

    row = pl.BlockSpec((tt, d), lambda i: (i, 0))
    vec = pl.BlockSpec((1, d), lambda i: (0, 0))
    return pl.pallas_call(
        body, grid=(t // tt,), in_specs=[row, vec, vec, vec], out_specs=row,
        out_shape=jax.ShapeDtypeStruct((t, d), BF16), name="normmod_fwd", compiler_params=_cparams(("parallel",)),
    )(x, g, sc, sh)


def normmod_bwd(x, g, sc, sh, dh, dres):
    t, d = x.shape
    tt = _row_tile(t)

    def body(x_ref, g_ref, sc_ref, sh_ref, dh_ref, dres_ref, dx_ref, dg_ref, dsc_ref, dsh_ref):
        _, vjp = jax.vjp(_normmod, x_ref[...], g_ref[...], sc_ref[...], sh_ref[...])
        dx, dg, dsc, dsh = vjp(dh_ref[...])
        dx_ref[...] = dx + dres_ref[...]

        @pl.when(pl.program_id(0) == 0)
        def _():
            dg_ref[...] = jnp.zeros_like(dg_ref)
            dsc_ref[...] = jnp.zeros_like(dsc_ref)
            dsh_ref[...] = jnp.zeros_like(dsh_ref)

        dg_ref[...] += dg
        dsc_ref[...] += dsc
        dsh_ref[...] += dsh

    row = pl.BlockSpec((tt, d), lambda i: (i, 0))
    vec = pl.BlockSpec((1, d), lambda i: (0, 0))
    vs = jax.ShapeDtypeStruct((1, d), F32)
    return pl.pallas_call(
        body, grid=(t // tt,), in_specs=[row, vec, vec, vec, row, row], out_specs=[row, vec, vec, vec],
        out_shape=[jax.ShapeDtypeStruct((t, d), F32), vs, vs, vs], name="normmod_bwd",
        compiler_params=_cparams(("arbitrary",)),
    )(x, g, sc, sh, dh, dres)


def resid_bwd(dxn, m, g):
    t, d = dxn.shape
    tt = _row_tile(t)

    def body(dx_ref, m_ref, g_ref, dm_ref, dg_ref):
        dx = dx_ref[...]
        dm_ref[...] = (dx * g_ref[...]).astype(BF16)

        @pl.when(pl.program_id(0) == 0)
        def _():
            dg_ref[...] = jnp.zeros_like(dg_ref)

        dg_ref[...] += jnp.sum(dx * m_ref[...], axis=0, keepdims=True)

    row = pl.BlockSpec((tt, d), lambda i: (i, 0))
    vec = pl.BlockSpec((1, d), lambda i: (0, 0))
    return pl.pallas_call(
        body, grid=(t // tt,), in_specs=[row, row, vec], out_specs=[row, vec],
        out_shape=[jax.ShapeDtypeStruct((t, d), BF16), jax.ShapeDtypeStruct((1, d), F32)], name="resid_bwd",
        compiler_params=_cparams(("arbitrary",)),
    )(dxn, m, g)


def loss_head(x, gf, tgt):
    t, d = x.shape
    tt = _row_tile(t)

    def f(xv, gv, tv):
        y = _rms(xv, gv)
        return 0.5 * jnp.sum(jnp.mean(jnp.square(y - tv), axis=-1))

    def body(x_ref, g_ref, t_ref, loss_ref, dx_ref, dg_ref):
        val, vjp = jax.vjp(lambda xv, gv: f(xv, gv, t_ref[...]), x_ref[...], g_ref[...])
        dx, dg = vjp(jnp.ones((), F32))
        dx_ref[...] = dx

        @pl.when(pl.program_id(0) == 0)
        def _():
            dg_ref[...] = jnp.zeros_like(dg_ref)
            loss_ref[...] = jnp.zeros_like(loss_ref)

        dg_ref[...] += dg
        loss_ref[...] += jnp.full(loss_ref.shape, val, F32)

    row = pl.BlockSpec((tt, d), lambda i: (i, 0))
    vec = pl.BlockSpec((1, d), lambda i: (0, 0))
    lsp = pl.BlockSpec((1, 128), lambda i: (0, 0))
    return pl.pallas_call(
        body, grid=(t // tt,), in_specs=[row, vec, row], out_specs=[lsp, row, vec],
        out_shape=[jax.ShapeDtypeStruct((1, 128), F32), jax.ShapeDtypeStruct((t, d), F32),
                   jax.ShapeDtypeStruct((1, d), F32)],
        name="loss_head", compiler_params=_cparams(("arbitrary",)),
    )(x, gf, tgt)


def matmul(a, b, *, name, out_dtypes=(F32,), epi=None, epi_in=(), tm=1024, tn=512, tk=1024):
    m, k = a.shape
    n = b.shape[1]
    tm, tn, tk = min(tm, m), _div_tile(n, tn), _div_tile(k, tk)
    nk = k // tk
    n_epi, n_out = len(epi_in), len(out_dtypes)

    def body(*refs):
        a_ref, b_ref = refs[:2]
        e_refs = refs[2:2 + n_epi]
        o_refs = refs[2 + n_epi:2 + n_epi + n_out]
        acc_ref = refs[-1]
        kk = pl.program_id(2)

        @pl.when(kk == 0)
        def _():
            acc_ref[...] = jnp.zeros_like(acc_ref)

        acc_ref[...] += jnp.dot(a_ref[...].astype(BF16), b_ref[...].astype(BF16), preferred_element_type=F32)

        @pl.when(kk == nk - 1)
        def _():
            acc = acc_ref[...]
            outs = (acc,) if epi is None else epi(acc, *[r[...] for r in e_refs])
            for o_ref, o in zip(o_refs, outs):
                o_ref[...] = o.astype(o_ref.dtype)

    in_specs = [pl.BlockSpec((tm, tk), lambda i, j, kk: (i, kk)), pl.BlockSpec((tk, tn), lambda i, j, kk: (kk, j))]
    for _, kind in epi_in:
        in_specs.append(pl.BlockSpec((tm, tn), lambda i, j, kk: (i, j)) if kind == "mn"
                        else pl.BlockSpec((1, tn), lambda i, j, kk: (0, j)))
    out_spec = pl.BlockSpec((tm, tn), lambda i, j, kk: (i, j))
    outs = pl.pallas_call(
        body, grid=(m // tm, n // tn, nk), in_specs=in_specs, out_specs=[out_spec] * n_out,
        out_shape=[jax.ShapeDtypeStruct((m, n), dt) for dt in out_dtypes],
        scratch_shapes=[pltpu.VMEM((tm, tn), F32)], name=name,
        compiler_params=_cparams(("parallel", "parallel", "arbitrary")),
    )(a, b, *[arr for arr, _ in epi_in])
    return outs[0] if n_out == 1 else outs


def matmul_tn(a, b, *, name, tko=1024, tn=512, tt=1024):
    t, ko = a.shape
    n = b.shape[1]
    tko, tn, tt = _div_tile(ko, tko), _div_tile(n, tn), min(tt, t)
    nt = t // tt

    def body(a_ref, b_ref, o_ref, acc_ref):
        s = pl.program_id(2)

        @pl.when(s == 0)
        def _():
            acc_ref[...] = jnp.zeros_like(acc_ref)

        acc_ref[...] += _dot(a_ref[...], b_ref[...], ((0,), (0,)))

        @pl.when(s == nt - 1)
        def _():
            o_ref[...] = acc_ref[...]

    return pl.pallas_call(
        body, grid=(ko // tko, n // tn, nt),
        in_specs=[pl.BlockSpec((tt, tko), lambda i, j, s: (s, i)), pl.BlockSpec((tt, tn), lambda i, j, s: (s, j))],
        out_specs=pl.BlockSpec((tko, tn), lambda i, j, s: (i, j)), out_shape=jax.ShapeDtypeStruct((ko, n), F32),
        scratch_shapes=[pltpu.VMEM((tko, tn), F32)], name=name,
        compiler_params=_cparams(("parallel", "parallel", "arbitrary")),
    )(a, b)


def merge_fwd(ys, w_branch, z):
    t = ys[0].shape[0]
    d = D_MODEL
    tm, tn = min(512, t), 512
    gate0 = Z_GATE // tn

    def body(y0, y1, y2, y3, w_ref, g0, g1, g2, g3, up_ref, mg_ref):
        acc = jnp.zeros((tm, tn), F32)
        for n, (y_ref, g_ref) in enumerate(zip((y0, y1, y2, y3), (g0, g1, g2, g3))):
            up = jnp.dot(y_ref[...], w_ref[n], preferred_element_type=F32)
            up_ref[n] = up
            acc = acc + jax.nn.sigmoid(g_ref[...]) * up
        mg_ref[...] = acc.astype(BF16)

    ysp = pl.BlockSpec((tm, WIDTH), lambda i, j: (i, 0))
    gate_specs = [pl.BlockSpec((tm, tn), functools.partial(lambda i, j, n: (i, gate0 + n * (d // tn) + j), n=n))
                  for n in range(N_BRANCH)]
    return pl.pallas_call(
        body, grid=(t // tm, d // tn),
        in_specs=[ysp, ysp, ysp, ysp, pl.BlockSpec((N_BRANCH, WIDTH, tn), lambda i, j: (0, 0, j))] + gate_specs,
        out_specs=[pl.BlockSpec((N_BRANCH, tm, tn), lambda i, j: (0, i, j)), pl.BlockSpec((tm, tn), lambda i, j: (i, j))],
        out_shape=[jax.ShapeDtypeStruct((N_BRANCH, t, d), F32), jax.ShapeDtypeStruct((t, d), BF16)],
        name="merge_fwd", compiler_params=_cparams(("parallel", "parallel")),
    )(*ys, w_branch, z, z, z, z)


def merge_bwd(dmerged, up, z):
    t, d = dmerged.shape
    tm, tn = min(512, t), 512
    gate0 = Z_GATE // tn
    nj = d // tn

    def body(dm_ref, up_ref, g_ref, dzg_ref, dup_ref):
        dm = dm_ref[...]
        s = jax.nn.sigmoid(g_ref[...])
        dzg_ref[...] = (dm * up_ref[0] * s * (1.0 - s)).astype(BF16)
        dup_ref[0] = (dm * s).astype(BF16)

    blk = pl.BlockSpec((1, tm, tn), lambda i, j, n: (n, i, j))
    return pl.pallas_call(
        body, grid=(t // tm, nj, N_BRANCH),
        in_specs=[pl.BlockSpec((tm, tn), lambda i, j, n: (i, j)), blk,
                  pl.BlockSpec((tm, tn), lambda i, j, n: (i, gate0 + n * nj + j))],
        out_specs=[pl.BlockSpec((tm, tn), lambda i, j, n: (i, n * nj + j)), blk],
        out_shape=[jax.ShapeDtypeStruct((t, N_BRANCH * d), BF16), jax.ShapeDtypeStruct((N_BRANCH, t, d), BF16)],
        name="merge_bwd", compiler_params=_cparams(("parallel", "parallel", "arbitrary")),
    )(dmerged, up, z)


def _sg_tile(zsg, g, w, bt):
    tt = zsg.shape[0]
    a = jax.nn.gelu(zsg)
    u, v = a[:, :WIDTH], a[:, WIDTH:]
    vc = v - jnp.mean(v, axis=-1, keepdims=True)
    v = (vc * lax.rsqrt(jnp.mean(vc * vc, axis=-1, keepdims=True) + NORM_EPS)) * g
    row = lax.broadcasted_iota(jnp.int32, (CHUNK, CHUNK), 0)
    col = lax.broadcasted_iota(jnp.int32, (CHUNK, CHUNK), 1)
    cols = []
    for gi in range(SG_GROUPS):
        wc = jnp.where(row >= col, w[gi], 0.0)
        bias = bt[:, gi:gi + 1]
        rows = [_mm(wc, v[ch * CHUNK:(ch + 1) * CHUNK, gi * CHUNK:(gi + 1) * CHUNK]) + bias for ch in range(tt // CHUNK)]
        cols.append(jnp.concatenate(rows, axis=0) if len(rows) > 1 else rows[0])
    return u * jnp.concatenate(cols, axis=1)


def sg_fwd(z, g, w, bt):
    t = z.shape[0]
    tt = _row_tile(t)

    def body(z_ref, g_ref, w_ref, b_ref, y_ref):
        y_ref[...] = _sg_tile(z_ref[...], g_ref[...], w_ref[...], b_ref[...]).astype(BF16)

    return pl.pallas_call(
        body, grid=(t // tt,),
        in_specs=[pl.BlockSpec((tt, 2 * WIDTH), lambda i: (i, 0)), pl.BlockSpec((1, WIDTH), lambda i: (0, 0)),
                  pl.BlockSpec((SG_GROUPS, CHUNK, CHUNK), lambda i: (0, 0, 0)), pl.BlockSpec((CHUNK, SG_GROUPS), lambda i: (0, 0))],
        out_specs=pl.BlockSpec((tt, WIDTH), lambda i: (i, 0)), out_shape=jax.ShapeDtypeStruct((t, WIDTH), BF16),
        name="sg_fwd", compiler_params=_cparams(("parallel",)),
    )(z, g, w, bt)


def sg_bwd(z, g, w, bt, dy):
    t = z.shape[0]
    tt = _row_tile(t)

    def body(z_ref, g_ref, w_ref, b_ref, dy_ref, dz_ref, dg_ref, dw_ref, db_ref):
        _, vjp = jax.vjp(_sg_tile, z_ref[...], g_ref[...], w_ref[...], b_ref[...])
        dz, dg, dw, db = vjp(dy_ref[...])
        dz_ref[...] = dz.astype(BF16)

        @pl.when(pl.program_id(0) == 0)
        def _():
            dg_ref[...] = jnp.zeros_like(dg_ref)
            dw_ref[...] = jnp.zeros_like(dw_ref)
            db_ref[...] = jnp.zeros_like(db_ref)

        dg_ref[...] += dg
        dw_ref[...] += dw
        db_ref[...] += db

    gs = pl.BlockSpec((1, WIDTH), lambda i: (0, 0))
    ws = pl.BlockSpec((SG_GROUPS, CHUNK, CHUNK), lambda i: (0, 0, 0))
    bs = pl.BlockSpec((CHUNK, SG_GROUPS), lambda i: (0, 0))
    return pl.pallas_call(
        body, grid=(t // tt,),
        in_specs=[pl.BlockSpec((tt, 2 * WIDTH), lambda i: (i, 0)), gs, ws, bs, pl.BlockSpec((tt, WIDTH), lambda i: (i, 0))],
        out_specs=[pl.BlockSpec((tt, 2 * WIDTH), lambda i: (i, 0)), gs, ws, bs],
        out_shape=[jax.ShapeDtypeStruct((t, 2 * WIDTH), BF16), jax.ShapeDtypeStruct((1, WIDTH), F32),
                   jax.ShapeDtypeStruct((SG_GROUPS, CHUNK, CHUNK), F32), jax.ShapeDtypeStruct((CHUNK, SG_GROUPS), F32)],
        name="sg_bwd", compiler_params=_cparams(("arbitrary",)),
    )(z, g, w, bt, dy)


HALO = 8


def _shift_down(halo, cur, k):
    tt = cur.shape[0]
    ext = jnp.concatenate([halo, cur], axis=0)
    return ext[HALO - k:HALO - k + tt]


def _shift_up(cur, halo, k):
    tt = cur.shape[0]
    ext = jnp.concatenate([cur, halo], axis=0)
    return ext[k:k + tt]


def conv_fwd(z, cw):
    t = z.shape[0]
    tt = _row_tile(t)
    cb = Z_CONV // WIDTH
    hb = tt // HALO

    def body(gb_ref, gc_ref, xv_ref, hgc_ref, hxv_ref, w_ref, y_ref):
        first = (pl.program_id(0) == 0)
        tcur = gc_ref[...] * xv_ref[...]
        thalo = jnp.where(first, 0.0, hgc_ref[...] * hxv_ref[...])
        w = w_ref[...]
        y = w[2:3] * tcur + w[1:2] * _shift_down(thalo, tcur, 1) + w[0:1] * _shift_down(thalo, tcur, 2)
        y_ref[...] = (gb_ref[...] * y).astype(BF16)

    def cur(off):
        return pl.BlockSpec((tt, WIDTH), lambda i: (i, cb + off))

    def prev(off):
        return pl.BlockSpec((HALO, WIDTH), lambda i: (jnp.maximum(i * hb - 1, 0), cb + off))

    return pl.pallas_call(
        body, grid=(t // tt,),
        in_specs=[cur(0), cur(1), cur(2), prev(1), prev(2), pl.BlockSpec((3, WIDTH), lambda i: (0, 0))],
        out_specs=pl.BlockSpec((tt, WIDTH), lambda i: (i, 0)), out_shape=jax.ShapeDtypeStruct((t, WIDTH), BF16),
        name="conv_fwd", compiler_params=_cparams(("parallel",)),
    )(z, z, z, z, z, cw)


def conv_bwd(z, cw, dy):
    t = z.shape[0]
    tt = _row_tile(t)
    nt = t // tt
    cb = Z_CONV // WIDTH
    hb = tt // HALO

    def body(gb_ref, gc_ref, xv_ref, hgc_ref, hxv_ref, ngb_ref, dy_ref, ndy_ref, w_ref, dz_ref, dw_ref):
        i = pl.program_id(0)
        gb, gc, xv = gb_ref[...], gc_ref[...], xv_ref[...]
        tcur = gc * xv
        thalo = jnp.where(i == 0, 0.0, hgc_ref[...] * hxv_ref[...])
        t1, t2 = _shift_down(thalo, tcur, 1), _shift_down(thalo, tcur, 2)
        w = w_ref[...]
        y = w[2:3] * tcur + w[1:2] * t1 + w[0:1] * t2
        dy = dy_ref[...]
        dyc = dy * gb
        dyn = jnp.where(i == nt - 1, 0.0, ndy_ref[...] * ngb_ref[...])
        dt = w[2:3] * dyc + w[1:2] * _shift_up(dyc, dyn, 1) + w[0:1] * _shift_up(dyc, dyn, 2)
        dz_ref[...] = jnp.concatenate([dy * y, dt * xv, dt * gc], axis=1).astype(BF16)

        @pl.when(i == 0)
        def _():
            dw_ref[...] = jnp.zeros_like(dw_ref)

        dw_ref[...] += jnp.concatenate([jnp.sum(dyc * t2, axis=0, keepdims=True), jnp.sum(dyc * t1, axis=0, keepdims=True),
                                        jnp.sum(dyc * tcur, axis=0, keepdims=True)], axis=0)

    def cur(off):
        return pl.BlockSpec((tt, WIDTH), lambda i: (i, cb + off))

    def prev(off):
        return pl.BlockSpec((HALO, WIDTH), lambda i: (jnp.maximum(i * hb - 1, 0), cb + off))

    last = t // HALO - 1
    nxt_z = pl.BlockSpec((HALO, WIDTH), lambda i: (jnp.minimum((i + 1) * hb, last), cb))
    nxt_dy = pl.BlockSpec((HALO, WIDTH), lambda i: (jnp.minimum((i + 1) * hb, last), 0))
    return pl.pallas_call(
        body, grid=(nt,),
        in_specs=[cur(0), cur(1), cur(2), prev(1), prev(2), nxt_z, pl.BlockSpec((tt, WIDTH), lambda i: (i, 0)), nxt_dy,
                  pl.BlockSpec((3, WIDTH), lambda i: (0, 0))],
        out_specs=[pl.BlockSpec((tt, 3 * WIDTH), lambda i: (i, 0)), pl.BlockSpec((3, WIDTH), lambda i: (0, 0))],
        out_shape=[jax.ShapeDtypeStruct((t, 3 * WIDTH), BF16), jax.ShapeDtypeStruct((3, WIDTH), F32)],
        name="conv_bwd", compiler_params=_cparams(("arbitrary",)),
    )(z, z, z, z, z, z, dy, dy, cw)


def _mla_prep(zm, cos, sin, gq, gkv, wuq, wukv):
    tt = zm.shape[0]
    cq, ckv, kpe = zm[:, :MLA_Q_RANK], zm[:, MLA_Q_RANK:MLA_Q_RANK + MLA_KV_RANK], zm[:, MLA_Q_RANK + MLA_KV_RANK:MLA_COLS]
    q = _mm(_rms(cq, gq), wuq)
    kv = _mm(_rms(ckv, gkv), wukv)
    half = MLA_ROPE // 2

    def rope(xr):
        x1, x2 = xr[:, :half], xr[:, half:]
        return jnp.concatenate([x1 * cos - x2 * sin, x2 * cos + x1 * sin], axis=-1)

    kpe = rope(kpe)
    pad = jnp.zeros((tt, MLA_QK_PAD - MLA_NOPE - MLA_ROPE), F32)
    dq, dkv = MLA_NOPE + MLA_ROPE, MLA_NOPE + MLA_V
    qs, ks, vs = [], [], []
    for h in range(HEADS):
        qs.append(jnp.concatenate([q[:, h * dq:h * dq + MLA_NOPE], rope(q[:, h * dq + MLA_NOPE:(h + 1) * dq]), pad], axis=-1))
        ks.append(jnp.concatenate([kv[:, h * dkv:h * dkv + MLA_NOPE], kpe, pad], axis=-1))
        vs.append(kv[:, h * dkv + MLA_NOPE:(h + 1) * dkv])
    return qs, ks, vs


def _mla_prep_specs(tt):
    zs = pl.BlockSpec((tt, WIDTH), lambda i: (i, Z_MLA // WIDTH))
    cs = pl.BlockSpec((tt, MLA_ROPE // 2), lambda i: (i, 0))
    return [zs, cs, cs, pl.BlockSpec((1, MLA_Q_RANK), lambda i: (0, 0)), pl.BlockSpec((1, MLA_KV_RANK), lambda i: (0, 0)),
            pl.BlockSpec((MLA_Q_RANK, HEADS * (MLA_NOPE + MLA_ROPE)), lambda i: (0, 0)),
            pl.BlockSpec((MLA_KV_RANK, HEADS * (MLA_NOPE + MLA_V)), lambda i: (0, 0))]


def mla_prep_fwd(z, cos, sin, gq, gkv, wuq, wukv):
    t = z.shape[0]
    tt = _row_tile(t)

    def body(z_ref, c_ref, s_ref, gq_ref, gkv_ref, wq_ref, wkv_ref, q_ref, k_ref, v_ref):
        qs, ks, vs = _mla_prep(z_ref[...], c_ref[...], s_ref[...], gq_ref[...], gkv_ref[...],
                               wq_ref[...].astype(F32), wkv_ref[...].astype(F32))
        for h in range(HEADS):
            q_ref[h] = qs[h].astype(BF16)
            k_ref[h] = ks[h].astype(BF16)
            v_ref[h] = vs[h].astype(BF16)

    hs = pl.BlockSpec((HEADS, tt, MLA_QK_PAD), lambda i: (0, i, 0))
    vsp = pl.BlockSpec((HEADS, tt, MLA_V), lambda i: (0, i, 0))
    return pl.pallas_call(
        body, grid=(t // tt,), in_specs=_mla_prep_specs(tt), out_specs=[hs, hs, vsp],
        out_shape=[jax.ShapeDtypeStruct((HEADS, t, MLA_QK_PAD), BF16), jax.ShapeDtypeStruct((HEADS, t, MLA_QK_PAD), BF16),
                   jax.ShapeDtypeStruct((HEADS, t, MLA_V), BF16)],
        name="mla_prep_fwd", compiler_params=_cparams(("parallel",)),
    )(z, cos, sin, gq, gkv, wuq, wukv)


def mla_prep_bwd(z, cos, sin, gq, gkv, wuq, wukv, dq, dk, dv):
    t = z.shape[0]
    tt = _row_tile(t)

    def body(z_ref, c_ref, s_ref, gq_ref, gkv_ref, wq_ref, wkv_ref, dq_ref, dk_ref, dv_ref,
             dz_ref, dgq_ref, dgkv_ref, dwq_ref, dwkv_ref):
        cos_v, sin_v = c_ref[...], s_ref[...]
        _, vjp = jax.vjp(lambda zm, a, b, wq, wkv: _mla_prep(zm, cos_v, sin_v, a, b, wq, wkv),
                         z_ref[...], gq_ref[...], gkv_ref[...], wq_ref[...].astype(F32), wkv_ref[...].astype(F32))
        cot = ([dq_ref[h] for h in range(HEADS)], [dk_ref[h] for h in range(HEADS)], [dv_ref[h] for h in range(HEADS)])
        dz, dgq, dgkv, dwq, dwkv = vjp(cot)
        dz_ref[...] = dz.astype(BF16)

        @pl.when(pl.program_id(0) == 0)
        def _():
            for r in (dgq_ref, dgkv_ref, dwq_ref, dwkv_ref):
                r[...] = jnp.zeros_like(r)

        dgq_ref[...] += dgq
        dgkv_ref[...] += dgkv
        dwq_ref[...] += dwq
        dwkv_ref[...] += dwkv

    specs = _mla_prep_specs(tt)
    hs = pl.BlockSpec((HEADS, tt, MLA_QK_PAD), lambda i: (0, i, 0))
    vsp = pl.BlockSpec((HEADS, tt, MLA_V), lambda i: (0, i, 0))
    nq, nkv = HEADS * (MLA_NOPE + MLA_ROPE), HEADS * (MLA_NOPE + MLA_V)
    return pl.pallas_call(
        body, grid=(t // tt,), in_specs=specs + [hs, hs, vsp],
        out_specs=[pl.BlockSpec((tt, WIDTH), lambda i: (i, 0)), specs[3], specs[4], specs[5], specs[6]],
        out_shape=[jax.ShapeDtypeStruct((t, WIDTH), BF16), jax.ShapeDtypeStruct((1, MLA_Q_RANK), F32),
                   jax.ShapeDtypeStruct((1, MLA_KV_RANK), F32), jax.ShapeDtypeStruct((MLA_Q_RANK, nq), F32),
                   jax.ShapeDtypeStruct((MLA_KV_RANK, nkv), F32)],
        name="mla_prep_bwd", compiler_params=_cparams(("arbitrary",)),
    )(z, cos, sin, gq, gkv, wuq, wukv, dq, dk, dv)


def _att_blocks(t):
    return min(ATT_BQ, t), min(ATT_BK, t)


def _key_blocks(a, bk):
    h, t, d = a.shape
    return a.reshape(h, t // bk, bk, d).transpose(0, 1, 3, 2)


def _att_positions(kb, qi, bq, bk):
    krow = kb * bk + lax.broadcasted_iota(jnp.int32, (bk, bq), 0)
    qcol = qi * bq + lax.broadcasted_iota(jnp.int32, (bk, bq), 1)
    return krow, qcol


def _walk(qi, nd, step, carry, diagonal_first=False):
    n_full = qi * nd
    if diagonal_first:
        carry = lax.fori_loop(0, nd, lambda i, c: step(n_full + nd - 1 - i, c, True), carry)
        return lax.fori_loop(0, n_full, lambda i, c: step(n_full - 1 - i, c, False), carry)
    carry = lax.fori_loop(0, n_full, lambda kb, c: step(kb, c, False), carry)
    return lax.fori_loop(0, nd, lambda i, c: step(n_full + i, c, True), carry)


def softmax_att_fwd(qt, k, vtb, scale):
    hh, dk, t = qt.shape
    dv = vtb.shape[2]
    bq, bk = _att_blocks(t)

    def body(qt_ref, k_ref, vt_ref, o_ref, lse_ref):
        qi = pl.program_id(1)
        qtv = qt_ref[0]

        def step(kb, carry, masked):
            m, l, acc = carry
            ks = pl.multiple_of(kb * bk, bk)
            s = jnp.dot(k_ref[0, pl.ds(ks, bk), :], qtv, preferred_element_type=F32) * scale
            if masked:
                krow, qcol = _att_positions(kb, qi, bq, bk)
                s = jnp.where(krow <= qcol, s, NEG_BIG)
            m_new = jnp.maximum(m, jnp.max(s, axis=0, keepdims=True))
            p = jnp.exp(s - m_new)
            alpha = jnp.exp(m - m_new)
            l = alpha * l + jnp.sum(p, axis=0, keepdims=True)
            acc = alpha * acc + jnp.dot(vt_ref[0, kb], p.astype(BF16), preferred_element_type=F32)
            return m_new, l, acc

        init = (jnp.full((1, bq), NEG_BIG, F32), jnp.zeros((1, bq), F32), jnp.zeros((dv, bq), F32))
        m, l, acc = _walk(qi, bq // bk, step, init)
        o_ref[0] = acc / l
        lse_ref[0] = m + jnp.log(l)

    return pl.pallas_call(
        body, grid=(hh, t // bq),
        in_specs=[pl.BlockSpec((1, dk, bq), lambda h, i: (h, 0, i)), pl.BlockSpec((1, t, dk), lambda h, i: (h, 0, 0)),
                  pl.BlockSpec((1, t // bk, dv, bk), lambda h, i: (h, 0, 0, 0))],
        out_specs=[pl.BlockSpec((1, dv, bq), lambda h, i: (h, 0, i)), pl.BlockSpec((1, 1, bq), lambda h, i: (h, 0, i))],
        out_shape=[jax.ShapeDtypeStruct((hh, dv, t), F32), jax.ShapeDtypeStruct((hh, 1, t), F32)],
        name="softmax_att_fwd", compiler_params=_cparams(("parallel", "arbitrary")),
    )(qt, k, vtb)


def softmax_att_bwd(qt, q, k, ktb, v, ot, lse, do, dot_, scale):
    hh, dk, t = qt.shape
    dv = v.shape[2]
    bq, bk = _att_blocks(t)

    def body(qt_ref, q_ref, k_ref, kt_ref, v_ref, ot_ref, lse_ref, do_ref, dot_ref, dqt_ref, dk_ref, dv_ref):
        qi = pl.program_id(1)

        @pl.when(qi == 0)
        def _():
            dk_ref[...] = jnp.zeros_like(dk_ref)
            dv_ref[...] = jnp.zeros_like(dv_ref)

        qtv, qv, dov, dotv = qt_ref[0], q_ref[0], do_ref[0], dot_ref[0]
        lse_v = lse_ref[0]
        delta = jnp.sum(dotv.astype(F32) * ot_ref[0], axis=0, keepdims=True)

        def step(kb, dqt, masked):
            ks = pl.multiple_of(kb * bk, bk)
            s = jnp.dot(k_ref[0, pl.ds(ks, bk), :], qtv, preferred_element_type=F32) * scale
            if masked:
                krow, qcol = _att_positions(kb, qi, bq, bk)
                s = jnp.where(krow <= qcol, s, NEG_BIG)
            p = jnp.exp(s - lse_v)
            dp = jnp.dot(v_ref[0, pl.ds(ks, bk), :], dotv, preferred_element_type=F32)
            ds = (p * (dp - delta) * scale).astype(BF16)
            dk_ref[0, pl.ds(ks, bk), :] += jnp.dot(ds, qv, preferred_element_type=F32)
            dv_ref[0, pl.ds(ks, bk), :] += jnp.dot(p.astype(BF16), dov, preferred_element_type=F32)
            return dqt + jnp.dot(kt_ref[0, kb], ds, preferred_element_type=F32)

        dqt_ref[0] = _walk(qi, bq // bk, step, jnp.zeros((dk, bq), F32))

    nkb = t // bk
    qts = pl.BlockSpec((1, dk, bq), lambda h, i: (h, 0, i))
    qs = pl.BlockSpec((1, bq, dk), lambda h, i: (h, i, 0))
    kfull = pl.BlockSpec((1, t, dk), lambda h, i: (h, 0, 0))
    vfull = pl.BlockSpec((1, t, dv), lambda h, i: (h, 0, 0))
    vts = pl.BlockSpec((1, dv, bq), lambda h, i: (h, 0, i))
    return pl.pallas_call(
        body, grid=(hh, t // bq),
        in_specs=[qts, qs, kfull, pl.BlockSpec((1, nkb, dk, bk), lambda h, i: (h, 0, 0, 0)), vfull, vts,
                  pl.BlockSpec((1, 1, bq), lambda h, i: (h, 0, i)), pl.BlockSpec((1, bq, dv), lambda h, i: (h, i, 0)), vts],
        out_specs=[qts, kfull, vfull],
        out_shape=[jax.ShapeDtypeStruct((hh, dk, t), F32), jax.ShapeDtypeStruct((hh, t, dk), F32),
                   jax.ShapeDtypeStruct((hh, t, dv), F32)],
        name="softmax_att_bwd", compiler_params=_cparams(("parallel", "arbitrary")),
    )(qt, q, k, ktb, v, ot, lse, do, dot_)


def _softplus(z):
    return jnp.maximum(z, 0.0) + jnp.log(1.0 + jnp.exp(-jnp.abs(z)))


def _strict_upper(bk):
    r = lax.broadcasted_iota(jnp.int32, (bk, bk), 0)
    c = lax.broadcasted_iota(jnp.int32, (bk, bk), 1)
    return jnp.where(c > r, 1.0, 0.0).astype(BF16)


def _strict_lower(bk):
    r = lax.broadcasted_iota(jnp.int32, (bk, bk), 0)
    c = lax.broadcasted_iota(jnp.int32, (bk, bk), 1)
    return jnp.where(c < r, 1.0, 0.0).astype(BF16)


def stick_att_fwd(qt, k, vtb):
    hh, dk, t = qt.shape
    dv = vtb.shape[2]
    bq, bk = _att_blocks(t)
    nkb = t // bk

    def body(qt_ref, k_ref, vt_ref, o_ref, rs_ref):
        qi = pl.program_id(1)
        qtv = qt_ref[0]
        upper = _strict_upper(bk)

        def step(kb, carry, masked):
            r, acc = carry
            ks = pl.multiple_of(kb * bk, bk)
            z = jnp.dot(k_ref[0, pl.ds(ks, bk), :], qtv, preferred_element_type=F32)
            sp = _softplus(z)
            l = -sp
            if masked:
                krow, qcol = _att_positions(kb, qi, bq, bk)
                mask = krow < qcol
                l = jnp.where(mask, l, 0.0)
            rs_ref[0, kb] = r
            a = jnp.exp(z - sp + r + _split_dot(upper, l))
            if masked:
                a = jnp.where(mask, a, 0.0)
            acc = acc + jnp.dot(vt_ref[0, kb], a.astype(BF16), preferred_element_type=F32)
            return r + jnp.sum(l, axis=0, keepdims=True), acc

        init = (jnp.zeros((1, bq), F32), jnp.zeros((dv, bq), F32))
        o_ref[0] = _walk(qi, bq // bk, step, init, diagonal_first=True)[1]

    return pl.pallas_call(
        body, grid=(hh, t // bq),
        in_specs=[pl.BlockSpec((1, dk, bq), lambda h, i: (h, 0, i)), pl.BlockSpec((1, t, dk), lambda h, i: (h, 0, 0)),
                  pl.BlockSpec((1, nkb, dv, bk), lambda h, i: (h, 0, 0, 0))],
        out_specs=[pl.BlockSpec((1, dv, bq), lambda h, i: (h, 0, i)), pl.BlockSpec((1, nkb, 1, bq), lambda h, i: (h, 0, 0, i))],
        out_shape=[jax.ShapeDtypeStruct((hh, dv, t), F32), jax.ShapeDtypeStruct((hh, nkb, 1, t), F32)],
        name="stick_att_fwd", compiler_params=_cparams(("parallel", "arbitrary")),
    )(qt, k, vtb)


def stick_att_bwd(qt, q, k, ktb, v, rs, do, dot_, scale):
    hh, dk, t = qt.shape
    dv = v.shape[2]
    bq, bk = _att_blocks(t)
    nkb = t // bk

    def body(qt_ref, q_ref, k_ref, kt_ref, v_ref, rs_ref, do_ref, dot_ref, dqt_ref, dk_ref, dv_ref):
        qi = pl.program_id(1)

        @pl.when(qi == 0)
        def _():
            dk_ref[...] = jnp.zeros_like(dk_ref)
            dv_ref[...] = jnp.zeros_like(dv_ref)

        qtv, qv, dov, dotv = qt_ref[0], q_ref[0], do_ref[0], dot_ref[0]
        upper, lower = _strict_upper(bk), _strict_lower(bk)

        def step(kb, carry, masked):
            pre, dqt = carry
            ks = pl.multiple_of(kb * bk, bk)
            z = jnp.dot(k_ref[0, pl.ds(ks, bk), :], qtv, preferred_element_type=F32)
            sp = _softplus(z)
            l = -sp
            if masked:
                krow, qcol = _att_positions(kb, qi, bq, bk)
                mask = krow < qcol
                l = jnp.where(mask, l, 0.0)
            la = z - sp
            a = jnp.exp(la + rs_ref[0, kb] + _split_dot(upper, l))
            if masked:
                a = jnp.where(mask, a, 0.0)
            beta = jnp.exp(la)
            g = a * jnp.dot(v_ref[0, pl.ds(ks, bk), :], dotv, preferred_element_type=F32)
            dz = g * (1.0 - beta) - beta * (pre + jnp.dot(lower, g.astype(BF16), preferred_element_type=F32))
            if masked:
                dz = jnp.where(mask, dz, 0.0)
            dzb = dz.astype(BF16)
            dk_ref[0, pl.ds(ks, bk), :] += jnp.dot(dzb, qv, preferred_element_type=F32)
            dv_ref[0, pl.ds(ks, bk), :] += jnp.dot(a.astype(BF16), dov, preferred_element_type=F32)
            return pre + jnp.sum(g, axis=0, keepdims=True), dqt + jnp.dot(kt_ref[0, kb], dzb, preferred_element_type=F32)

        init = (jnp.zeros((1, bq), F32), jnp.zeros((dk, bq), F32))
        dqt_ref[0] = _walk(qi, bq // bk, step, init)[1] * scale

    qts = pl.BlockSpec((1, dk, bq), lambda h, i: (h, 0, i))
    qs = pl.BlockSpec((1, bq, dk), lambda h, i: (h, i, 0))
    kfull = pl.BlockSpec((1, t, dk), lambda h, i: (h, 0, 0))
    vfull = pl.BlockSpec((1, t, dv), lambda h, i: (h, 0, 0))
    return pl.pallas_call(
        body, grid=(hh, t // bq),
        in_specs=[qts, qs, kfull, pl.BlockSpec((1, nkb, dk, bk), lambda h, i: (h, 0, 0, 0)), vfull,
                  pl.BlockSpec((1, nkb, 1, bq), lambda h, i: (h, 0, 0, i)), pl.BlockSpec((1, bq, dv), lambda h, i: (h, i, 0)),
                  pl.BlockSpec((1, dv, bq), lambda h, i: (h, 0, i))],
        out_specs=[qts, kfull, vfull],
        out_shape=[jax.ShapeDtypeStruct((hh, dk, t), F32), jax.ShapeDtypeStruct((hh, t, dk), F32),
                   jax.ShapeDtypeStruct((hh, t, dv), F32)],
        name="stick_att_bwd", compiler_params=_cparams(("parallel", "arbitrary")),
    )(qt, q, k, ktb, v, rs, do, dot_)


def mod_fwd(c_all, ada_w, ada_b_cols):
    nl, d, n = ada_w.shape

    def body(c_ref, w_ref, b_ref, o_ref):
        act = jax.nn.silu(c_ref[...])
        o_ref[0] = jnp.dot(act, w_ref[0], precision=lax.Precision.HIGHEST, preferred_element_type=F32) + b_ref[0]

    return pl.pallas_call(
        body, grid=(nl,),
        in_specs=[pl.BlockSpec((N_DEV, d), lambda l: (0, 0)), pl.BlockSpec((1, d, n), lambda l: (l, 0, 0)),
                  pl.BlockSpec((1, 1, n), lambda l: (l, 0, 0))],
        out_specs=pl.BlockSpec((1, N_DEV, n), lambda l: (l, 0, 0)), out_shape=jax.ShapeDtypeStruct((nl, N_DEV, n), F32),
        name="mod_fwd", compiler_params=_cparams(("parallel",)),
    )(c_all, ada_w, ada_b_cols)


def ada_w_grad(c_all_t, dmod_cols):
    d = c_all_t.shape[0]
    nl, _, n = dmod_cols.shape

    def body(c_ref, dm_ref, o_ref):
        act = jax.nn.silu(c_ref[...])
        dm = dm_ref[0]
        acc = act[:, 0:1] * dm[0:1, :]
        for e in range(1, N_DEV):
            acc = acc + act[:, e:e + 1] * dm[e:e + 1, :]
        o_ref[0] = acc

    return pl.pallas_call(
        body, grid=(nl,),
        in_specs=[pl.BlockSpec((d, N_DEV), lambda l: (0, 0)), pl.BlockSpec((1, N_DEV, n), lambda l: (l, 0, 0))],
        out_specs=pl.BlockSpec((1, d, n), lambda l: (l, 0, 0)), out_shape=jax.ShapeDtypeStruct((nl, d, n), F32),
        name="ada_w_grad", compiler_params=_cparams(("parallel",)),
    )(c_all_t, dmod_cols)


ADAM_BLOCK_BYTES = 6 * 1024 * 1024


def _rows2d(a, lead=0):
    return a.reshape(a.shape[:lead] + (-1, a.shape[-1])) if a.ndim > lead + 1 else a.reshape(a.shape[:lead] + (1, -1))


def adamw(slots, w, m, v, *, name):
    shape = w.shape
    s2 = _rows2d(slots, 1)
    w2, m2, v2 = _rows2d(w), _rows2d(m), _rows2d(v)
    ns, r, c = s2.shape
    per_row = c * (ns * s2.dtype.itemsize + 7 * 4)
    tr = r
    if r * per_row > ADAM_BLOCK_BYTES:
        tr = max(t for t in range(8, r, 8) if r % t == 0 and t * per_row <= ADAM_BLOCK_BYTES)

    def body(s_ref, w_ref, m_ref, v_ref, g_ref, d_ref, mo_ref, vo_ref):
        g = s_ref[0].astype(F32)
        for i in range(1, ns):
            g = g + s_ref[i].astype(F32)
        mn = ADAM_B1 * m_ref[...] + (1.0 - ADAM_B1) * g
        vn = ADAM_B2 * v_ref[...] + (1.0 - ADAM_B2) * jnp.square(g)
        m_hat = mn / (1.0 - ADAM_B1 ** ADAM_STEP)
        v_hat = vn / (1.0 - ADAM_B2 ** ADAM_STEP)
        g_ref[...] = g
        d_ref[...] = -ADAM_LR * (m_hat / (jnp.sqrt(v_hat) + ADAM_EPS) + ADAM_WD * w_ref[...])
        mo_ref[...] = mn
        vo_ref[...] = vn

    row = pl.BlockSpec((tr, c), lambda i: (i, 0))
    outs = pl.pallas_call(
        body, grid=(r // tr,), in_specs=[pl.BlockSpec((ns, tr, c), lambda i: (0, i, 0)), row, row, row],
        out_specs=[row] * 4, out_shape=[jax.ShapeDtypeStruct((r, c), F32)] * 4, name=name,
        compiler_params=_cparams(("parallel",)),
    )(s2, w2, m2, v2)
    return [o.reshape(shape) for o in outs]


def _exchange(srcs, *, gather, name):
    n = len(srcs)
    shapes = [tuple(s.shape) if gather else tuple(s.shape[1:]) for s in srcs]

    def body(*refs):
        src_refs, out_refs = refs[:n], refs[n:2 * n]
        send_sems, recv_sems, local_sems = refs[2 * n:]
        mx, my, mc = lax.axis_index("x"), lax.axis_index("y"), lax.axis_index("c")
        me = 4 * mx + 2 * my + mc
        peers = []
        for k in range(1, N_DEV):
            px = 1 - mx if k & 4 else mx
            py = 1 - my if k & 2 else my
            pc = 1 - mc if k & 1 else mc
            peers.append(((px, py, pc), 4 * px + 2 * py + pc))

        def part(a, idx):
            return src_refs[a] if gather else src_refs[a].at[idx]

        def copy(k, a, slot):
            dev, pid = peers[k]
            return pltpu.make_async_remote_copy(src_ref=part(a, pid), dst_ref=out_refs[a].at[slot], send_sem=send_sems.at[k, a],
                                                recv_sem=recv_sems.at[k, a], device_id=dev, device_id_type=MESH)

        local = [pltpu.make_async_copy(part(a, me), out_refs[a].at[me], local_sems.at[a]) for a in range(n)]
        for cp in local:
            cp.start()
        for a in range(n):
            for k in range(N_DEV - 1):
                copy(k, a, me).start()
        for a in range(n):
            for k in range(N_DEV - 1):
                copy(k, a, me).wait_send()
                copy(k, a, peers[k][1]).wait_recv()
        for cp in local:
            cp.wait()

    any_spec = pl.BlockSpec(memory_space=pl.ANY)
    return pl.pallas_call(
        body, in_specs=[any_spec] * n, out_specs=[any_spec] * n,
        out_shape=[jax.ShapeDtypeStruct((N_DEV,) + shp, s.dtype) for shp, s in zip(shapes, srcs)],
        scratch_shapes=[pltpu.SemaphoreType.DMA((N_DEV - 1, n)), pltpu.SemaphoreType.DMA((N_DEV - 1, n)),
                        pltpu.SemaphoreType.DMA((n,))],
        name=name, compiler_params=pltpu.CompilerParams(has_side_effects=True),
    )(*srcs)


def _to_blocks(full, ax):
    shp = full.shape
    g = full.reshape(shp[:ax] + (N_DEV, shp[ax] // N_DEV) + shp[ax + 1:])
    return jnp.moveaxis(g, ax, 0)


def _from_blocks(seg, ax):
    g = jnp.moveaxis(seg, 0, ax)
    shp = g.shape
    return g.reshape(shp[:ax] + (shp[ax] * shp[ax + 1],) + shp[ax + 2:])


def _to_heads(a, dh):
    return a.reshape(a.shape[0], -1, dh).transpose(1, 0, 2)


def _from_heads(a):
    return a.transpose(1, 0, 2).reshape(a.shape[1], -1)


def _from_heads_t(a):
    return a.transpose(2, 0, 1).reshape(a.shape[2], -1)


MLA_SCALE = (MLA_NOPE + MLA_ROPE) ** -0.5
SB_SCALE = SB_DIM ** -0.5
assert math.frexp(SB_SCALE)[0] == 0.5


def _resid_epi(acc, x, g):
    return x + g * acc, acc


def _layer_fwd(x, mod, p, cos, sin):
    t = x.shape[0]
    bk = _att_blocks(t)[1]
    sh1, sc1, g1, sh2, sc2, g2 = [mod[i:i + 1] for i in range(6)]
    h = normmod_fwd(x, p["norm1_g"], sc1, sh1)
    z = matmul(h, p["w_in"], name="mm_in")
    y_sg = sg_fwd(z, p["sg_norm_g"], p["sg_w"], p["sg_bt"])
    y_cv = conv_fwd(z, p["conv_w"])
    mq, mk, mv = mla_prep_fwd(z, cos, sin, p["gq"], p["gkv"], p["wuq"], p["wukv"])
    mot, lse = softmax_att_fwd(mq.transpose(0, 2, 1), mk, _key_blocks(mv, bk), MLA_SCALE)
    y_mla = _from_heads_t(mot).astype(BF16)
    sq, sk, sv = [_to_heads(z[:, Z_SB + i * WIDTH:Z_SB + (i + 1) * WIDTH], SB_DIM).astype(BF16) for i in range(3)]
    sq = sq * SB_SCALE
    sot, rs = stick_att_fwd(sq.transpose(0, 2, 1), sk, _key_blocks(sv, bk))
    y_sb = _from_heads_t(sot).astype(BF16)
    ys = (y_sg, y_cv, y_mla, y_sb)
    up, merged = merge_fwd(ys, p["w_branch"], z)
    x1, out = matmul(merged, p["w_out"], name="mm_out", out_dtypes=(F32, F32), epi=_resid_epi, epi_in=((x, "mn"), (g1, "n")))
    h2 = normmod_fwd(x1, p["norm2_g"], sc2, sh2)
    a, r = matmul(h2, p["w1"], name="mm_up", out_dtypes=(F32, BF16),
                  epi=lambda acc: (acc, jnp.square(jnp.maximum(acc, 0.0))))
    x2, mo = matmul(r, p["w2"], name="mm_down", out_dtypes=(F32, F32), epi=_resid_epi, epi_in=((x1, "mn"), (g2, "n")))
    saved = dict(x=x, h=h, z=z, ys=ys, mq=mq, mk=mk, mv=mv, mot=mot, lse=lse, sq=sq, sk=sk, sv=sv, rs=rs, up=up,
                 merged=merged, out=out, x1=x1, h2=h2, a=a, r=r, mo=mo)
    return x2, saved


def _layer_bwd(dx2, s, mod, p, cos, sin):
    t = dx2.shape[0]
    bk = _att_blocks(t)[1]
    sh1, sc1, g1, sh2, sc2, g2 = [mod[i:i + 1] for i in range(6)]
    g = {}
    dm, dg2 = resid_bwd(dx2, s["mo"], g2)
    da = matmul(dm, p["w2_t"], name="mm_down_dx", out_dtypes=(BF16,), epi=lambda acc, a: (acc * (2.0 * jnp.maximum(a, 0.0)),),
                epi_in=((s["a"], "mn"),))
    g["mlp_w2"] = matmul_tn(s["r"], dm, name="mm_down_dw")
    dh2 = matmul(da, p["w1_t"], name="mm_up_dx")
    g["mlp_w1"] = matmul_tn(s["h2"], da, name="mm_up_dw")
    dx1, g["norm2_g"], dsc2, dsh2 = normmod_bwd(s["x1"], p["norm2_g"], sc2, sh2, dh2, dx2)
    dout, dg1 = resid_bwd(dx1, s["out"], g1)
    dmerged = matmul(dout, p["w_out_t"], name="mm_out_dx")
    g["w_out"] = matmul_tn(s["merged"], dout, name="mm_out_dw")
    dzg, dup = merge_bwd(dmerged, s["up"], s["z"])
    dys = [matmul(dup[n], p["w_branch_t"][n], name="mm_branch_dx") for n in range(N_BRANCH)]
    g["w_branch"] = jnp.stack([matmul_tn(s["ys"][n], dup[n], name="mm_branch_dw") for n in range(N_BRANCH)])
    z = s["z"]
    dz_sg, g["sg_norm_g"], g["sg_w"], dbt = sg_bwd(z, p["sg_norm_g"], p["sg_w"], p["sg_bt"], dys[0])
    g["sg_b"] = dbt.T
    dz_cv, g["conv_w"] = conv_bwd(z, p["conv_w"], dys[1])
    do = _to_heads(dys[2], MLA_V).astype(BF16)
    dqt, dk, dv = softmax_att_bwd(s["mq"].transpose(0, 2, 1), s["mq"], s["mk"], _key_blocks(s["mk"], bk), s["mv"], s["mot"],
                                  s["lse"], do, do.transpose(0, 2, 1), MLA_SCALE)
    dz_mla, g["mla_q_norm_g"], g["mla_kv_norm_g"], g["mla_w_uq"], g["mla_w_ukv"] = mla_prep_bwd(
        z, cos, sin, p["gq"], p["gkv"], p["wuq"], p["wukv"], dqt.transpose(0, 2, 1), dk, dv)
    do = _to_heads(dys[3], SB_DIM).astype(BF16)
    dqt, dk, dv = stick_att_bwd(s["sq"].transpose(0, 2, 1), s["sq"], s["sk"], _key_blocks(s["sk"], bk), s["sv"], s["rs"],
                                do, do.transpose(0, 2, 1), SB_SCALE)
    dz_sb = jnp.concatenate([_from_heads_t(dqt), _from_heads(dk), _from_heads(dv)], axis=1).astype(BF16)
    dz = jnp.concatenate([dz_sg, dz_cv, dz_mla, dz_sb, dzg], axis=1)
    dh = matmul(dz, p["w_in_t"], name="mm_in_dx", tk=2176)
    dw_in = matmul_tn(s["h"], dz, name="mm_in_dw")
    g["w_in"] = jnp.concatenate([dw_in[:, :Z_MLA + MLA_COLS], dw_in[:, Z_SB:]], axis=1)
    dx, g["norm1_g"], dsc1, dsh1 = normmod_bwd(s["x"], p["norm1_g"], sc1, sh1, dh, dx1)
    g["mla_q_norm_g"], g["mla_kv_norm_g"] = g["mla_q_norm_g"][0], g["mla_kv_norm_g"][0]
    g["norm1_g"], g["norm2_g"], g["sg_norm_g"] = g["norm1_g"][0], g["norm2_g"][0], g["sg_norm_g"][0]
    g["ada_b"] = jnp.concatenate([dsh1, dsc1, dg1, dsh2, dsc2, dg2], axis=1)[0]
    return dx, g


WEIGHT_NAMES = ("ada_w", "ada_b", "norm1_g", "norm2_g", "w_in", "sg_norm_g", "sg_w", "sg_b", "conv_w", "mla_q_norm_g",
                "mla_w_uq", "mla_kv_norm_g", "mla_w_ukv", "w_branch", "w_out", "mlp_w1", "mlp_w2", "final_norm_g")
SHARDED = (("w_in", 2), ("mla_w_uq", 2), ("mla_w_ukv", 2), ("w_branch", 3), ("w_out", 1), ("mlp_w1", 2), ("mlp_w2", 1),
           ("conv_w", 2))
REPLICATED = ("ada_b", "norm1_g", "norm2_g", "sg_norm_g", "sg_w", "sg_b", "mla_q_norm_g", "mla_kv_norm_g", "final_norm_g")


def kernel(x, c, positions, ada_w, ada_b, norm1_g, norm2_g, w_in, sg_norm_g, sg_w, sg_b, conv_w, mla_q_norm_g, mla_w_uq, mla_kv_norm_g, mla_w_ukv, w_branch, w_out, mlp_w1, mlp_w2, final_norm_g, loss_target, m_ada_w, m_ada_b, m_norm1_g, m_norm2_g, m_w_in, m_sg_norm_g, m_sg_w, m_sg_b, m_conv_w, m_mla_q_norm_g, m_mla_w_uq, m_mla_kv_norm_g, m_mla_w_ukv, m_w_branch, m_w_out, m_mlp_w1, m_mlp_w2, m_final_norm_g, v_ada_w, v_ada_b, v_norm1_g, v_norm2_g, v_w_in, v_sg_norm_g, v_sg_w, v_sg_b, v_conv_w, v_mla_q_norm_g, v_mla_w_uq, v_mla_kv_norm_g, v_mla_w_ukv, v_w_branch, v_w_out, v_mlp_w1, v_mlp_w2, v_final_norm_g):
    given = dict(locals())
    w = {n: given[n] for n in WEIGHT_NAMES}
    m = {n: given["m_" + n] for n in WEIGHT_NAMES}
    v = {n: given["v_" + n] for n in WEIGHT_NAMES}
    xs, tgt = x[0], loss_target[0]
    nl = ada_w.shape[0]
    me = 4 * lax.axis_index("x") + 2 * lax.axis_index("y") + lax.axis_index("c")

    inv_freq = ROPE_THETA ** (-jnp.arange(0, MLA_ROPE, 2, dtype=F32) / MLA_ROPE)
    ang = positions[0].astype(F32)[:, None] * inv_freq
    cos, sin = jnp.cos(ang), jnp.sin(ang)

    c_all, conv_blocks = _exchange([c, conv_w], gather=True, name="gather_c_conv")
    c_all = c_all[:, 0]
    conv_full = _from_blocks(conv_blocks, 2)
    ncol = ada_w.shape[2]
    ada_b_cols = lax.dynamic_slice_in_dim(ada_b, me * ncol, ncol, axis=1)[:, None, :]
    mod_cols = mod_fwd(c_all, ada_w, ada_b_cols)
    got = _exchange([mod_cols], gather=True, name="gather_mod")[0]
    mod = jnp.moveaxis(lax.dynamic_index_in_dim(got, me, axis=2, keepdims=False), 0, 1).reshape(nl, 6, D_MODEL)

    names = [n for n, _ in SHARDED if n != "conv_w"]
    got = _exchange([w[n].astype(BF16) for n in names], gather=True, name="gather_weights")
    full = {n: _from_blocks(seg, ax) for (n, ax), seg in zip(SHARDED, got)}
    cut = Z_MLA + MLA_COLS
    w_in_pad = jnp.concatenate([full["w_in"][:, :, :cut], jnp.zeros((nl, D_MODEL, Z_SB - cut), BF16), full["w_in"][:, :, cut:]], axis=2)
    layers = []
    for l in range(nl):
        layers.append(dict(
            norm1_g=norm1_g[l][None], norm2_g=norm2_g[l][None], sg_norm_g=sg_norm_g[l][None], sg_w=sg_w[l], sg_bt=sg_b[l].T,
            conv_w=conv_full[l], gq=mla_q_norm_g[l][None], gkv=mla_kv_norm_g[l][None], wuq=full["mla_w_uq"][l],
            wukv=full["mla_w_ukv"][l], w_in=w_in_pad[l], w_in_t=w_in_pad[l].T, w_branch=full["w_branch"][l],
            w_branch_t=full["w_branch"][l].transpose(0, 2, 1), w_out=full["w_out"][l], w_out_t=full["w_out"][l].T,
            w1=full["mlp_w1"][l], w1_t=full["mlp_w1"][l].T, w2=full["mlp_w2"][l], w2_t=full["mlp_w2"][l].T))

    saved = []
    xc = xs
    for l in range(nl):
        xc, s = _layer_fwd(xc, mod[l], layers[l], cos, sin)
        saved.append(s)
    loss_part, dx, dgf = loss_head(xc, final_norm_g[None], tgt)
    loss = lax.psum(loss_part[0, 0], AXES)
    grads = [None] * nl
    for l in reversed(range(nl)):
        dx, grads[l] = _layer_bwd(dx, saved[l], mod[l], layers[l], cos, sin)
    gfull = {n: jnp.stack([grads[l][n] for l in range(nl)]) for n in grads[0]}
    gfull["final_norm_g"] = dgf[0]

    out = {}

    def update(n, slots):
        for kind, arr in zip(("grad", "delta", "new_m", "new_v"), adamw(slots, w[n], m[n], v[n], name="adamw_" + n)):
            out[kind, n] = arr

    got = _exchange([_to_blocks(gfull[n], ax).astype(BF16) for n, ax in SHARDED], gather=False, name="exchange_grads")
    for (n, _), slots in zip(SHARDED, got):
        update(n, slots)
    got = _exchange([gfull[n] for n in REPLICATED], gather=True, name="gather_small_grads")
    for n, slots in zip(REPLICATED, got):
        update(n, slots)
    dmod_cols = jnp.moveaxis(lax.dynamic_slice_in_dim(got[0], me * ncol, ncol, axis=2), 0, 1)
    update("ada_w", ada_w_grad(c_all.T, dmod_cols)[None])

    res = [loss, dx[None]]
    for kind in ("grad", "delta", "new_m", "new_v"):
        res += [out[kind, n] for n in WEIGHT_NAMES]
    return tuple(res)
```

```python
import functools
import math

import jax
import jax.numpy as jnp
from jax import lax
from jax.experimental import pallas as pl
from jax.experimental.pallas import tpu as pltpu

F32 = jnp.float32
BF16 = jnp.bfloat16

D_MODEL = 1024
WIDTH = 512
CHUNK = 128
SG_GROUPS = 4
HEADS = 8
MLA_NOPE = 64
MLA_ROPE = 32
MLA_V = 64
MLA_Q_RANK = 256
MLA_KV_RANK = 128
MLA_QK_PAD = 128
SB_DIM = 64
ROPE_THETA = 10000.0
NORM_EPS = 1e-6
N_BRANCH = 4
D_FF = 4096
Z_SG, Z_CONV, Z_MLA, Z_SB, Z_GATE, Z_COLS = 0, 1024, 2560, 3072, 4608, 8704
IN_COLS = 8608
MLA_COLS = MLA_Q_RANK + MLA_KV_RANK + MLA_ROPE

ADAM_LR, ADAM_B1, ADAM_B2, ADAM_EPS, ADAM_WD, ADAM_STEP = 0.001, 0.9, 0.999, 1e-08, 0.01, 10

N_DEV = 8
AXES = ("x", "y", "c")
MESH = pl.DeviceIdType.MESH
VMEM_LIMIT_V7X = 56 * 1024 * 1024
ATT_BQ = 1024
ATT_BK = 256
NEG_BIG = -1e30


def _cparams(sem=None):
    return pltpu.CompilerParams(dimension_semantics=sem, vmem_limit_bytes=VMEM_LIMIT_V7X)


def _div_tile(n, pref):
    if n <= pref:
        return n
    t = (pref // 128) * 128
    while t >= 128:
        if n % t == 0:
            return t
        t -= 128
    raise ValueError(f"no tile for {n}")


def _row_tile(t):
    return min(512, t)


def _dot(a, b, dims):
    return lax.dot_general(a.astype(BF16), b.astype(BF16), (dims, ((), ())), preferred_element_type=F32)


@jax.custom_vjp
def _mm(a, b):
    return _dot(a, b, ((1,), (0,)))


def _mm_fwd(a, b):
    return _mm(a, b), (a, b)


def _mm_bwd(res, g):
    a, b = res
    return _dot(g, b, ((1,), (1,))).astype(a.dtype), _dot(a, g, ((0,), (0,))).astype(b.dtype)


_mm.defvjp(_mm_fwd, _mm_bwd)


def _rms(x, g):
    return (x * lax.rsqrt(jnp.mean(x * x, axis=-1, keepdims=True) + NORM_EPS)) * g


def _normmod(x, g, sc, sh):
    return _rms(x, g) * (1.0 + sc) + sh


def normmod_fwd(x, g, sc, sh):
    t, d = x.shape
    tt = _row_tile(t)

    def body(x_ref, g_ref, sc_ref, sh_ref, h_ref):
        h_ref[...] = _normmod(x_ref[...], g_ref[...], sc_ref[...], sh_ref[...]).astype(BF16)

    row = pl.BlockSpec((tt, d), lambda i: (i, 0))
    vec = pl.BlockSpec((1, d), lambda i: (0, 0))
    return pl.pallas_call(
        body, grid=(t // tt,), in_specs=[row, vec, vec, vec], out_specs=row,
        out_shape=jax.ShapeDtypeStruct((t, d), BF16), name="normmod_fwd", compiler_params=_cparams(("parallel",)),
    )(x, g, sc, sh)


def normmod_bwd(x, g, sc, sh, dh, dres):
    t, d = x.shape
    tt = _row_tile(t)

    def body(x_ref, g_ref, sc_ref, sh_ref, dh_ref, dres_ref, dx_ref, dg_ref, dsc_ref, dsh_ref):
        _, vjp = jax.vjp(_normmod, x_ref[...], g_ref[...], sc_ref[...], sh_ref[...])
        dx, dg, dsc, dsh = vjp(dh_ref[...])
        dx_ref[...] = dx + dres_ref[...]

        @pl.when(pl.program_id(0) == 0)
        def _():
            dg_ref[...] = jnp.zeros_like(dg_ref)
            dsc_ref[...] = jnp.zeros_like(dsc_ref)
            dsh_ref[...] = jnp.zeros_like(dsh_ref)

        dg_ref[...] += dg
        dsc_ref[...] += dsc
        dsh_ref[...] += dsh

    row = pl.BlockSpec((tt, d), lambda i: (i, 0))
    vec = pl.BlockSpec((1, d), lambda i: (0, 0))
    vs = jax.ShapeDtypeStruct((1, d), F32)
    return pl.pallas_call(
        body, grid=(t // tt,), in_specs=[row, vec, vec, vec, row, row], out_specs=[row, vec, vec, vec],
        out_shape=[jax.ShapeDtypeStruct((t, d), F32), vs, vs, vs], name="normmod_bwd",
        compiler_params=_cparams(("arbitrary",)),
    )(x, g, sc, sh, dh, dres)


def resid_bwd(dxn, m, g):
    t, d = dxn.shape
    tt = _row_tile(t)

    def body(dx_ref, m_ref, g_ref, dm_ref, dg_ref):
        dx = dx_ref[...]
        dm_ref[...] = (dx * g_ref[...]).astype(BF16)

        @pl.when(pl.program_id(0) == 0)
        def _():
            dg_ref[...] = jnp.zeros_like(dg_ref)

        dg_ref[...] += jnp.sum(dx * m_ref[...], axis=0, keepdims=True)

    row = pl.BlockSpec((tt, d), lambda i: (i, 0))
    vec = pl.BlockSpec((1, d), lambda i: (0, 0))
    return pl.pallas_call(
        body, grid=(t // tt,), in_specs=[row, row, vec], out_specs=[row, vec],
        out_shape=[jax.ShapeDtypeStruct((t, d), BF16), jax.ShapeDtypeStruct((1, d), F32)], name="resid_bwd",
        compiler_params=_cparams(("arbitrary",)),
    )(dxn, m, g)


def loss_head(x, gf, tgt):
    t, d = x.shape
    tt = _row_tile(t)

    def f(xv, gv, tv):
        y = _rms(xv, gv)
        return 0.5 * jnp.sum(jnp.mean(jnp.square(y - tv), axis=-1))

    def body(x_ref, g_ref, t_ref, loss_ref, dx_ref, dg_ref):
        val, vjp = jax.vjp(lambda xv, gv: f(xv, gv, t_ref[...]), x_ref[...], g_ref[...])
        dx, dg = vjp(jnp.ones((), F32))
        dx_ref[...] = dx

        @pl.when(pl.program_id(0) == 0)
        def _():
            dg_ref[...] = jnp.zeros_like(dg_ref)
            loss_ref[...] = jnp.zeros_like(loss_ref)

        dg_ref[...] += dg
        loss_ref[...] += jnp.full(loss_ref.shape, val, F32)

    row = pl.BlockSpec((tt, d), lambda i: (i, 0))
    vec = pl.BlockSpec((1, d), lambda i: (0, 0))
    lsp = pl.BlockSpec((1, 128), lambda i: (0, 0))
    return pl.pallas_call(
        body, grid=(t // tt,), in_specs=[row, vec, row], out_specs=[lsp, row, vec],
        out_shape=[jax.ShapeDtypeStruct((1, 128), F32), jax.ShapeDtypeStruct((t, d), F32),
                   jax.ShapeDtypeStruct((1, d), F32)],
        name="loss_head", compiler_params=_cparams(("arbitrary",)),
    )(x, gf, tgt)


def matmul(a, b, *, name, out_dtypes=(F32,), epi=None, epi_in=(), tm=1024, tn=512, tk=1024):
    m, k = a.shape
    n = b.shape[1]
    tm, tn, tk = min(tm, m), _div_tile(n, tn), _div_tile(k, tk)
    nk = k // tk
    n_epi, n_out = len(epi_in), len(out_dtypes)

    def body(*refs):
        a_ref, b_ref = refs[:2]
        e_refs = refs[2:2 + n_epi]
        o_refs = refs[2 + n_epi:2 + n_epi + n_out]
        acc_ref = refs[-1]
        kk = pl.program_id(2)

        @pl.when(kk == 0)
        def _():
            acc_ref[...] = jnp.zeros_like(acc_ref)

        acc_ref[...] += jnp.dot(a_ref[...].astype(BF16), b_ref[...].astype(BF16), preferred_element_type=F32)

        @pl.when(kk == nk - 1)
        def _():
            acc = acc_ref[...]
            outs = (acc,) if epi is None else epi(acc, *[r[...] for r in e_refs])
            for o_ref, o in zip(o_refs, outs):
                o_ref[...] = o.astype(o_ref.dtype)

    in_specs = [pl.BlockSpec((tm, tk), lambda i, j, kk: (i, kk)), pl.BlockSpec((tk, tn), lambda i, j, kk: (kk, j))]
    for _, kind in epi_in:
        in_specs.append(pl.BlockSpec((tm, tn), lambda i, j, kk: (i, j)) if kind == "mn"
                        else pl.BlockSpec((1, tn), lambda i, j, kk: (0, j)))
    out_spec = pl.BlockSpec((tm, tn), lambda i, j, kk: (i, j))
    outs = pl.pallas_call(
        body, grid=(m // tm, n // tn, nk), in_specs=in_specs, out_specs=[out_spec] * n_out,
        out_shape=[jax.ShapeDtypeStruct((m, n), dt) for dt in out_dtypes],
        scratch_shapes=[pltpu.VMEM((tm, tn), F32)], name=name,
        compiler_params=_cparams(("parallel", "parallel", "arbitrary")),
    )(a, b, *[arr for arr, _ in epi_in])
    return outs[0] if n_out == 1 else outs


def matmul_tn(a, b, *, name, tko=1024, tn=512, tt=1024):
    t, ko = a.shape
    n = b.shape[1]
    tko, tn, tt = _div_tile(ko, tko), _div_tile(n, tn), min(tt, t)
    nt = t // tt

    def body(a_ref, b_ref, o_ref, acc_ref):
        s = pl.program_id(2)

        @pl.when(s == 0)
        def _():
            acc_ref[...] = jnp.zeros_like(acc_ref)

        acc_ref[...] += _dot(a_ref[...], b_ref[...], ((0,), (0,)))

        @pl.when(s == nt - 1)
        def _():
            o_ref[...] = acc_ref[...]

    return pl.pallas_call(
        body, grid=(ko // tko, n // tn, nt),
        in_specs=[pl.BlockSpec((tt, tko), lambda i, j, s: (s, i)), pl.BlockSpec((tt, tn), lambda i, j, s: (s, j))],
        out_specs=pl.BlockSpec((tko, tn), lambda i, j, s: (i, j)), out_shape=jax.ShapeDtypeStruct((ko, n), F32),
        scratch_shapes=[pltpu.VMEM((tko, tn), F32)], name=name,
        compiler_params=_cparams(("parallel", "parallel", "arbitrary")),
    )(a, b)


def merge_fwd(ys, w_branch, z):
    t = ys[0].shape[0]
    d = D_MODEL
    tm, tn = min(512, t), 512
    gate0 = Z_GATE // tn

    def body(y0, y1, y2, y3, w_ref, g0, g1, g2, g3, up_ref, mg_ref):
        acc = jnp.zeros((tm, tn), F32)
        for n, (y_ref, g_ref) in enumerate(zip((y0, y1, y2, y3), (g0, g1, g2, g3))):
            up = jnp.dot(y_ref[...], w_ref[n], preferred_element_type=F32)
            up_ref[n] = up
            acc = acc + jax.nn.sigmoid(g_ref[...]) * up
        mg_ref[...] = acc.astype(BF16)

    ysp = pl.BlockSpec((tm, WIDTH), lambda i, j: (i, 0))
    gate_specs = [pl.BlockSpec((tm, tn), functools.partial(lambda i, j, n: (i, gate0 + n * (d // tn) + j), n=n))
                  for n in range(N_BRANCH)]
    return pl.pallas_call(
        body, grid=(t // tm, d // tn),
        in_specs=[ysp, ysp, ysp, ysp, pl.BlockSpec((N_BRANCH, WIDTH, tn), lambda i, j: (0, 0, j))] + gate_specs,
        out_specs=[pl.BlockSpec((N_BRANCH, tm, tn), lambda i, j: (0, i, j)), pl.BlockSpec((tm, tn), lambda i, j: (i, j))],
        out_shape=[jax.ShapeDtypeStruct((N_BRANCH, t, d), F32), jax.ShapeDtypeStruct((t, d), BF16)],
        name="merge_fwd", compiler_params=_cparams(("parallel", "parallel")),
    )(*ys, w_branch, z, z, z, z)


def merge_bwd(dmerged, up, z):
    t, d = dmerged.shape
    tm, tn = min(512, t), 512
    gate0 = Z_GATE // tn
    nj = d // tn

    def body(dm_ref, up_ref, g_ref, dzg_ref, dup_ref):
        dm = dm_ref[...]
        s = jax.nn.sigmoid(g_ref[...])
        dzg_ref[...] = (dm * up_ref[0] * s * (1.0 - s)).astype(BF16)
        dup_ref[0] = (dm * s).astype(BF16)

    blk = pl.BlockSpec((1, tm, tn), lambda i, j, n: (n, i, j))
    return pl.pallas_call(
        body, grid=(t // tm, nj, N_BRANCH),
        in_specs=[pl.BlockSpec((tm, tn), lambda i, j, n: (i, j)), blk,
                  pl.BlockSpec((tm, tn), lambda i, j, n: (i, gate0 + n * nj + j))],
        out_specs=[pl.BlockSpec((tm, tn), lambda i, j, n: (i, n * nj + j)), blk],
        out_shape=[jax.ShapeDtypeStruct((t, N_BRANCH * d), BF16), jax.ShapeDtypeStruct((N_BRANCH, t, d), BF16)],
        name="merge_bwd", compiler_params=_cparams(("parallel", "parallel", "arbitrary")),
    )(dmerged, up, z)


def _sg_tile(zsg, g, w, bt):
    tt = zsg.shape[0]
    a = jax.nn.gelu(zsg)
    u, v = a[:, :WIDTH], a[:, WIDTH:]
    vc = v - jnp.mean(v, axis=-1, keepdims=True)
    v = (vc * lax.rsqrt(jnp.mean(vc * vc, axis=-1, keepdims=True) + NORM_EPS)) * g
    row = lax.broadcasted_iota(jnp.int32, (CHUNK, CHUNK), 0)
    col = lax.broadcasted_iota(jnp.int32, (CHUNK, CHUNK), 1)
    cols = []
    for gi in range(SG_GROUPS):
        wc = jnp.where(row >= col, w[gi], 0.0)
        bias = bt[:, gi:gi + 1]
        rows = [_mm(wc, v[ch * CHUNK:(ch + 1) * CHUNK, gi * CHUNK:(gi + 1) * CHUNK]) + bias for ch in range(tt // CHUNK)]
        cols.append(jnp.concatenate(rows, axis=0) if len(rows) > 1 else rows[0])
    return u * jnp.concatenate(cols, axis=1)


def sg_fwd(z, g, w, bt):
    t = z.shape[0]
    tt = _row_tile(t)

    def body(z_ref, g_ref, w_ref, b_ref, y_ref):
        y_ref[...] = _sg_tile(z_ref[...], g_ref[...], w_ref[...], b_ref[...]).astype(BF16)

    return pl.pallas_call(
        body, grid=(t // tt,),
        in_specs=[pl.BlockSpec((tt, 2 * WIDTH), lambda i: (i, 0)), pl.BlockSpec((1, WIDTH), lambda i: (0, 0)),
                  pl.BlockSpec((SG_GROUPS, CHUNK, CHUNK), lambda i: (0, 0, 0)), pl.BlockSpec((CHUNK, SG_GROUPS), lambda i: (0, 0))],
        out_specs=pl.BlockSpec((tt, WIDTH), lambda i: (i, 0)), out_shape=jax.ShapeDtypeStruct((t, WIDTH), BF16),
        name="sg_fwd", compiler_params=_cparams(("parallel",)),
    )(z, g, w, bt)


def sg_bwd(z, g, w, bt, dy):
    t = z.shape[0]
    tt = _row_tile(t)

    def body(z_ref, g_ref, w_ref, b_ref, dy_ref, dz_ref, dg_ref, dw_ref, db_ref):
        _, vjp = jax.vjp(_sg_tile, z_ref[...], g_ref[...], w_ref[...], b_ref[...])
        dz, dg, dw, db = vjp(dy_ref[...])
        dz_ref[...] = dz.astype(BF16)

        @pl.when(pl.program_id(0) == 0)
        def _():
            dg_ref[...] = jnp.zeros_like(dg_ref)
            dw_ref[...] = jnp.zeros_like(dw_ref)
            db_ref[...] = jnp.zeros_like(db_ref)

        dg_ref[...] += dg
        dw_ref[...] += dw
        db_ref[...] += db

    gs = pl.BlockSpec((1, WIDTH), lambda i: (0, 0))
    ws = pl.BlockSpec((SG_GROUPS, CHUNK, CHUNK), lambda i: (0, 0, 0))
    bs = pl.BlockSpec((CHUNK, SG_GROUPS), lambda i: (0, 0))
    return pl.pallas_call(
        body, grid=(t // tt,),
        in_specs=[pl.BlockSpec((tt, 2 * WIDTH), lambda i: (i, 0)), gs, ws, bs, pl.BlockSpec((tt, WIDTH), lambda i: (i, 0))],
        out_specs=[pl.BlockSpec((tt, 2 * WIDTH), lambda i: (i, 0)), gs, ws, bs],
        out_shape=[jax.ShapeDtypeStruct((t, 2 * WIDTH), BF16), jax.ShapeDtypeStruct((1, WIDTH), F32),
                   jax.ShapeDtypeStruct((SG_GROUPS, CHUNK, CHUNK), F32), jax.ShapeDtypeStruct((CHUNK, SG_GROUPS), F32)],
        name="sg_bwd", compiler_params=_cparams(("arbitrary",)),
    )(z, g, w, bt, dy)


HALO = 8


def _shift_down(halo, cur, k):
    tt = cur.shape[0]
    ext = jnp.concatenate([halo, cur], axis=0)
    return ext[HALO - k:HALO - k + tt]


def _shift_up(cur, halo, k):
    tt = cur.shape[0]
    ext = jnp.concatenate([cur, halo], axis=0)
    return ext[k:k + tt]


def conv_fwd(z, cw):
    t = z.shape[0]
    tt = _row_tile(t)
    cb = Z_CONV // WIDTH
    hb = tt // HALO

    def body(gb_ref, gc_ref, xv_ref, hgc_ref, hxv_ref, w_ref, y_ref):
        first = (pl.program_id(0) == 0)
        tcur = gc_ref[...] * xv_ref[...]
        thalo = jnp.where(first, 0.0, hgc_ref[...] * hxv_ref[...])
        w = w_ref[...]
        y = w[2:3] * tcur + w[1:2] * _shift_down(thalo, tcur, 1) + w[0:1] * _shift_down(thalo, tcur, 2)
        y_ref[...] = (gb_ref[...] * y).astype(BF16)

    def cur(off):
        return pl.BlockSpec((tt, WIDTH), lambda i: (i, cb + off))

    def prev(off):
        return pl.BlockSpec((HALO, WIDTH), lambda i: (jnp.maximum(i * hb - 1, 0), cb + off))

    return pl.pallas_call(
        body, grid=(t // tt,),
        in_specs=[cur(0), cur(1), cur(2), prev(1), prev(2), pl.BlockSpec((3, WIDTH), lambda i: (0, 0))],
        out_specs=pl.BlockSpec((tt, WIDTH), lambda i: (i, 0)), out_shape=jax.ShapeDtypeStruct((t, WIDTH), BF16),
        name="conv_fwd", compiler_params=_cparams(("parallel",)),
    )(z, z, z, z, z, cw)


def conv_bwd(z, cw, dy):
    t = z.shape[0]
    tt = _row_tile(t)
    nt = t // tt
    cb = Z_CONV // WIDTH
    hb = tt // HALO

    def body(gb_ref, gc_ref, xv_ref, hgc_ref, hxv_ref, ngb_ref, dy_ref, ndy_ref, w_ref, dz_ref, dw_ref):
        i = pl.program_id(0)
        gb, gc, xv = gb_ref[...], gc_ref[...], xv_ref[...]
        tcur = gc * xv
        thalo = jnp.where(i == 0, 0.0, hgc_ref[...] * hxv_ref[...])
        t1, t2 = _shift_down(thalo, tcur, 1), _shift_down(thalo, tcur, 2)
        w = w_ref[...]
        y = w[2:3] * tcur + w[1:2] * t1 + w[0:1] * t2
        dy = dy_ref[...]
        dyc = dy * gb
        dyn = jnp.where(i == nt - 1, 0.0, ndy_ref[...] * ngb_ref[...])
        dt = w[2:3] * dyc + w[1:2] * _shift_up(dyc, dyn, 1) + w[0:1] * _shift_up(dyc, dyn, 2)
        dz_ref[...] = jnp.concatenate([dy * y, dt * xv, dt * gc], axis=1).astype(BF16)

        @pl.when(i == 0)
        def _():
            dw_ref[...] = jnp.zeros_like(dw_ref)

        dw_ref[...] += jnp.concatenate([jnp.sum(dyc * t2, axis=0, keepdims=True), jnp.sum(dyc * t1, axis=0, keepdims=True),
                                        jnp.sum(dyc * tcur, axis=0, keepdims=True)], axis=0)

    def cur(off):
        return pl.BlockSpec((tt, WIDTH), lambda i: (i, cb + off))

    def prev(off):
        return pl.BlockSpec((HALO, WIDTH), lambda i: (jnp.maximum(i * hb - 1, 0), cb + off))

    last = t // HALO - 1
    nxt_z = pl.BlockSpec((HALO, WIDTH), lambda i: (jnp.minimum((i + 1) * hb, last), cb))
    nxt_dy = pl.BlockSpec((HALO, WIDTH), lambda i: (jnp.minimum((i + 1) * hb, last), 0))
    return pl.pallas_call(
        body, grid=(nt,),
        in_specs=[cur(0), cur(1), cur(2), prev(1), prev(2), nxt_z, pl.BlockSpec((tt, WIDTH), lambda i: (i, 0)), nxt_dy,
                  pl.BlockSpec((3, WIDTH), lambda i: (0, 0))],
        out_specs=[pl.BlockSpec((tt, 3 * WIDTH), lambda i: (i, 0)), pl.BlockSpec((3, WIDTH), lambda i: (0, 0))],
        out_shape=[jax.ShapeDtypeStruct((t, 3 * WIDTH), BF16), jax.ShapeDtypeStruct((3, WIDTH), F32)],
        name="conv_bwd", compiler_params=_cparams(("arbitrary",)),
    )(z, z, z, z, z, z, dy, dy, cw)


def _mla_prep(zm, cos, sin, gq, gkv, wuq, wukv):
    tt = zm.shape[0]
    cq, ckv, kpe = zm[:, :MLA_Q_RANK], zm[:, MLA_Q_RANK:MLA_Q_RANK + MLA_KV_RANK], zm[:, MLA_Q_RANK + MLA_KV_RANK:MLA_COLS]
    q = _mm(_rms(cq, gq), wuq)
    kv = _mm(_rms(ckv, gkv), wukv)
    half = MLA_ROPE // 2

    def rope(xr):
        x1, x2 = xr[:, :half], xr[:, half:]
        return jnp.concatenate([x1 * cos - x2 * sin, x2 * cos + x1 * sin], axis=-1)

    kpe = rope(kpe)
    pad = jnp.zeros((tt, MLA_QK_PAD - MLA_NOPE - MLA_ROPE), F32)
    dq, dkv = MLA_NOPE + MLA_ROPE, MLA_NOPE + MLA_V
    qs, ks, vs = [], [], []
    for h in range(HEADS):
        qs.append(jnp.concatenate([q[:, h * dq:h * dq + MLA_NOPE], rope(q[:, h * dq + MLA_NOPE:(h + 1) * dq]), pad], axis=-1))
        ks.append(jnp.concatenate([kv[:, h * dkv:h * dkv + MLA_NOPE], kpe, pad], axis=-1))
        vs.append(kv[:, h * dkv + MLA_NOPE:(h + 1) * dkv])
    return qs, ks, vs


def _mla_prep_specs(tt):
    zs = pl.BlockSpec((tt, WIDTH), lambda i: (i, Z_MLA // WIDTH))
    cs = pl.BlockSpec((tt, MLA_ROPE // 2), lambda i: (i, 0))
    return [zs, cs, cs, pl.BlockSpec((1, MLA_Q_RANK), lambda i: (0, 0)), pl.BlockSpec((1, MLA_KV_RANK), lambda i: (0, 0)),
            pl.BlockSpec((MLA_Q_RANK, HEADS * (MLA_NOPE + MLA_ROPE)), lambda i: (0, 0)),
            pl.BlockSpec((MLA_KV_RANK, HEADS * (MLA_NOPE + MLA_V)), lambda i: (0, 0))]


def mla_prep_fwd(z, cos, sin, gq, gkv, wuq, wukv):
    t = z.shape[0]
    tt = _row_tile(t)

    def body(z_ref, c_ref, s_ref, gq_ref, gkv_ref, wq_ref, wkv_ref, q_ref, k_ref, v_ref):
        qs, ks, vs = _mla_prep(z_ref[...], c_ref[...], s_ref[...], gq_ref[...], gkv_ref[...],
                               wq_ref[...].astype(F32), wkv_ref[...].astype(F32))
        for h in range(HEADS):
            q_ref[h] = qs[h].astype(BF16)
            k_ref[h] = ks[h].astype(BF16)
            v_ref[h] = vs[h].astype(BF16)

    hs = pl.BlockSpec((HEADS, tt, MLA_QK_PAD), lambda i: (0, i, 0))
    vsp = pl.BlockSpec((HEADS, tt, MLA_V), lambda i: (0, i, 0))
    return pl.pallas_call(
        body, grid=(t // tt,), in_specs=_mla_prep_specs(tt), out_specs=[hs, hs, vsp],
        out_shape=[jax.ShapeDtypeStruct((HEADS, t, MLA_QK_PAD), BF16), jax.ShapeDtypeStruct((HEADS, t, MLA_QK_PAD), BF16),
                   jax.ShapeDtypeStruct((HEADS, t, MLA_V), BF16)],
        name="mla_prep_fwd", compiler_params=_cparams(("parallel",)),
    )(z, cos, sin, gq, gkv, wuq, wukv)


def mla_prep_bwd(z, cos, sin, gq, gkv, wuq, wukv, dq, dk, dv):
    t = z.shape[0]
    tt = _row_tile(t)

    def body(z_ref, c_ref, s_ref, gq_ref, gkv_ref, wq_ref, wkv_ref, dq_ref, dk_ref, dv_ref,
             dz_ref, dgq_ref, dgkv_ref, dwq_ref, dwkv_ref):
        cos_v, sin_v = c_ref[...], s_ref[...]
        _, vjp = jax.vjp(lambda zm, a, b, wq, wkv: _mla_prep(zm, cos_v, sin_v, a, b, wq, wkv),
                         z_ref[...], gq_ref[...], gkv_ref[...], wq_ref[...].astype(F32), wkv_ref[...].astype(F32))
        cot = ([dq_ref[h] for h in range(HEADS)], [dk_ref[h] for h in range(HEADS)], [dv_ref[h] for h in range(HEADS)])
        dz, dgq, dgkv, dwq, dwkv = vjp(cot)
        dz_ref[...] = dz.astype(BF16)

        @pl.when(pl.program_id(0) == 0)
        def _():
            for r in (dgq_ref, dgkv_ref, dwq_ref, dwkv_ref):
                r[...] = jnp.zeros_like(r)

        dgq_ref[...] += dgq
        dgkv_ref[...] += dgkv
        dwq_ref[...] += dwq
        dwkv_ref[...] += dwkv

    specs = _mla_prep_specs(tt)
    hs = pl.BlockSpec((HEADS, tt, MLA_QK_PAD), lambda i: (0, i, 0))
    vsp = pl.BlockSpec((HEADS, tt, MLA_V), lambda i: (0, i, 0))
    nq, nkv = HEADS * (MLA_NOPE + MLA_ROPE), HEADS * (MLA_NOPE + MLA_V)
    return pl.pallas_call(
        body, grid=(t // tt,), in_specs=specs + [hs, hs, vsp],
        out_specs=[pl.BlockSpec((tt, WIDTH), lambda i: (i, 0)), specs[3], specs[4], specs[5], specs[6]],
        out_shape=[jax.ShapeDtypeStruct((t, WIDTH), BF16), jax.ShapeDtypeStruct((1, MLA_Q_RANK), F32),
                   jax.ShapeDtypeStruct((1, MLA_KV_RANK), F32), jax.ShapeDtypeStruct((MLA_Q_RANK, nq), F32),
                   jax.ShapeDtypeStruct((MLA_KV_RANK, nkv), F32)],
        name="mla_prep_bwd", compiler_params=_cparams(("arbitrary",)),
    )(z, cos, sin, gq, gkv, wuq, wukv, dq, dk, dv)


def _att_blocks(t):
    return min(ATT_BQ, t), min(ATT_BK, t)


def _key_blocks(a, bk):
    h, t, d = a.shape
    return a.reshape(h, t // bk, bk, d).transpose(0, 1, 3, 2)


def _att_positions(kb, qi, bq, bk):
    krow = kb * bk + lax.broadcasted_iota(jnp.int32, (bk, bq), 0)
    qcol = qi * bq + lax.broadcasted_iota(jnp.int32, (bk, bq), 1)
    return krow, qcol


def _walk(qi, nd, step, carry, diagonal_first=False):
    n_full = qi * nd
    if diagonal_first:
        carry = lax.fori_loop(0, nd, lambda i, c: step(n_full + nd - 1 - i, c, True), carry)
        return lax.fori_loop(0, n_full, lambda i, c: step(n_full - 1 - i, c, False), carry)
    carry = lax.fori_loop(0, n_full, lambda kb, c: step(kb, c, False), carry)
    return lax.fori_loop(0, nd, lambda i, c: step(n_full + i, c, True), carry)


def softmax_att_fwd(qt, k, vtb, scale):
    hh, dk, t = qt.shape
    dv = vtb.shape[2]
    bq, bk = _att_blocks(t)

    def body(qt_ref, k_ref, vt_ref, o_ref, lse_ref):
        qi = pl.program_id(1)
        qtv = qt_ref[0]

        def step(kb, carry, masked):
            m, l, acc = carry
            ks = pl.multiple_of(kb * bk, bk)
            s = jnp.dot(k_ref[0, pl.ds(ks, bk), :], qtv, preferred_element_type=F32) * scale
            if masked:
                krow, qcol = _att_positions(kb, qi, bq, bk)
                s = jnp.where(krow <= qcol, s, NEG_BIG)
            m_new = jnp.maximum(m, jnp.max(s, axis=0, keepdims=True))
            p = jnp.exp(s - m_new)
            alpha = jnp.exp(m - m_new)
            l = alpha * l + jnp.sum(p, axis=0, keepdims=True)
            acc = alpha * acc + jnp.dot(vt_ref[0, kb], p.astype(BF16), preferred_element_type=F32)
            return m_new, l, acc

        init = (jnp.full((1, bq), NEG_BIG, F32), jnp.zeros((1, bq), F32), jnp.zeros((dv, bq), F32))
        m, l, acc = _walk(qi, bq // bk, step, init)
        o_ref[0] = acc / l
        lse_ref[0] = m + jnp.log(l)

    return pl.pallas_call(
        body, grid=(hh, t // bq),
        in_specs=[pl.BlockSpec((1, dk, bq), lambda h, i: (h, 0, i)), pl.BlockSpec((1, t, dk), lambda h, i: (h, 0, 0)),
                  pl.BlockSpec((1, t // bk, dv, bk), lambda h, i: (h, 0, 0, 0))],
        out_specs=[pl.BlockSpec((1, dv, bq), lambda h, i: (h, 0, i)), pl.BlockSpec((1, 1, bq), lambda h, i: (h, 0, i))],
        out_shape=[jax.ShapeDtypeStruct((hh, dv, t), F32), jax.ShapeDtypeStruct((hh, 1, t), F32)],
        name="softmax_att_fwd", compiler_params=_cparams(("parallel", "arbitrary")),
    )(qt, k, vtb)


def softmax_att_bwd(qt, q, k, ktb, v, ot, lse, do, dot_, scale):
    hh, dk, t = qt.shape
    dv = v.shape[2]
    bq, bk = _att_blocks(t)

    def body(qt_ref, q_ref, k_ref, kt_ref, v_ref, ot_ref, lse_ref, do_ref, dot_ref, dqt_ref, dk_ref, dv_ref):
        qi = pl.program_id(1)

        @pl.when(qi == 0)
        def _():
            dk_ref[...] = jnp.zeros_like(dk_ref)
            dv_ref[...] = jnp.zeros_like(dv_ref)

        qtv, qv, dov, dotv = qt_ref[0], q_ref[0], do_ref[0], dot_ref[0]
        lse_v = lse_ref[0]
        delta = jnp.sum(dotv.astype(F32) * ot_ref[0], axis=0, keepdims=True)

        def step(kb, dqt, masked):
            ks = pl.multiple_of(kb * bk, bk)
            s = jnp.dot(k_ref[0, pl.ds(ks, bk), :], qtv, preferred_element_type=F32) * scale
            if masked:
                krow, qcol = _att_positions(kb, qi, bq, bk)
                s = jnp.where(krow <= qcol, s, NEG_BIG)
            p = jnp.exp(s - lse_v)
            dp = jnp.dot(v_ref[0, pl.ds(ks, bk), :], dotv, preferred_element_type=F32)
            ds = (p * (dp - delta) * scale).astype(BF16)
            dk_ref[0, pl.ds(ks, bk), :] += jnp.dot(ds, qv, preferred_element_type=F32)
            dv_ref[0, pl.ds(ks, bk), :] += jnp.dot(p.astype(BF16), dov, preferred_element_type=F32)
            return dqt + jnp.dot(kt_ref[0, kb], ds, preferred_element_type=F32)

        dqt_ref[0] = _walk(qi, bq // bk, step, jnp.zeros((dk, bq), F32))

    nkb = t // bk
    qts = pl.BlockSpec((1, dk, bq), lambda h, i: (h, 0, i))
    qs = pl.BlockSpec((1, bq, dk), lambda h, i: (h, i, 0))
    kfull = pl.BlockSpec((1, t, dk), lambda h, i: (h, 0, 0))
    vfull = pl.BlockSpec((1, t, dv), lambda h, i: (h, 0, 0))
    vts = pl.BlockSpec((1, dv, bq), lambda h, i: (h, 0, i))
    return pl.pallas_call(
        body, grid=(hh, t // bq),
        in_specs=[qts, qs, kfull, pl.BlockSpec((1, nkb, dk, bk), lambda h, i: (h, 0, 0, 0)), vfull, vts,
                  pl.BlockSpec((1, 1, bq), lambda h, i: (h, 0, i)), pl.BlockSpec((1, bq, dv), lambda h, i: (h, i, 0)), vts],
        out_specs=[qts, kfull, vfull],
        out_shape=[jax.ShapeDtypeStruct((hh, dk, t), F32), jax.ShapeDtypeStruct((hh, t, dk), F32),
                   jax.ShapeDtypeStruct((hh, t, dv), F32)],
        name="softmax_att_bwd", compiler_params=_cparams(("parallel", "arbitrary")),
    )(qt, q, k, ktb, v, ot, lse, do, dot_)


SUM_ROWS = 8


def _softplus(z):
    return jnp.maximum(z, 0.0) + jnp.log(1.0 + jnp.exp(-jnp.abs(z)))


def _tri_ext(bk, kind):
    r = lax.broadcasted_iota(jnp.int32, (bk + SUM_ROWS, bk), 0)
    c = lax.broadcasted_iota(jnp.int32, (bk + SUM_ROWS, bk), 1)
    if kind == "neg_later":
        return jnp.where((c > r) | (r >= bk), -1.0, 0.0).astype(BF16)
    return jnp.where((c <= r) | (r >= bk), 1.0, 0.0).astype(BF16)


def _split_dot(m01, x):
    hi = lax.bitcast_convert_type(lax.bitcast_convert_type(x, jnp.uint32) & jnp.uint32(0xFFFF0000), F32)
    lo = x - hi
    return (jnp.dot(m01, hi.astype(BF16), preferred_element_type=F32) + jnp.dot(m01, lo.astype(BF16), preferred_element_type=F32))


def _first_last_step(nh, nq):
    h, i = pl.program_id(0), pl.program_id(1)
    return (h == 0) & (i == 0), (h == nh - 1) & (i == nq - 1)


def stick_att_fwd(qt, k, vtb, comm=None):
    hh, dk, t = qt.shape
    dv = vtb.shape[2]
    bq, bk = _att_blocks(t)
    nkb = t // bk
    srcs, gather, c_specs, c_shapes, c_sems = _with_exchange(comm)
    n = len(srcs)

    def body(*refs):
        qt_ref, k_ref, vt_ref = refs[:3]
        o_ref, rs_ref = refs[3 + n:5 + n]
        if n:
            start, wait = _exchange_ops(refs[3:3 + n], refs[5 + n:5 + 2 * n], *refs[5 + 2 * n:], gather)
            first, last = _first_last_step(hh, t // bq)
            pl.when(first)(start)
        qi = pl.program_id(1)
        qtv = qt_ref[0]
        neg_later = _tri_ext(bk, "neg_later")

        def step(kb, carry, masked):
            r, acc = carry
            ks = pl.multiple_of(kb * bk, bk)
            z = jnp.dot(k_ref[0, pl.ds(ks, bk), :], qtv, preferred_element_type=F32)
            sp = _softplus(z)
            la = z - sp
            if masked:
                krow, qcol = _att_positions(kb, qi, bq, bk)
                mask = krow < qcol
                sp = jnp.where(mask, sp, 0.0)
            rs_ref[0, kb] = r
            sums = _split_dot(neg_later, sp)
            a = jnp.exp(la + sums[:bk])
            if masked:
                a = jnp.where(mask, a, 0.0)
            acc = acc + jnp.dot(vt_ref[0, kb], a.astype(BF16), preferred_element_type=F32) * jnp.exp(r)
            return r + sums[bk:bk + 1], acc

        init = (jnp.zeros((1, bq), F32), jnp.zeros((dv, bq), F32))
        o_ref[0] = _walk(qi, bq // bk, step, init, diagonal_first=True)[1]
        if n:
            pl.when(last)(wait)

    outs = pl.pallas_call(
        body, grid=(hh, t // bq),
        in_specs=[pl.BlockSpec((1, dk, bq), lambda h, i: (h, 0, i)), pl.BlockSpec((1, t, dk), lambda h, i: (h, 0, 0)),
                  pl.BlockSpec((1, nkb, dv, bk), lambda h, i: (h, 0, 0, 0))] + c_specs,
        out_specs=[pl.BlockSpec((1, dv, bq), lambda h, i: (h, 0, i)),
                   pl.BlockSpec((1, nkb, 1, bq), lambda h, i: (h, 0, 0, i))] + c_specs,
        out_shape=[jax.ShapeDtypeStruct((hh, dv, t), F32), jax.ShapeDtypeStruct((hh, nkb, 1, t), F32)] + c_shapes,
        scratch_shapes=c_sems, name="stick_att_fwd_exchange" if n else "stick_att_fwd",
        compiler_params=pltpu.CompilerParams(dimension_semantics=("arbitrary", "arbitrary") if n else ("parallel", "arbitrary"),
                                             vmem_limit_bytes=VMEM_LIMIT_V7X, has_side_effects=bool(n)),
    )(qt, k, vtb, *srcs)
    return outs[0], outs[1], outs[2:]


def stick_att_bwd(qt, q, k, ktb, v, rs, do, dot_, scale, comm=None):
    hh, dk, t = qt.shape
    dv = v.shape[2]
    bq, bk = _att_blocks(t)
    nkb = t // bk
    srcs, gather, c_specs, c_shapes, c_sems = _with_exchange(comm)
    n = len(srcs)

    def body(*refs):
        qt_ref, q_ref, k_ref, kt_ref, v_ref, rs_ref, do_ref, dot_ref = refs[:8]
        dqt_ref, dk_ref, dv_ref = refs[8 + n:11 + n]
        if n:
            start, wait = _exchange_ops(refs[8:8 + n], refs[11 + n:11 + 2 * n], *refs[11 + 2 * n:], gather)
            first, last = _first_last_step(hh, t // bq)
            pl.when(first)(start)
        qi = pl.program_id(1)

        @pl.when(qi == 0)
        def _():
            dk_ref[...] = jnp.zeros_like(dk_ref)
            dv_ref[...] = jnp.zeros_like(dv_ref)

        qtv, qv, dov, dotv = qt_ref[0], q_ref[0], do_ref[0], dot_ref[0]
        neg_later, upto = _tri_ext(bk, "neg_later"), _tri_ext(bk, "upto")

        def step(kb, carry, masked):
            pre, dqt = carry
            ks = pl.multiple_of(kb * bk, bk)
            z = jnp.dot(k_ref[0, pl.ds(ks, bk), :], qtv, preferred_element_type=F32)
            sp = _softplus(z)
            la = z - sp
            if masked:
                krow, qcol = _att_positions(kb, qi, bq, bk)
                mask = krow < qcol
                sp = jnp.where(mask, sp, 0.0)
            a = jnp.exp(la + _split_dot(neg_later, sp)[:bk] + rs_ref[0, kb])
            if masked:
                a = jnp.where(mask, a, 0.0)
            beta = jnp.exp(la)
            g = a * jnp.dot(v_ref[0, pl.ds(ks, bk), :], dotv, preferred_element_type=F32)
            sums = jnp.dot(upto, g.astype(BF16), preferred_element_type=F32)
            dz = g - beta * (pre + sums[:bk])
            if masked:
                dz = jnp.where(mask, dz, 0.0)
            dzb = dz.astype(BF16)
            dk_ref[0, pl.ds(ks, bk), :] += jnp.dot(dzb, qv, preferred_element_type=F32)
            dv_ref[0, pl.ds(ks, bk), :] += jnp.dot(a.astype(BF16), dov, preferred_element_type=F32)
            return pre + sums[bk:bk + 1], dqt + jnp.dot(kt_ref[0, kb], dzb, preferred_element_type=F32)

        init = (jnp.zeros((1, bq), F32), jnp.zeros((dk, bq), F32))
        dqt_ref[0] = _walk(qi, bq // bk, step, init)[1] * scale
        if n:
            pl.when(last)(wait)

    qts = pl.BlockSpec((1, dk, bq), lambda h, i: (h, 0, i))
    qs = pl.BlockSpec((1, bq, dk), lambda h, i: (h, i, 0))
    kfull = pl.BlockSpec((1, t, dk), lambda h, i: (h, 0, 0))
    vfull = pl.BlockSpec((1, t, dv), lambda h, i: (h, 0, 0))
    outs = pl.pallas_call(
        body, grid=(hh, t // bq),
        in_specs=[qts, qs, kfull, pl.BlockSpec((1, nkb, dk, bk), lambda h, i: (h, 0, 0, 0)), vfull,
                  pl.BlockSpec((1, nkb, 1, bq), lambda h, i: (h, 0, 0, i)), pl.BlockSpec((1, bq, dv), lambda h, i: (h, i, 0)),
                  pl.BlockSpec((1, dv, bq), lambda h, i: (h, 0, i))] + c_specs,
        out_specs=[qts, kfull, vfull] + c_specs,
        out_shape=[jax.ShapeDtypeStruct((hh, dk, t), F32), jax.ShapeDtypeStruct((hh, t, dk), F32),
                   jax.ShapeDtypeStruct((hh, t, dv), F32)] + c_shapes,
        scratch_shapes=c_sems, name="stick_att_bwd_exchange" if n else "stick_att_bwd",
        compiler_params=pltpu.CompilerParams(dimension_semantics=("arbitrary", "arbitrary") if n else ("parallel", "arbitrary"),
                                             vmem_limit_bytes=VMEM_LIMIT_V7X, has_side_effects=bool(n)),
    )(qt, q, k, ktb, v, rs, do, dot_, *srcs)
    return outs[0], outs[1], outs[2], outs[3:]


def mod_fwd(c_all, ada_w, ada_b_cols):
    nl, d, n = ada_w.shape

    def body(c_ref, w_ref, b_ref, o_ref):
        act = jax.nn.silu(c_ref[...])
        o_ref[0] = jnp.dot(act, w_ref[0], precision=lax.Precision.HIGHEST, preferred_element_type=F32) + b_ref[0]

    return pl.pallas_call(
        body, grid=(nl,),
        in_specs=[pl.BlockSpec((N_DEV, d), lambda l: (0, 0)), pl.BlockSpec((1, d, n), lambda l: (l, 0, 0)),
                  pl.BlockSpec((1, 1, n), lambda l: (l, 0, 0))],
        out_specs=pl.BlockSpec((1, N_DEV, n), lambda l: (l, 0, 0)), out_shape=jax.ShapeDtypeStruct((nl, N_DEV, n), F32),
        name="mod_fwd", compiler_params=_cparams(("parallel",)),
    )(c_all, ada_w, ada_b_cols)


def ada_w_grad(c_all_t, dmod_cols):
    d = c_all_t.shape[0]
    nl, _, n = dmod_cols.shape

    def body(c_ref, dm_ref, o_ref):
        act = jax.nn.silu(c_ref[...])
        dm = dm_ref[0]
        acc = act[:, 0:1] * dm[0:1, :]
        for e in range(1, N_DEV):
            acc = acc + act[:, e:e + 1] * dm[e:e + 1, :]
        o_ref[0] = acc

    return pl.pallas_call(
        body, grid=(nl,),
        in_specs=[pl.BlockSpec((d, N_DEV), lambda l: (0, 0)), pl.BlockSpec((1, N_DEV, n), lambda l: (l, 0, 0))],
        out_specs=pl.BlockSpec((1, d, n), lambda l: (l, 0, 0)), out_shape=jax.ShapeDtypeStruct((nl, d, n), F32),
        name="ada_w_grad", compiler_params=_cparams(("parallel",)),
    )(c_all_t, dmod_cols)


ADAM_BLOCK_BYTES = 6 * 1024 * 1024


def _rows2d(a, lead=0):
    return a.reshape(a.shape[:lead] + (-1, a.shape[-1])) if a.ndim > lead + 1 else a.reshape(a.shape[:lead] + (1, -1))


def adamw(slots, w, m, v, *, name):
    shape = w.shape
    s2 = _rows2d(slots, 1)
    w2, m2, v2 = _rows2d(w), _rows2d(m), _rows2d(v)
    ns, r, c = s2.shape
    per_row = c * (ns * s2.dtype.itemsize + 7 * 4)
    tr = r
    if r * per_row > ADAM_BLOCK_BYTES:
        tr = max(t for t in range(8, r, 8) if r % t == 0 and t * per_row <= ADAM_BLOCK_BYTES)

    def body(s_ref, w_ref, m_ref, v_ref, g_ref, d_ref, mo_ref, vo_ref):
        g = s_ref[0].astype(F32)
        for i in range(1, ns):
            g = g + s_ref[i].astype(F32)
        mn = ADAM_B1 * m_ref[...] + (1.0 - ADAM_B1) * g
        vn = ADAM_B2 * v_ref[...] + (1.0 - ADAM_B2) * jnp.square(g)
        m_hat = mn / (1.0 - ADAM_B1 ** ADAM_STEP)
        v_hat = vn / (1.0 - ADAM_B2 ** ADAM_STEP)
        g_ref[...] = g
        d_ref[...] = -ADAM_LR * (m_hat / (jnp.sqrt(v_hat) + ADAM_EPS) + ADAM_WD * w_ref[...])
        mo_ref[...] = mn
        vo_ref[...] = vn

    row = pl.BlockSpec((tr, c), lambda i: (i, 0))
    outs = pl.pallas_call(
        body, grid=(r // tr,), in_specs=[pl.BlockSpec((ns, tr, c), lambda i: (0, i, 0)), row, row, row],
        out_specs=[row] * 4, out_shape=[jax.ShapeDtypeStruct((r, c), F32)] * 4, name=name,
        compiler_params=_cparams(("parallel",)),
    )(s2, w2, m2, v2)
    return [o.reshape(shape) for o in outs]


def _exchange(srcs, *, gather, name):
    n = len(srcs)

    def body(*refs):
        start, wait = _exchange_ops(refs[:n], refs[n:2 * n], *refs[2 * n:], gather)
        start()
        wait()

    return pl.pallas_call(
        body, in_specs=[ANY_SPEC] * n, out_specs=[ANY_SPEC] * n, out_shape=_exchange_out_shapes(srcs, gather),
        scratch_shapes=_exchange_sems(n), name=name, compiler_params=pltpu.CompilerParams(has_side_effects=True),
    )(*srcs)


ANY_SPEC = pl.BlockSpec(memory_space=pl.ANY)


def _exchange_out_shapes(srcs, gather):
    return [jax.ShapeDtypeStruct((N_DEV,) + (tuple(s.shape) if gather else tuple(s.shape[1:])), s.dtype) for s in srcs]


def _exchange_sems(n):
    return [pltpu.SemaphoreType.DMA((N_DEV - 1, n)), pltpu.SemaphoreType.DMA((N_DEV - 1, n)), pltpu.SemaphoreType.DMA((n,))]


def _exchange_ops(src_refs, out_refs, send_sems, recv_sems, local_sems, gather):
    n = len(src_refs)
    mx, my, mc = lax.axis_index("x"), lax.axis_index("y"), lax.axis_index("c")
    me = 4 * mx + 2 * my + mc
    peers = []
    for k in range(1, N_DEV):
        px = 1 - mx if k & 4 else mx
        py = 1 - my if k & 2 else my
        pc = 1 - mc if k & 1 else mc
        peers.append(((px, py, pc), 4 * px + 2 * py + pc))

    def part(a, idx):
        return src_refs[a] if gather else src_refs[a].at[idx]

    def copy(k, a, slot):
        dev, pid = peers[k]
        return pltpu.make_async_remote_copy(src_ref=part(a, pid), dst_ref=out_refs[a].at[slot], send_sem=send_sems.at[k, a],
                                            recv_sem=recv_sems.at[k, a], device_id=dev, device_id_type=MESH)

    def local(a):
        return pltpu.make_async_copy(part(a, me), out_refs[a].at[me], local_sems.at[a])

    def start():
        for a in range(n):
            local(a).start()
            for k in range(N_DEV - 1):
                copy(k, a, me).start()

    def wait():
        for a in range(n):
            for k in range(N_DEV - 1):
                copy(k, a, me).wait_send()
                copy(k, a, peers[k][1]).wait_recv()
            local(a).wait()

    return start, wait


def _with_exchange(comm):
    srcs, gather = comm if comm is not None else ((), True)
    n = len(srcs)
    return (list(srcs), gather, [ANY_SPEC] * n, _exchange_out_shapes(srcs, gather) if n else [], _exchange_sems(n) if n else [])


def _to_blocks(full, ax):
    shp = full.shape
    g = full.reshape(shp[:ax] + (N_DEV, shp[ax] // N_DEV) + shp[ax + 1:])
    return jnp.moveaxis(g, ax, 0)


def _from_blocks(seg, ax):
    g = jnp.moveaxis(seg, 0, ax)
    shp = g.shape
    return g.reshape(shp[:ax] + (shp[ax] * shp[ax + 1],) + shp[ax + 2:])


def _to_heads(a, dh):
    return a.reshape(a.shape[0], -1, dh).transpose(1, 0, 2)


def _from_heads(a):
    return a.transpose(1, 0, 2).reshape(a.shape[1], -1)


def _from_heads_t(a):
    return a.transpose(2, 0, 1).reshape(a.shape[2], -1)


MLA_SCALE = (MLA_NOPE + MLA_ROPE) ** -0.5
SB_SCALE = SB_DIM ** -0.5
assert math.frexp(SB_SCALE)[0] == 0.5


def _resid_epi(acc, x, g):
    return x + g * acc, acc


def _layer_fwd(x, mod, p, cos, sin, comm=None):
    t = x.shape[0]
    bk = _att_blocks(t)[1]
    sh1, sc1, g1, sh2, sc2, g2 = [mod[i:i + 1] for i in range(6)]
    h = normmod_fwd(x, p["norm1_g"], sc1, sh1)
    z = matmul(h, p["w_in"], name="mm_in")
    y_sg = sg_fwd(z, p["sg_norm_g"], p["sg_w"], p["sg_bt"])
    y_cv = conv_fwd(z, p["conv_w"])
    mq, mk, mv = mla_prep_fwd(z, cos, sin, p["gq"], p["gkv"], p["wuq"], p["wukv"])
    mot, lse = softmax_att_fwd(mq.transpose(0, 2, 1), mk, _key_blocks(mv, bk), MLA_SCALE)
    y_mla = _from_heads_t(mot).astype(BF16)
    sq, sk, sv = [_to_heads(z[:, Z_SB + i * WIDTH:Z_SB + (i + 1) * WIDTH], SB_DIM).astype(BF16) for i in range(3)]
    sq = sq * SB_SCALE
    sot, rs, got = stick_att_fwd(sq.transpose(0, 2, 1), sk, _key_blocks(sv, bk), comm)
    y_sb = _from_heads_t(sot).astype(BF16)
    ys = (y_sg, y_cv, y_mla, y_sb)
    up, merged = merge_fwd(ys, p["w_branch"], z)
    x1, out = matmul(merged, p["w_out"], name="mm_out", out_dtypes=(F32, F32), epi=_resid_epi, epi_in=((x, "mn"), (g1, "n")))
    h2 = normmod_fwd(x1, p["norm2_g"], sc2, sh2)
    a, r = matmul(h2, p["w1"], name="mm_up", out_dtypes=(F32, BF16),
                  epi=lambda acc: (acc, jnp.square(jnp.maximum(acc, 0.0))))
    x2, mo = matmul(r, p["w2"], name="mm_down", out_dtypes=(F32, F32), epi=_resid_epi, epi_in=((x1, "mn"), (g2, "n")))
    saved = dict(x=x, h=h, z=z, ys=ys, mq=mq, mk=mk, mv=mv, mot=mot, lse=lse, sq=sq, sk=sk, sv=sv, rs=rs, up=up,
                 merged=merged, out=out, x1=x1, h2=h2, a=a, r=r, mo=mo)
    return x2, saved, got


def _layer_bwd(dx2, s, mod, p, cos, sin, comm=None):
    t = dx2.shape[0]
    bk = _att_blocks(t)[1]
    sh1, sc1, g1, sh2, sc2, g2 = [mod[i:i + 1] for i in range(6)]
    g = {}
    dm, dg2 = resid_bwd(dx2, s["mo"], g2)
    da = matmul(dm, p["w2_t"], name="mm_down_dx", out_dtypes=(BF16,), epi=lambda acc, a: (acc * (2.0 * jnp.maximum(a, 0.0)),),
                epi_in=((s["a"], "mn"),))
    g["mlp_w2"] = matmul_tn(s["r"], dm, name="mm_down_dw")
    dh2 = matmul(da, p["w1_t"], name="mm_up_dx")
    g["mlp_w1"] = matmul_tn(s["h2"], da, name="mm_up_dw")
    dx1, g["norm2_g"], dsc2, dsh2 = normmod_bwd(s["x1"], p["norm2_g"], sc2, sh2, dh2, dx2)
    dout, dg1 = resid_bwd(dx1, s["out"], g1)
    dmerged = matmul(dout, p["w_out_t"], name="mm_out_dx")
    g["w_out"] = matmul_tn(s["merged"], dout, name="mm_out_dw")
    dzg, dup = merge_bwd(dmerged, s["up"], s["z"])
    dys = [matmul(dup[n], p["w_branch_t"][n], name="mm_branch_dx") for n in range(N_BRANCH)]
    g["w_branch"] = jnp.stack([matmul_tn(s["ys"][n], dup[n], name="mm_branch_dw") for n in range(N_BRANCH)])
    z = s["z"]
    dz_sg, g["sg_norm_g"], g["sg_w"], dbt = sg_bwd(z, p["sg_norm_g"], p["sg_w"], p["sg_bt"], dys[0])
    g["sg_b"] = dbt.T
    dz_cv, g["conv_w"] = conv_bwd(z, p["conv_w"], dys[1])
    do = _to_heads(dys[2], MLA_V).astype(BF16)
    dqt, dk, dv = softmax_att_bwd(s["mq"].transpose(0, 2, 1), s["mq"], s["mk"], _key_blocks(s["mk"], bk), s["mv"], s["mot"],
                                  s["lse"], do, do.transpose(0, 2, 1), MLA_SCALE)
    dz_mla, g["mla_q_norm_g"], g["mla_kv_norm_g"], g["mla_w_uq"], g["mla_w_ukv"] = mla_prep_bwd(
        z, cos, sin, p["gq"], p["gkv"], p["wuq"], p["wukv"], dqt.transpose(0, 2, 1), dk, dv)
    do = _to_heads(dys[3], SB_DIM).astype(BF16)
    dqt, dk, dv, got = stick_att_bwd(s["sq"].transpose(0, 2, 1), s["sq"], s["sk"], _key_blocks(s["sk"], bk), s["sv"], s["rs"],
                                     do, do.transpose(0, 2, 1), SB_SCALE, comm)
    dz_sb = jnp.concatenate([_from_heads_t(dqt), _from_heads(dk), _from_heads(dv)], axis=1).astype(BF16)
    dz = jnp.concatenate([dz_sg, dz_cv, dz_mla, dz_sb, dzg], axis=1)
    dh = matmul(dz, p["w_in_t"], name="mm_in_dx", tk=2176)
    dw_in = matmul_tn(s["h"], dz, name="mm_in_dw")
    g["w_in"] = jnp.concatenate([dw_in[:, :Z_MLA + MLA_COLS], dw_in[:, Z_SB:]], axis=1)
    dx, g["norm1_g"], dsc1, dsh1 = normmod_bwd(s["x"], p["norm1_g"], sc1, sh1, dh, dx1)
    g["mla_q_norm_g"], g["mla_kv_norm_g"] = g["mla_q_norm_g"][0], g["mla_kv_norm_g"][0]
    g["norm1_g"], g["norm2_g"], g["sg_norm_g"] = g["norm1_g"][0], g["norm2_g"][0], g["sg_norm_g"][0]
    g["ada_b"] = jnp.concatenate([dsh1, dsc1, dg1, dsh2, dsc2, dg2], axis=1)[0]
    return dx, g, got


WEIGHT_NAMES = ("ada_w", "ada_b", "norm1_g", "norm2_g", "w_in", "sg_norm_g", "sg_w", "sg_b", "conv_w", "mla_q_norm_g",
                "mla_w_uq", "mla_kv_norm_g", "mla_w_ukv", "w_branch", "w_out", "mlp_w1", "mlp_w2", "final_norm_g")
SHARDED = (("w_in", 2), ("mla_w_uq", 2), ("mla_w_ukv", 2), ("w_branch", 3), ("w_out", 1), ("mlp_w1", 2), ("mlp_w2", 1),
           ("conv_w", 2))
REPLICATED = ("ada_b", "norm1_g", "norm2_g", "sg_norm_g", "sg_w", "sg_b", "mla_q_norm_g", "mla_kv_norm_g", "final_norm_g")


def kernel(x, c, positions, ada_w, ada_b, norm1_g, norm2_g, w_in, sg_norm_g, sg_w, sg_b, conv_w, mla_q_norm_g, mla_w_uq, mla_kv_norm_g, mla_w_ukv, w_branch, w_out, mlp_w1, mlp_w2, final_norm_g, loss_target, m_ada_w, m_ada_b, m_norm1_g, m_norm2_g, m_w_in, m_sg_norm_g, m_sg_w, m_sg_b, m_conv_w, m_mla_q_norm_g, m_mla_w_uq, m_mla_kv_norm_g, m_mla_w_ukv, m_w_branch, m_w_out, m_mlp_w1, m_mlp_w2, m_final_norm_g, v_ada_w, v_ada_b, v_norm1_g, v_norm2_g, v_w_in, v_sg_norm_g, v_sg_w, v_sg_b, v_conv_w, v_mla_q_norm_g, v_mla_w_uq, v_mla_kv_norm_g, v_mla_w_ukv, v_w_branch, v_w_out, v_mlp_w1, v_mlp_w2, v_final_norm_g):
    given = dict(locals())
    w = {n: given[n] for n in WEIGHT_NAMES}
    m = {n: given["m_" + n] for n in WEIGHT_NAMES}
    v = {n: given["v_" + n] for n in WEIGHT_NAMES}
    xs, tgt = x[0], loss_target[0]
    nl = ada_w.shape[0]
    me = 4 * lax.axis_index("x") + 2 * lax.axis_index("y") + lax.axis_index("c")

    inv_freq = ROPE_THETA ** (-jnp.arange(0, MLA_ROPE, 2, dtype=F32) / MLA_ROPE)
    ang = positions[0].astype(F32)[:, None] * inv_freq
    cos, sin = jnp.cos(ang), jnp.sin(ang)

    c_all, conv_blocks = _exchange([c, conv_w], gather=True, name="gather_c_conv")
    c_all = c_all[:, 0]
    conv_full = _from_blocks(conv_blocks, 2)
    ncol = ada_w.shape[2]
    ada_b_cols = lax.dynamic_slice_in_dim(ada_b, me * ncol, ncol, axis=1)[:, None, :]
    mod_cols = mod_fwd(c_all, ada_w, ada_b_cols)
    got = _exchange([mod_cols], gather=True, name="gather_mod")[0]
    mod = jnp.moveaxis(lax.dynamic_index_in_dim(got, me, axis=2, keepdims=False), 0, 1).reshape(nl, 6, D_MODEL)

    names = [n for n, _ in SHARDED if n != "conv_w"]
    cut = Z_MLA + MLA_COLS

    def weight_shards(lo, hi):
        return [w[n][lo:hi].astype(BF16) for n in names]

    def layer_weights(got, lo, hi):
        full = {n: _from_blocks(seg, ax) for (n, ax), seg in zip(SHARDED, got)}
        w_in_pad = jnp.concatenate([full["w_in"][:, :, :cut], jnp.zeros((hi - lo, D_MODEL, Z_SB - cut), BF16),
                                    full["w_in"][:, :, cut:]], axis=2)
        return [dict(
            norm1_g=norm1_g[l][None], norm2_g=norm2_g[l][None], sg_norm_g=sg_norm_g[l][None], sg_w=sg_w[l], sg_bt=sg_b[l].T,
            conv_w=conv_full[l], gq=mla_q_norm_g[l][None], gkv=mla_kv_norm_g[l][None], wuq=full["mla_w_uq"][i],
            wukv=full["mla_w_ukv"][i], w_in=w_in_pad[i], w_in_t=w_in_pad[i].T, w_branch=full["w_branch"][i],
            w_branch_t=full["w_branch"][i].transpose(0, 2, 1), w_out=full["w_out"][i], w_out_t=full["w_out"][i].T,
            w1=full["mlp_w1"][i], w1_t=full["mlp_w1"][i].T, w2=full["mlp_w2"][i], w2_t=full["mlp_w2"][i].T)
            for i, l in enumerate(range(lo, hi))]

    layers = layer_weights(_exchange(weight_shards(0, 1), gather=True, name="gather_weights_first"), 0, 1)

    saved = []
    xc = xs
    for l in range(nl):
        comm = (weight_shards(1, nl), True) if l == 0 and nl > 1 else None
        xc, s, got = _layer_fwd(xc, mod[l], layers[l], cos, sin, comm)
        saved.append(s)
        if comm is not None:
            layers += layer_weights(got, 1, nl)
    loss_part, dx, dgf = loss_head(xc, final_norm_g[None], tgt)
    loss = lax.psum(loss_part[0, 0], AXES)

    def grad_blocks(g):
        return [_to_blocks(g[n][None], ax).astype(BF16) for n, ax in SHARDED]

    grads, landed = [None] * nl, [None] * nl
    for l in reversed(range(nl)):
        comm = (grad_blocks(grads[l + 1]), False) if l + 1 < nl else None
        dx, grads[l], got = _layer_bwd(dx, saved[l], mod[l], layers[l], cos, sin, comm)
        if comm is not None:
            landed[l + 1] = got
    landed[0] = _exchange(grad_blocks(grads[0]), gather=False, name="exchange_grads_first")
    gfull = {n: jnp.stack([grads[l][n] for l in range(nl)]) for n in REPLICATED if n != "final_norm_g"}
    gfull["final_norm_g"] = dgf[0]

    out = {}

    def update(n, slots):
        for kind, arr in zip(("grad", "delta", "new_m", "new_v"), adamw(slots, w[n], m[n], v[n], name="adamw_" + n)):
            out[kind, n] = arr

    for i, (n, _) in enumerate(SHARDED):
        update(n, jnp.concatenate([landed[l][i] for l in range(nl)], axis=1))
    got = _exchange([gfull[n] for n in REPLICATED], gather=True, name="gather_small_grads")
    for n, slots in zip(REPLICATED, got):
        update(n, slots)
    dmod_cols = jnp.moveaxis(lax.dynamic_slice_in_dim(got[0], me * ncol, ncol, axis=2), 0, 1)
    update("ada_w", ada_w_grad(c_all.T, dmod_cols)[None])

    res = [loss, dx[None]]
    for kind in ("grad", "delta", "new_m", "new_v"):
        res += [out[kind, n] for n in WEIGHT_NAMES]
    return tuple(res)
```

```python
import functools
import math

import jax
import jax.numpy as jnp
from jax import lax
from jax.experimental import pallas as pl
from jax.experimental.pallas import tpu as pltpu

F32 = jnp.float32
BF16 = jnp.bfloat16

D_MODEL = 1024
WIDTH = 512
CHUNK = 128
SG_GROUPS = 4
HEADS = 8
MLA_NOPE = 64
MLA_ROPE = 32
MLA_V = 64
MLA_Q_RANK = 256
MLA_KV_RANK = 128
MLA_QK_PAD = 128
SB_DIM = 64
ROPE_THETA = 10000.0
NORM_EPS = 1e-6
N_BRANCH = 4
D_FF = 4096
Z_SG, Z_CONV, Z_MLA, Z_SB, Z_GATE, Z_COLS = 0, 1024, 2560, 3072, 4608, 8704
IN_COLS = 8608
MLA_COLS = MLA_Q_RANK + MLA_KV_RANK + MLA_ROPE

ADAM_LR, ADAM_B1, ADAM_B2, ADAM_EPS, ADAM_WD, ADAM_STEP = 0.001, 0.9, 0.999, 1e-08, 0.01, 10

N_DEV = 8
AXES = ("x", "y", "c")
MESH = pl.DeviceIdType.MESH
VMEM_LIMIT_V7X = 56 * 1024 * 1024
ATT_BQ = 1024
ATT_BK = 256
NEG_BIG = -1e30


def _cparams(sem=None):
    return pltpu.CompilerParams(dimension_semantics=sem, vmem_limit_bytes=VMEM_LIMIT_V7X)


def _div_tile(n, pref):
    if n <= pref:
        return n
    t = (pref // 128) * 128
    while t >= 128:
        if n % t == 0:
            return t
        t -= 128
    raise ValueError(f"no tile for {n}")


def _row_tile(t):
    return min(512, t)


def _dot(a, b, dims):
    return lax.dot_general(a.astype(BF16), b.astype(BF16), (dims, ((), ())), preferred_element_type=F32)


@jax.custom_vjp
def _mm(a, b):
    return _dot(a, b, ((1,), (0,)))


def _mm_fwd(a, b):
    return _mm(a, b), (a, b)


def _mm_bwd(res, g):
    a, b = res
    return _dot(g, b, ((1,), (1,))).astype(a.dtype), _dot(a, g, ((0,), (0,))).astype(b.dtype)


_mm.defvjp(_mm_fwd, _mm_bwd)


def _rms(x, g):
    return (x * lax.rsqrt(jnp.mean(x * x, axis=-1, keepdims=True) + NORM_EPS)) * g


def _normmod(x, g, sc, sh):
    return _rms(x, g) * (1.0 + sc) + sh


def normmod_fwd(x, g, sc, sh):
    t, d = x.shape
    tt = _row_tile(t)

    def body(x_ref, g_ref, sc_ref, sh_ref, h_ref):
        h_ref[...] = _normmod(x_ref[...], g_ref[...], sc_ref[...], sh_ref[...]).astype(BF16)

    row = pl.BlockSpec((tt, d), lambda i: (i, 0))
    vec = pl.BlockSpec((1, d), lambda i: (0, 0))
    return pl.pallas_call(
        body, grid=(t // tt,), in_specs=[row, vec, vec, vec], out_specs=row,
        out_shape=jax.ShapeDtypeStruct((t, d), BF16), name="normmod_fwd", compiler_params=_cparams(("parallel",)),
    )(x, g, sc, sh)


def normmod_bwd(x, g, sc, sh, dh, dres):
    t, d = x.shape
    tt = _row_tile(t)

    def body(x_ref, g_ref, sc_ref, sh_ref, dh_ref, dres_ref, dx_ref, dg_ref, dsc_ref, dsh_ref):
        _, vjp = jax.vjp(_normmod, x_ref[...], g_ref[...], sc_ref[...], sh_ref[...])
        dx, dg, dsc, dsh = vjp(dh_ref[...])
        dx_ref[...] = dx + dres_ref[...]

        @pl.when(pl.program_id(0) == 0)
        def _():
            dg_ref[...] = jnp.zeros_like(dg_ref)
            dsc_ref[...] = jnp.zeros_like(dsc_ref)
            dsh_ref[...] = jnp.zeros_like(dsh_ref)

        dg_ref[...] += dg
        dsc_ref[...] += dsc
        dsh_ref[...] += dsh

    row = pl.BlockSpec((tt, d), lambda i: (i, 0))
    vec = pl.BlockSpec((1, d), lambda i: (0, 0))
    vs = jax.ShapeDtypeStruct((1, d), F32)
    return pl.pallas_call(
        body, grid=(t // tt,), in_specs=[row, vec, vec, vec, row, row], out_specs=[row, vec, vec, vec],
        out_shape=[jax.ShapeDtypeStruct((t, d), F32), vs, vs, vs], name="normmod_bwd",
        compiler_params=_cparams(("arbitrary",)),
    )(x, g, sc, sh, dh, dres)


def resid_bwd(dxn, m, g):
    t, d = dxn.shape
    tt = _row_tile(t)

    def body(dx_ref, m_ref, g_ref, dm_ref, dg_ref):
        dx = dx_ref[...]
        dm_ref[...] = (dx * g_ref[...]).astype(BF16)

        @pl.when(pl.program_id(0) == 0)
        def _():
            dg_ref[...] = jnp.zeros_like(dg_ref)

        dg_ref[...] += jnp.sum(dx * m_ref[...], axis=0, keepdims=True)

    row = pl.BlockSpec((tt, d), lambda i: (i, 0))
    vec = pl.BlockSpec((1, d), lambda i: (0, 0))
    return pl.pallas_call(
        body, grid=(t // tt,), in_specs=[row, row, vec], out_specs=[row, vec],
        out_shape=[jax.ShapeDtypeStruct((t, d), BF16), jax.ShapeDtypeStruct((1, d), F32)], name="resid_bwd",
        compiler_params=_cparams(("arbitrary",)),
    )(dxn, m, g)


def loss_head(x, gf, tgt):
    t, d = x.shape
    tt = _row_tile(t)

    def f(xv, gv, tv):
        y = _rms(xv, gv)
        return 0.5 * jnp.sum(jnp.mean(jnp.square(y - tv), axis=-1))

    def body(x_ref, g_ref, t_ref, loss_ref, dx_ref, dg_ref):
        val, vjp = jax.vjp(lambda xv, gv: f(xv, gv, t_ref[...]), x_ref[...], g_ref[...])
        dx, dg = vjp(jnp.ones((), F32))
        dx_ref[...] = dx

        @pl.when(pl.program_id(0) == 0)
        def _():
            dg_ref[...] = jnp.zeros_like(dg_ref)
            loss_ref[...] = jnp.zeros_like(loss_ref)

        dg_ref[...] += dg
        loss_ref[...] += jnp.full(loss_ref.shape, val, F32)

    row = pl.BlockSpec((tt, d), lambda i: (i, 0))
    vec = pl.BlockSpec((1, d), lambda i: (0, 0))
    lsp = pl.BlockSpec((1, 128), lambda i: (0, 0))
    return pl.pallas_call(
        body, grid=(t // tt,), in_specs=[row, vec, row], out_specs=[lsp, row, vec],
        out_shape=[jax.ShapeDtypeStruct((1, 128), F32), jax.ShapeDtypeStruct((t, d), F32),
                   jax.ShapeDtypeStruct((1, d), F32)],
        name="loss_head", compiler_params=_cparams(("arbitrary",)),
    )(x, gf, tgt)


def matmul(a, b, *, name, out_dtypes=(F32,), epi=None, epi_in=(), tm=2048, tn=512, tk=1024):
    m, k = a.shape
    n = b.shape[1]
    tm, tn, tk = min(tm, m), _div_tile(n, tn), _div_tile(k, tk)
    nk = k // tk
    n_epi, n_out = len(epi_in), len(out_dtypes)

    def body(*refs):
        a_ref, b_ref = refs[:2]
        e_refs = refs[2:2 + n_epi]
        o_refs = refs[2 + n_epi:2 + n_epi + n_out]
        kk = pl.program_id(2)
        part = jnp.dot(a_ref[...].astype(BF16), b_ref[...].astype(BF16), preferred_element_type=F32)

        def finish(acc):
            outs = (acc,) if epi is None else epi(acc, *[r[...] for r in e_refs])
            for o_ref, o in zip(o_refs, outs):
                o_ref[...] = o.astype(o_ref.dtype)

        if nk == 1:
            finish(part)
            return
        acc_ref = refs[-1]

        @pl.when(kk == 0)
        def _():
            acc_ref[...] = part

        @pl.when((kk > 0) & (kk < nk - 1))
        def _():
            acc_ref[...] += part

        @pl.when(kk == nk - 1)
        def _():
            finish(acc_ref[...] + part)

    in_specs = [pl.BlockSpec((tm, tk), lambda i, j, kk: (i, kk)), pl.BlockSpec((tk, tn), lambda i, j, kk: (kk, j))]
    for _, kind in epi_in:
        in_specs.append(pl.BlockSpec((tm, tn), lambda i, j, kk: (i, j)) if kind == "mn"
                        else pl.BlockSpec((1, tn), lambda i, j, kk: (0, j)))
    out_spec = pl.BlockSpec((tm, tn), lambda i, j, kk: (i, j))
    outs = pl.pallas_call(
        body, grid=(m // tm, n // tn, nk), in_specs=in_specs, out_specs=[out_spec] * n_out,
        out_shape=[jax.ShapeDtypeStruct((m, n), dt) for dt in out_dtypes],
        scratch_shapes=[pltpu.VMEM((tm, tn), F32)] if nk > 1 else [], name=name,
        compiler_params=_cparams(("parallel", "parallel", "arbitrary")),
    )(a, b, *[arr for arr, _ in epi_in])
    return outs[0] if n_out == 1 else outs


def matmul_tn(a, b, *, name, tko=1024, tn=512, tt=2048):
    t, ko = a.shape
    n = b.shape[1]
    tko, tn, tt = _div_tile(ko, tko), _div_tile(n, tn), min(tt, t)
    nt = t // tt

    def body(a_ref, b_ref, o_ref):
        s = pl.program_id(2)
        part = _dot(a_ref[...], b_ref[...], ((0,), (0,)))

        @pl.when(s == 0)
        def _():
            o_ref[...] = part

        @pl.when(s > 0)
        def _():
            o_ref[...] += part

    return pl.pallas_call(
        body, grid=(ko // tko, n // tn, nt),
        in_specs=[pl.BlockSpec((tt, tko), lambda i, j, s: (s, i)), pl.BlockSpec((tt, tn), lambda i, j, s: (s, j))],
        out_specs=pl.BlockSpec((tko, tn), lambda i, j, s: (i, j)), out_shape=jax.ShapeDtypeStruct((ko, n), F32),
        name=name, compiler_params=_cparams(("parallel", "parallel", "arbitrary")),
    )(a, b)


def merge_fwd(ys, w_branch, z):
    t = ys[0].shape[0]
    d = D_MODEL
    tm, tn = min(512, t), 512
    gate0 = Z_GATE // tn

    def body(y0, y1, y2, y3, w_ref, g0, g1, g2, g3, up_ref, mg_ref):
        acc = jnp.zeros((tm, tn), F32)
        for n, (y_ref, g_ref) in enumerate(zip((y0, y1, y2, y3), (g0, g1, g2, g3))):
            up = jnp.dot(y_ref[...], w_ref[n], preferred_element_type=F32)
            up_ref[n] = up.astype(BF16)
            acc = acc + jax.nn.sigmoid(g_ref[...]) * up
        mg_ref[...] = acc.astype(BF16)

    ysp = pl.BlockSpec((tm, WIDTH), lambda i, j: (i, 0))
    gate_specs = [pl.BlockSpec((tm, tn), functools.partial(lambda i, j, n: (i, gate0 + n * (d // tn) + j), n=n))
                  for n in range(N_BRANCH)]
    return pl.pallas_call(
        body, grid=(t // tm, d // tn),
        in_specs=[ysp, ysp, ysp, ysp, pl.BlockSpec((N_BRANCH, WIDTH, tn), lambda i, j: (0, 0, j))] + gate_specs,
        out_specs=[pl.BlockSpec((N_BRANCH, tm, tn), lambda i, j: (0, i, j)), pl.BlockSpec((tm, tn), lambda i, j: (i, j))],
        out_shape=[jax.ShapeDtypeStruct((N_BRANCH, t, d), BF16), jax.ShapeDtypeStruct((t, d), BF16)],
        name="merge_fwd", compiler_params=_cparams(("parallel", "parallel")),
    )(*ys, w_branch, z, z, z, z)


def merge_bwd(dmerged, up, z):
    t, d = dmerged.shape
    tm, tn = min(512, t), 512
    gate0 = Z_GATE // tn
    nj = d // tn

    def body(dm_ref, up_ref, g_ref, dzg_ref, dup_ref):
        dm = dm_ref[...]
        s = jax.nn.sigmoid(g_ref[...])
        dzg_ref[...] = (dm * up_ref[0] * s * (1.0 - s)).astype(BF16)
        dup_ref[0] = (dm * s).astype(BF16)

    blk = pl.BlockSpec((1, tm, tn), lambda i, j, n: (n, i, j))
    return pl.pallas_call(
        body, grid=(t // tm, nj, N_BRANCH),
        in_specs=[pl.BlockSpec((tm, tn), lambda i, j, n: (i, j)), blk,
                  pl.BlockSpec((tm, tn), lambda i, j, n: (i, gate0 + n * nj + j))],
        out_specs=[pl.BlockSpec((tm, tn), lambda i, j, n: (i, n * nj + j)), blk],
        out_shape=[jax.ShapeDtypeStruct((t, N_BRANCH * d), BF16), jax.ShapeDtypeStruct((N_BRANCH, t, d), BF16)],
        name="merge_bwd", compiler_params=_cparams(("parallel", "parallel", "arbitrary")),
    )(dmerged, up, z)


def _sg_tile(zsg, g, w, bt):
    tt = zsg.shape[0]
    a = jax.nn.gelu(zsg)
    u, v = a[:, :WIDTH], a[:, WIDTH:]
    vc = v - jnp.mean(v, axis=-1, keepdims=True)
    v = (vc * lax.rsqrt(jnp.mean(vc * vc, axis=-1, keepdims=True) + NORM_EPS)) * g
    row = lax.broadcasted_iota(jnp.int32, (CHUNK, CHUNK), 0)
    col = lax.broadcasted_iota(jnp.int32, (CHUNK, CHUNK), 1)
    cols = []
    for gi in range(SG_GROUPS):
        wc = jnp.where(row >= col, w[gi], 0.0)
        bias = bt[:, gi:gi + 1]
        rows = [_mm(wc, v[ch * CHUNK:(ch + 1) * CHUNK, gi * CHUNK:(gi + 1) * CHUNK]) + bias for ch in range(tt // CHUNK)]
        cols.append(jnp.concatenate(rows, axis=0) if len(rows) > 1 else rows[0])
    return u * jnp.concatenate(cols, axis=1)


def sg_fwd(z, g, w, bt):
    t = z.shape[0]
    tt = _row_tile(t)

    def body(z_ref, g_ref, w_ref, b_ref, y_ref):
        y_ref[...] = _sg_tile(z_ref[...], g_ref[...], w_ref[...], b_ref[...]).astype(BF16)

    return pl.pallas_call(
        body, grid=(t // tt,),
        in_specs=[pl.BlockSpec((tt, 2 * WIDTH), lambda i: (i, 0)), pl.BlockSpec((1, WIDTH), lambda i: (0, 0)),
                  pl.BlockSpec((SG_GROUPS, CHUNK, CHUNK), lambda i: (0, 0, 0)), pl.BlockSpec((CHUNK, SG_GROUPS), lambda i: (0, 0))],
        out_specs=pl.BlockSpec((tt, WIDTH), lambda i: (i, 0)), out_shape=jax.ShapeDtypeStruct((t, WIDTH), BF16),
        name="sg_fwd", compiler_params=_cparams(("parallel",)),
    )(z, g, w, bt)


def sg_bwd(z, g, w, bt, dy):
    t = z.shape[0]
    tt = _row_tile(t)

    def body(z_ref, g_ref, w_ref, b_ref, dy_ref, dz_ref, dg_ref, dw_ref, db_ref):
        _, vjp = jax.vjp(_sg_tile, z_ref[...], g_ref[...], w_ref[...], b_ref[...])
        dz, dg, dw, db = vjp(dy_ref[...])
        dz_ref[...] = dz.astype(BF16)

        @pl.when(pl.program_id(0) == 0)
        def _():
            dg_ref[...] = jnp.zeros_like(dg_ref)
            dw_ref[...] = jnp.zeros_like(dw_ref)
            db_ref[...] = jnp.zeros_like(db_ref)

        dg_ref[...] += dg
        dw_ref[...] += dw
        db_ref[...] += db

    gs = pl.BlockSpec((1, WIDTH), lambda i: (0, 0))
    ws = pl.BlockSpec((SG_GROUPS, CHUNK, CHUNK), lambda i: (0, 0, 0))
    bs = pl.BlockSpec((CHUNK, SG_GROUPS), lambda i: (0, 0))
    return pl.pallas_call(
        body, grid=(t // tt,),
        in_specs=[pl.BlockSpec((tt, 2 * WIDTH), lambda i: (i, 0)), gs, ws, bs, pl.BlockSpec((tt, WIDTH), lambda i: (i, 0))],
        out_specs=[pl.BlockSpec((tt, 2 * WIDTH), lambda i: (i, 0)), gs, ws, bs],
        out_shape=[jax.ShapeDtypeStruct((t, 2 * WIDTH), BF16), jax.ShapeDtypeStruct((1, WIDTH), F32),
                   jax.ShapeDtypeStruct((SG_GROUPS, CHUNK, CHUNK), F32), jax.ShapeDtypeStruct((CHUNK, SG_GROUPS), F32)],
        name="sg_bwd", compiler_params=_cparams(("arbitrary",)),
    )(z, g, w, bt, dy)


HALO = 8


def _shift_down(halo, cur, k):
    tt = cur.shape[0]
    ext = jnp.concatenate([halo, cur], axis=0)
    return ext[HALO - k:HALO - k + tt]


def _shift_up(cur, halo, k):
    tt = cur.shape[0]
    ext = jnp.concatenate([cur, halo], axis=0)
    return ext[k:k + tt]


def conv_fwd(z, cw):
    t = z.shape[0]
    tt = _row_tile(t)
    cb = Z_CONV // WIDTH
    hb = tt // HALO

    def body(gb_ref, gc_ref, xv_ref, hgc_ref, hxv_ref, w_ref, y_ref):
        first = (pl.program_id(0) == 0)
        tcur = gc_ref[...] * xv_ref[...]
        thalo = jnp.where(first, 0.0, hgc_ref[...] * hxv_ref[...])
        w = w_ref[...]
        y = w[2:3] * tcur + w[1:2] * _shift_down(thalo, tcur, 1) + w[0:1] * _shift_down(thalo, tcur, 2)
        y_ref[...] = (gb_ref[...] * y).astype(BF16)

    def cur(off):
        return pl.BlockSpec((tt, WIDTH), lambda i: (i, cb + off))

    def prev(off):
        return pl.BlockSpec((HALO, WIDTH), lambda i: (jnp.maximum(i * hb - 1, 0), cb + off))

    return pl.pallas_call(
        body, grid=(t // tt,),
        in_specs=[cur(0), cur(1), cur(2), prev(1), prev(2), pl.BlockSpec((3, WIDTH), lambda i: (0, 0))],
        out_specs=pl.BlockSpec((tt, WIDTH), lambda i: (i, 0)), out_shape=jax.ShapeDtypeStruct((t, WIDTH), BF16),
        name="conv_fwd", compiler_params=_cparams(("parallel",)),
    )(z, z, z, z, z, cw)


def conv_bwd(z, cw, dy):
    t = z.shape[0]
    tt = _row_tile(t)
    nt = t // tt
    cb = Z_CONV // WIDTH
    hb = tt // HALO

    def body(gb_ref, gc_ref, xv_ref, hgc_ref, hxv_ref, ngb_ref, dy_ref, ndy_ref, w_ref, dz_ref, dw_ref):
        i = pl.program_id(0)
        gb, gc, xv = gb_ref[...], gc_ref[...], xv_ref[...]
        tcur = gc * xv
        thalo = jnp.where(i == 0, 0.0, hgc_ref[...] * hxv_ref[...])
        t1, t2 = _shift_down(thalo, tcur, 1), _shift_down(thalo, tcur, 2)
        w = w_ref[...]
        y = w[2:3] * tcur + w[1:2] * t1 + w[0:1] * t2
        dy = dy_ref[...]
        dyc = dy * gb
        dyn = jnp.where(i == nt - 1, 0.0, ndy_ref[...] * ngb_ref[...])
        dt = w[2:3] * dyc + w[1:2] * _shift_up(dyc, dyn, 1) + w[0:1] * _shift_up(dyc, dyn, 2)
        dz_ref[...] = jnp.concatenate([dy * y, dt * xv, dt * gc], axis=1).astype(BF16)

        @pl.when(i == 0)
        def _():
            dw_ref[...] = jnp.zeros_like(dw_ref)

        dw_ref[...] += jnp.concatenate([jnp.sum(dyc * t2, axis=0, keepdims=True), jnp.sum(dyc * t1, axis=0, keepdims=True),
                                        jnp.sum(dyc * tcur, axis=0, keepdims=True)], axis=0)

    def cur(off):
        return pl.BlockSpec((tt, WIDTH), lambda i: (i, cb + off))

    def prev(off):
        return pl.BlockSpec((HALO, WIDTH), lambda i: (jnp.maximum(i * hb - 1, 0), cb + off))

    last = t // HALO - 1
    nxt_z = pl.BlockSpec((HALO, WIDTH), lambda i: (jnp.minimum((i + 1) * hb, last), cb))
    nxt_dy = pl.BlockSpec((HALO, WIDTH), lambda i: (jnp.minimum((i + 1) * hb, last), 0))
    return pl.pallas_call(
        body, grid=(nt,),
        in_specs=[cur(0), cur(1), cur(2), prev(1), prev(2), nxt_z, pl.BlockSpec((tt, WIDTH), lambda i: (i, 0)), nxt_dy,
                  pl.BlockSpec((3, WIDTH), lambda i: (0, 0))],
        out_specs=[pl.BlockSpec((tt, 3 * WIDTH), lambda i: (i, 0)), pl.BlockSpec((3, WIDTH), lambda i: (0, 0))],
        out_shape=[jax.ShapeDtypeStruct((t, 3 * WIDTH), BF16), jax.ShapeDtypeStruct((3, WIDTH), F32)],
        name="conv_bwd", compiler_params=_cparams(("arbitrary",)),
    )(z, z, z, z, z, z, dy, dy, cw)


def _mla_prep(zm, cos, sin, gq, gkv, wuq, wukv):
    tt = zm.shape[0]
    cq, ckv, kpe = zm[:, :MLA_Q_RANK], zm[:, MLA_Q_RANK:MLA_Q_RANK + MLA_KV_RANK], zm[:, MLA_Q_RANK + MLA_KV_RANK:MLA_COLS]
    q = _mm(_rms(cq, gq), wuq)
    kv = _mm(_rms(ckv, gkv), wukv)
    half = MLA_ROPE // 2

    def rope(xr):
        x1, x2 = xr[:, :half], xr[:, half:]
        return jnp.concatenate([x1 * cos - x2 * sin, x2 * cos + x1 * sin], axis=-1)

    kpe = rope(kpe)
    pad = jnp.zeros((tt, MLA_QK_PAD - MLA_NOPE - MLA_ROPE), F32)
    dq, dkv = MLA_NOPE + MLA_ROPE, MLA_NOPE + MLA_V
    qs, ks, vs = [], [], []
    for h in range(HEADS):
        qs.append(jnp.concatenate([q[:, h * dq:h * dq + MLA_NOPE], rope(q[:, h * dq + MLA_NOPE:(h + 1) * dq]), pad], axis=-1))
        ks.append(jnp.concatenate([kv[:, h * dkv:h * dkv + MLA_NOPE], kpe, pad], axis=-1))
        vs.append(kv[:, h * dkv + MLA_NOPE:(h + 1) * dkv])
    return qs, ks, vs


def _mla_prep_specs(tt):
    zs = pl.BlockSpec((tt, WIDTH), lambda i: (i, Z_MLA // WIDTH))
    cs = pl.BlockSpec((tt, MLA_ROPE // 2), lambda i: (i, 0))
    return [zs, cs, cs, pl.BlockSpec((1, MLA_Q_RANK), lambda i: (0, 0)), pl.BlockSpec((1, MLA_KV_RANK), lambda i: (0, 0)),
            pl.BlockSpec((MLA_Q_RANK, HEADS * (MLA_NOPE + MLA_ROPE)), lambda i: (0, 0)),
            pl.BlockSpec((MLA_KV_RANK, HEADS * (MLA_NOPE + MLA_V)), lambda i: (0, 0))]


def mla_prep_fwd(z, cos, sin, gq, gkv, wuq, wukv):
    t = z.shape[0]
    tt = _row_tile(t)

    def body(z_ref, c_ref, s_ref, gq_ref, gkv_ref, wq_ref, wkv_ref, q_ref, k_ref, v_ref):
        qs, ks, vs = _mla_prep(z_ref[...], c_ref[...], s_ref[...], gq_ref[...], gkv_ref[...],
                               wq_ref[...].astype(F32), wkv_ref[...].astype(F32))
        for h in range(HEADS):
            q_ref[h] = qs[h].astype(BF16)
            k_ref[h] = ks[h].astype(BF16)
            v_ref[h] = vs[h].astype(BF16)

    hs = pl.BlockSpec((HEADS, tt, MLA_QK_PAD), lambda i: (0, i, 0))
    vsp = pl.BlockSpec((HEADS, tt, MLA_V), lambda i: (0, i, 0))
    return pl.pallas_call(
        body, grid=(t // tt,), in_specs=_mla_prep_specs(tt), out_specs=[hs, hs, vsp],
        out_shape=[jax.ShapeDtypeStruct((HEADS, t, MLA_QK_PAD), BF16), jax.ShapeDtypeStruct((HEADS, t, MLA_QK_PAD), BF16),
                   jax.ShapeDtypeStruct((HEADS, t, MLA_V), BF16)],
        name="mla_prep_fwd", compiler_params=_cparams(("parallel",)),
    )(z, cos, sin, gq, gkv, wuq, wukv)


def mla_prep_bwd(z, cos, sin, gq, gkv, wuq, wukv, dq, dk, dv):
    t = z.shape[0]
    tt = _row_tile(t)

    def body(z_ref, c_ref, s_ref, gq_ref, gkv_ref, wq_ref, wkv_ref, dq_ref, dk_ref, dv_ref,
             dz_ref, dgq_ref, dgkv_ref, dwq_ref, dwkv_ref):
        cos_v, sin_v = c_ref[...], s_ref[...]
        _, vjp = jax.vjp(lambda zm, a, b, wq, wkv: _mla_prep(zm, cos_v, sin_v, a, b, wq, wkv),
                         z_ref[...], gq_ref[...], gkv_ref[...], wq_ref[...].astype(F32), wkv_ref[...].astype(F32))
        cot = ([dq_ref[h] for h in range(HEADS)], [dk_ref[h] for h in range(HEADS)], [dv_ref[h] for h in range(HEADS)])
        dz, dgq, dgkv, dwq, dwkv = vjp(cot)
        dz_ref[...] = dz.astype(BF16)

        @pl.when(pl.program_id(0) == 0)
        def _():
            for r in (dgq_ref, dgkv_ref, dwq_ref, dwkv_ref):
                r[...] = jnp.zeros_like(r)

        dgq_ref[...] += dgq
        dgkv_ref[...] += dgkv
        dwq_ref[...] += dwq
        dwkv_ref[...] += dwkv

    specs = _mla_prep_specs(tt)
    hs = pl.BlockSpec((HEADS, tt, MLA_QK_PAD), lambda i: (0, i, 0))
    vsp = pl.BlockSpec((HEADS, tt, MLA_V), lambda i: (0, i, 0))
    nq, nkv = HEADS * (MLA_NOPE + MLA_ROPE), HEADS * (MLA_NOPE + MLA_V)
    return pl.pallas_call(
        body, grid=(t // tt,), in_specs=specs + [hs, hs, vsp],
        out_specs=[pl.BlockSpec((tt, WIDTH), lambda i: (i, 0)), specs[3], specs[4], specs[5], specs[6]],
        out_shape=[jax.ShapeDtypeStruct((t, WIDTH), BF16), jax.ShapeDtypeStruct((1, MLA_Q_RANK), F32),
                   jax.ShapeDtypeStruct((1, MLA_KV_RANK), F32), jax.ShapeDtypeStruct((MLA_Q_RANK, nq), F32),
                   jax.ShapeDtypeStruct((MLA_KV_RANK, nkv), F32)],
        name="mla_prep_bwd", compiler_params=_cparams(("arbitrary",)),
    )(z, cos, sin, gq, gkv, wuq, wukv, dq, dk, dv)


def _att_blocks(t):
    return min(ATT_BQ, t), min(ATT_BK, t)


def _key_blocks(a, bk):
    h, t, d = a.shape
    return a.reshape(h, t // bk, bk, d).transpose(0, 1, 3, 2)


def _att_positions(kb, qi, bq, bk):
    krow = kb * bk + lax.broadcasted_iota(jnp.int32, (bk, bq), 0)
    qcol = qi * bq + lax.broadcasted_iota(jnp.int32, (bk, bq), 1)
    return krow, qcol


def _walk(qi, nd, step, carry, diagonal_first=False):
    n_full = qi * nd
    if diagonal_first:
        carry = lax.fori_loop(0, nd, lambda i, c: step(n_full + nd - 1 - i, c, True), carry)
        return lax.fori_loop(0, n_full, lambda i, c: step(n_full - 1 - i, c, False), carry)
    carry = lax.fori_loop(0, n_full, lambda kb, c: step(kb, c, False), carry)
    return lax.fori_loop(0, nd, lambda i, c: step(n_full + i, c, True), carry)


def softmax_att_fwd(qt, k, vtb, scale):
    hh, dk, t = qt.shape
    dv = vtb.shape[2]
    bq, bk = _att_blocks(t)

    def body(qt_ref, k_ref, vt_ref, o_ref, lse_ref):
        qi = pl.program_id(1)
        qtv = qt_ref[0]

        def step(kb, carry, masked):
            m, l, acc = carry
            ks = pl.multiple_of(kb * bk, bk)
            s = jnp.dot(k_ref[0, pl.ds(ks, bk), :], qtv, preferred_element_type=F32) * scale
            if masked:
                krow, qcol = _att_positions(kb, qi, bq, bk)
                s = jnp.where(krow <= qcol, s, NEG_BIG)
            m_new = jnp.maximum(m, jnp.max(s, axis=0, keepdims=True))
            p = jnp.exp(s - m_new)
            alpha = jnp.exp(m - m_new)
            l = alpha * l + jnp.sum(p, axis=0, keepdims=True)
            acc = alpha * acc + jnp.dot(vt_ref[0, kb], p.astype(BF16), preferred_element_type=F32)
            return m_new, l, acc

        init = (jnp.full((1, bq), NEG_BIG, F32), jnp.zeros((1, bq), F32), jnp.zeros((dv, bq), F32))
        m, l, acc = _walk(qi, bq // bk, step, init)
        o_ref[0] = acc / l
        lse_ref[0] = m + jnp.log(l)

    return pl.pallas_call(
        body, grid=(hh, t // bq),
        in_specs=[pl.BlockSpec((1, dk, bq), lambda h, i: (h, 0, i)), pl.BlockSpec((1, t, dk), lambda h, i: (h, 0, 0)),
                  pl.BlockSpec((1, t // bk, dv, bk), lambda h, i: (h, 0, 0, 0))],
        out_specs=[pl.BlockSpec((1, dv, bq), lambda h, i: (h, 0, i)), pl.BlockSpec((1, 1, bq), lambda h, i: (h, 0, i))],
        out_shape=[jax.ShapeDtypeStruct((hh, dv, t), F32), jax.ShapeDtypeStruct((hh, 1, t), F32)],
        name="softmax_att_fwd", compiler_params=_cparams(("parallel", "arbitrary")),
    )(qt, k, vtb)


def softmax_att_bwd(qt, q, k, ktb, v, ot, lse, do, dot_, scale):
    hh, dk, t = qt.shape
    dv = v.shape[2]
    bq, bk = _att_blocks(t)

    def body(qt_ref, q_ref, k_ref, kt_ref, v_ref, ot_ref, lse_ref, do_ref, dot_ref, dqt_ref, dk_ref, dv_ref):
        qi = pl.program_id(1)

        @pl.when(qi == 0)
        def _():
            dk_ref[...] = jnp.zeros_like(dk_ref)
            dv_ref[...] = jnp.zeros_like(dv_ref)

        qtv, qv, dov, dotv = qt_ref[0], q_ref[0], do_ref[0], dot_ref[0]
        lse_v = lse_ref[0]
        delta = jnp.sum(dotv.astype(F32) * ot_ref[0], axis=0, keepdims=True)

        def step(kb, dqt, masked):
            ks = pl.multiple_of(kb * bk, bk)
            s = jnp.dot(k_ref[0, pl.ds(ks, bk), :], qtv, preferred_element_type=F32) * scale
            if masked:
                krow, qcol = _att_positions(kb, qi, bq, bk)
                s = jnp.where(krow <= qcol, s, NEG_BIG)
            p = jnp.exp(s - lse_v)
            dp = jnp.dot(v_ref[0, pl.ds(ks, bk), :], dotv, preferred_element_type=F32)
            ds = (p * (dp - delta) * scale).astype(BF16)
            dk_ref[0, pl.ds(ks, bk), :] += jnp.dot(ds, qv, preferred_element_type=F32)
            dv_ref[0, pl.ds(ks, bk), :] += jnp.dot(p.astype(BF16), dov, preferred_element_type=F32)
            return dqt + jnp.dot(kt_ref[0, kb], ds, preferred_element_type=F32)

        dqt_ref[0] = _walk(qi, bq // bk, step, jnp.zeros((dk, bq), F32))

    nkb = t // bk
    qts = pl.BlockSpec((1, dk, bq), lambda h, i: (h, 0, i))
    qs = pl.BlockSpec((1, bq, dk), lambda h, i: (h, i, 0))
    kfull = pl.BlockSpec((1, t, dk), lambda h, i: (h, 0, 0))
    vfull = pl.BlockSpec((1, t, dv), lambda h, i: (h, 0, 0))
    vts = pl.BlockSpec((1, dv, bq), lambda h, i: (h, 0, i))
    return pl.pallas_call(
        body, grid=(hh, t // bq),
        in_specs=[qts, qs, kfull, pl.BlockSpec((1, nkb, dk, bk), lambda h, i: (h, 0, 0, 0)), vfull, vts,
                  pl.BlockSpec((1, 1, bq), lambda h, i: (h, 0, i)), pl.BlockSpec((1, bq, dv), lambda h, i: (h, i, 0)), vts],
        out_specs=[qts, kfull, vfull],
        out_shape=[jax.ShapeDtypeStruct((hh, dk, t), F32), jax.ShapeDtypeStruct((hh, t, dk), F32),
                   jax.ShapeDtypeStruct((hh, t, dv), F32)],
        name="softmax_att_bwd", compiler_params=_cparams(("parallel", "arbitrary")),
    )(qt, q, k, ktb, v, ot, lse, do, dot_)


SUM_ROWS = 8


def _softplus(z):
    return jnp.maximum(z, 0.0) + jnp.log(1.0 + jnp.exp(-jnp.abs(z)))


def _tri_ext(bk, kind):
    r = lax.broadcasted_iota(jnp.int32, (bk + SUM_ROWS, bk), 0)
    c = lax.broadcasted_iota(jnp.int32, (bk + SUM_ROWS, bk), 1)
    if kind == "neg_later":
        return jnp.where((c > r) | (r >= bk), -1.0, 0.0).astype(BF16)
    return jnp.where((c <= r) | (r >= bk), 1.0, 0.0).astype(BF16)


def _split_dot(m01, x):
    hi = lax.bitcast_convert_type(lax.bitcast_convert_type(x, jnp.uint32) & jnp.uint32(0xFFFF0000), F32)
    lo = x - hi
    return (jnp.dot(m01, hi.astype(BF16), preferred_element_type=F32) + jnp.dot(m01, lo.astype(BF16), preferred_element_type=F32))


def sb_cast(z):
    t = z.shape[0]
    tt = _row_tile(t)
    cols = 3 * WIDTH
    assert Z_SB % cols == 0

    def body(z_ref, o_ref):
        o_ref[...] = z_ref[...].astype(BF16)

    return pl.pallas_call(
        body, grid=(t // tt,), in_specs=[pl.BlockSpec((tt, cols), lambda i: (i, Z_SB // cols))],
        out_specs=pl.BlockSpec((tt, cols), lambda i: (i, 0)), out_shape=jax.ShapeDtypeStruct((t, cols), BF16),
        name="sb_cast", compiler_params=_cparams(("parallel",)),
    )(z)


def _first_last_step(nh, nq):
    h, i = pl.program_id(0), pl.program_id(1)
    return (h == 0) & (i == 0), (h == nh - 1) & (i == nq - 1)


def stick_att_fwd(qt, k, vtb, comm=None):
    hh, dk, t = qt.shape
    dv = vtb.shape[2]
    bq, bk = _att_blocks(t)
    nkb = t // bk
    srcs, gather, c_specs, c_shapes, c_sems = _with_exchange(comm)
    n = len(srcs)

    def body(*refs):
        qt_ref, k_ref, vt_ref = refs[:3]
        o_ref, rs_ref = refs[3 + n:5 + n]
        if n:
            start, wait = _exchange_ops(refs[3:3 + n], refs[5 + n:5 + 2 * n], *refs[5 + 2 * n:], gather)
            first, last = _first_last_step(hh, t // bq)
            pl.when(first)(start)
        qi = pl.program_id(1)
        qtv = qt_ref[0]
        neg_later = _tri_ext(bk, "neg_later")

        def step(kb, carry, masked):
            r, acc = carry
            ks = pl.multiple_of(kb * bk, bk)
            z = jnp.dot(k_ref[0, pl.ds(ks, bk), :], qtv, preferred_element_type=F32)
            sp = _softplus(z)
            la = z - sp
            if masked:
                krow, qcol = _att_positions(kb, qi, bq, bk)
                mask = krow < qcol
                sp = jnp.where(mask, sp, 0.0)
            rs_ref[0, kb] = r
            sums = _split_dot(neg_later, sp)
            a = jnp.exp(la + sums[:bk])
            if masked:
                a = jnp.where(mask, a, 0.0)
            acc = acc + jnp.dot(vt_ref[0, kb], a.astype(BF16), preferred_element_type=F32) * jnp.exp(r)
            return r + sums[bk:bk + 1], acc

        init = (jnp.zeros((1, bq), F32), jnp.zeros((dv, bq), F32))
        o_ref[0] = _walk(qi, bq // bk, step, init, diagonal_first=True)[1]
        if n:
            pl.when(last)(wait)

    outs = pl.pallas_call(
        body, grid=(hh, t // bq),
        in_specs=[pl.BlockSpec((1, dk, bq), lambda h, i: (h, 0, i)), pl.BlockSpec((1, t, dk), lambda h, i: (h, 0, 0)),
                  pl.BlockSpec((1, nkb, dv, bk), lambda h, i: (h, 0, 0, 0))] + c_specs,
        out_specs=[pl.BlockSpec((1, dv, bq), lambda h, i: (h, 0, i)),
                   pl.BlockSpec((1, nkb, 1, bq), lambda h, i: (h, 0, 0, i))] + c_specs,
        out_shape=[jax.ShapeDtypeStruct((hh, dv, t), F32), jax.ShapeDtypeStruct((hh, nkb, 1, t), F32)] + c_shapes,
        scratch_shapes=c_sems, name="stick_att_fwd_exchange" if n else "stick_att_fwd",
        compiler_params=pltpu.CompilerParams(dimension_semantics=("arbitrary", "arbitrary") if n else ("parallel", "arbitrary"),
                                             vmem_limit_bytes=VMEM_LIMIT_V7X, has_side_effects=bool(n)),
    )(qt, k, vtb, *srcs)
    return outs[0], outs[1], outs[2:]


def stick_att_bwd(qt, q, k, ktb, v, rs, do, dot_, scale, comm=None):
    hh, dk, t = qt.shape
    dv = v.shape[2]
    bq, bk = _att_blocks(t)
    nkb = t // bk
    srcs, gather, c_specs, c_shapes, c_sems = _with_exchange(comm)
    n = len(srcs)

    def body(*refs):
        qt_ref, q_ref, k_ref, kt_ref, v_ref, rs_ref, do_ref, dot_ref = refs[:8]
        dqt_ref, dk_ref, dv_ref = refs[8 + n:11 + n]
        if n:
            start, wait = _exchange_ops(refs[8:8 + n], refs[11 + n:11 + 2 * n], *refs[11 + 2 * n:], gather)
            first, last = _first_last_step(hh, t // bq)
            pl.when(first)(start)
        qi = pl.program_id(1)

        @pl.when(qi == 0)
        def _():
            dk_ref[...] = jnp.zeros_like(dk_ref)
            dv_ref[...] = jnp.zeros_like(dv_ref)

        qtv, qv, dov, dotv = qt_ref[0], q_ref[0], do_ref[0], dot_ref[0]
        neg_later, upto = _tri_ext(bk, "neg_later"), _tri_ext(bk, "upto")

        def step(kb, carry, masked):
            pre, dqt = carry
            ks = pl.multiple_of(kb * bk, bk)
            z = jnp.dot(k_ref[0, pl.ds(ks, bk), :], qtv, preferred_element_type=F32)
            sp = _softplus(z)
            la = z - sp
            if masked:
                krow, qcol = _att_positions(kb, qi, bq, bk)
                mask = krow < qcol
                sp = jnp.where(mask, sp, 0.0)
            a = jnp.exp(la + _split_dot(neg_later, sp)[:bk] + rs_ref[0, kb])
            if masked:
                a = jnp.where(mask, a, 0.0)
            beta = jnp.exp(la)
            g = a * jnp.dot(v_ref[0, pl.ds(ks, bk), :], dotv, preferred_element_type=F32)
            sums = jnp.dot(upto, g.astype(BF16), preferred_element_type=F32)
            dz = g - beta * (pre + sums[:bk])
            if masked:
                dz = jnp.where(mask, dz, 0.0)
            dzb = dz.astype(BF16)
            dk_ref[0, pl.ds(ks, bk), :] += jnp.dot(dzb, qv, preferred_element_type=F32)
            dv_ref[0, pl.ds(ks, bk), :] += jnp.dot(a.astype(BF16), dov, preferred_element_type=F32)
            return pre + sums[bk:bk + 1], dqt + jnp.dot(kt_ref[0, kb], dzb, preferred_element_type=F32)

        init = (jnp.zeros((1, bq), F32), jnp.zeros((dk, bq), F32))
        dqt_ref[0] = _walk(qi, bq // bk, step, init)[1] * scale
        if n:
            pl.when(last)(wait)

    qts = pl.BlockSpec((1, dk, bq), lambda h, i: (h, 0, i))
    qs = pl.BlockSpec((1, bq, dk), lambda h, i: (h, i, 0))
    kfull = pl.BlockSpec((1, t, dk), lambda h, i: (h, 0, 0))
    vfull = pl.BlockSpec((1, t, dv), lambda h, i: (h, 0, 0))
    outs = pl.pallas_call(
        body, grid=(hh, t // bq),
        in_specs=[qts, qs, kfull, pl.BlockSpec((1, nkb, dk, bk), lambda h, i: (h, 0, 0, 0)), vfull,
                  pl.BlockSpec((1, nkb, 1, bq), lambda h, i: (h, 0, 0, i)), pl.BlockSpec((1, bq, dv), lambda h, i: (h, i, 0)),
                  pl.BlockSpec((1, dv, bq), lambda h, i: (h, 0, i))] + c_specs,
        out_specs=[qts, kfull, vfull] + c_specs,
        out_shape=[jax.ShapeDtypeStruct((hh, dk, t), F32), jax.ShapeDtypeStruct((hh, t, dk), F32),
                   jax.ShapeDtypeStruct((hh, t, dv), F32)] + c_shapes,
        scratch_shapes=c_sems, name="stick_att_bwd_exchange" if n else "stick_att_bwd",
        compiler_params=pltpu.CompilerParams(dimension_semantics=("arbitrary", "arbitrary") if n else ("parallel", "arbitrary"),
                                             vmem_limit_bytes=VMEM_LIMIT_V7X, has_side_effects=bool(n)),
    )(qt, q, k, ktb, v, rs, do, dot_, *srcs)
    return outs[0], outs[1], outs[2], outs[3:]


def mod_fwd(c_all, ada_w, ada_b_cols):
    nl, d, n = ada_w.shape

    def body(c_ref, w_ref, b_ref, o_ref):
        act = jax.nn.silu(c_ref[...])
        o_ref[0] = jnp.dot(act, w_ref[0], precision=lax.Precision.HIGHEST, preferred_element_type=F32) + b_ref[0]

    return pl.pallas_call(
        body, grid=(nl,),
        in_specs=[pl.BlockSpec((N_DEV, d), lambda l: (0, 0)), pl.BlockSpec((1, d, n), lambda l: (l, 0, 0)),
                  pl.BlockSpec((1, 1, n), lambda l: (l, 0, 0))],
        out_specs=pl.BlockSpec((1, N_DEV, n), lambda l: (l, 0, 0)), out_shape=jax.ShapeDtypeStruct((nl, N_DEV, n), F32),
        name="mod_fwd", compiler_params=_cparams(("parallel",)),
    )(c_all, ada_w, ada_b_cols)


def ada_w_grad(c_all_t, dmod_cols):
    d = c_all_t.shape[0]
    nl, _, n = dmod_cols.shape

    def body(c_ref, dm_ref, o_ref):
        act = jax.nn.silu(c_ref[...])
        dm = dm_ref[0]
        acc = act[:, 0:1] * dm[0:1, :]
        for e in range(1, N_DEV):
            acc = acc + act[:, e:e + 1] * dm[e:e + 1, :]
        o_ref[0] = acc

    return pl.pallas_call(
        body, grid=(nl,),
        in_specs=[pl.BlockSpec((d, N_DEV), lambda l: (0, 0)), pl.BlockSpec((1, N_DEV, n), lambda l: (l, 0, 0))],
        out_specs=pl.BlockSpec((1, d, n), lambda l: (l, 0, 0)), out_shape=jax.ShapeDtypeStruct((nl, d, n), F32),
        name="ada_w_grad", compiler_params=_cparams(("parallel",)),
    )(c_all_t, dmod_cols)


ADAM_BLOCK_BYTES = 6 * 1024 * 1024


def _rows2d(a, lead=0):
    return a.reshape(a.shape[:lead] + (-1, a.shape[-1])) if a.ndim > lead + 1 else a.reshape(a.shape[:lead] + (1, -1))


def adamw(slots, w, m, v, *, name):
    shape = w.shape
    s2 = _rows2d(slots, 1)
    w2, m2, v2 = _rows2d(w), _rows2d(m), _rows2d(v)
    ns, r, c = s2.shape
    per_row = c * (ns * s2.dtype.itemsize + 7 * 4)
    tr = r
    if r * per_row > ADAM_BLOCK_BYTES:
        tr = max(t for t in range(8, r, 8) if r % t == 0 and t * per_row <= ADAM_BLOCK_BYTES)

    def body(s_ref, w_ref, m_ref, v_ref, g_ref, d_ref, mo_ref, vo_ref):
        g = s_ref[0].astype(F32)
        for i in range(1, ns):
            g = g + s_ref[i].astype(F32)
        mn = ADAM_B1 * m_ref[...] + (1.0 - ADAM_B1) * g
        vn = ADAM_B2 * v_ref[...] + (1.0 - ADAM_B2) * jnp.square(g)
        m_hat = mn / (1.0 - ADAM_B1 ** ADAM_STEP)
        v_hat = vn / (1.0 - ADAM_B2 ** ADAM_STEP)
        g_ref[...] = g
        d_ref[...] = -ADAM_LR * (m_hat / (jnp.sqrt(v_hat) + ADAM_EPS) + ADAM_WD * w_ref[...])
        mo_ref[...] = mn
        vo_ref[...] = vn

    row = pl.BlockSpec((tr, c), lambda i: (i, 0))
    outs = pl.pallas_call(
        body, grid=(r // tr,), in_specs=[pl.BlockSpec((ns, tr, c), lambda i: (0, i, 0)), row, row, row],
        out_specs=[row] * 4, out_shape=[jax.ShapeDtypeStruct((r, c), F32)] * 4, name=name,
        compiler_params=_cparams(("parallel",)),
    )(s2, w2, m2, v2)
    return [o.reshape(shape) for o in outs]


def _exchange(srcs, *, gather, name):
    n = len(srcs)

    def body(*refs):
        start, wait = _exchange_ops(refs[:n], refs[n:2 * n], *refs[2 * n:], gather)
        start()
        wait()

    return pl.pallas_call(
        body, in_specs=[ANY_SPEC] * n, out_specs=[ANY_SPEC] * n, out_shape=_exchange_out_shapes(srcs, gather),
        scratch_shapes=_exchange_sems(n), name=name, compiler_params=pltpu.CompilerParams(has_side_effects=True),
    )(*srcs)


ANY_SPEC = pl.BlockSpec(memory_space=pl.ANY)


def _exchange_out_shapes(srcs, gather):
    return [jax.ShapeDtypeStruct((N_DEV,) + (tuple(s.shape) if gather else tuple(s.shape[1:])), s.dtype) for s in srcs]


def _exchange_sems(n):
    return [pltpu.SemaphoreType.DMA((N_DEV - 1, n)), pltpu.SemaphoreType.DMA((N_DEV - 1, n)), pltpu.SemaphoreType.DMA((n,))]


def _exchange_ops(src_refs, out_refs, send_sems, recv_sems, local_sems, gather):
    n = len(src_refs)
    mx, my, mc = lax.axis_index("x"), lax.axis_index("y"), lax.axis_index("c")
    me = 4 * mx + 2 * my + mc
    peers = []
    for k in range(1, N_DEV):
        px = 1 - mx if k & 4 else mx
        py = 1 - my if k & 2 else my
        pc = 1 - mc if k & 1 else mc
        peers.append(((px, py, pc), 4 * px + 2 * py + pc))

    def part(a, idx):
        return src_refs[a] if gather else src_refs[a].at[idx]

    def copy(k, a, slot):
        dev, pid = peers[k]
        return pltpu.make_async_remote_copy(src_ref=part(a, pid), dst_ref=out_refs[a].at[slot], send_sem=send_sems.at[k, a],
                                            recv_sem=recv_sems.at[k, a], device_id=dev, device_id_type=MESH)

    def local(a):
        return pltpu.make_async_copy(part(a, me), out_refs[a].at[me], local_sems.at[a])

    def start():
        for a in range(n):
            local(a).start()
            for k in range(N_DEV - 1):
                copy(k, a, me).start()

    def wait():
        for a in range(n):
            for k in range(N_DEV - 1):
                copy(k, a, me).wait_send()
                copy(k, a, peers[k][1]).wait_recv()
            local(a).wait()

    return start, wait


def _with_exchange(comm):
    srcs, gather = comm if comm is not None else ((), True)
    n = len(srcs)
    return (list(srcs), gather, [ANY_SPEC] * n, _exchange_out_shapes(srcs, gather) if n else [], _exchange_sems(n) if n else [])


def _to_blocks(full, ax):
    shp = full.shape
    g = full.reshape(shp[:ax] + (N_DEV, shp[ax] // N_DEV) + shp[ax + 1:])
    return jnp.moveaxis(g, ax, 0)


def _from_blocks(seg, ax):
    g = jnp.moveaxis(seg, 0, ax)
    shp = g.shape
    return g.reshape(shp[:ax] + (shp[ax] * shp[ax + 1],) + shp[ax + 2:])


def _to_heads(a, dh):
    return a.reshape(a.shape[0], -1, dh).transpose(1, 0, 2)


def _from_heads(a):
    return a.transpose(1, 0, 2).reshape(a.shape[1], -1)


def _from_heads_t(a):
    return a.transpose(2, 0, 1).reshape(a.shape[2], -1)


MLA_SCALE = (MLA_NOPE + MLA_ROPE) ** -0.5
SB_SCALE = SB_DIM ** -0.5
assert math.frexp(SB_SCALE)[0] == 0.5


def _resid_epi(acc, x, g):
    return x + g * acc, acc


def _layer_fwd(x, mod, p, cos, sin, comm=None):
    t = x.shape[0]
    bk = _att_blocks(t)[1]
    sh1, sc1, g1, sh2, sc2, g2 = [mod[i:i + 1] for i in range(6)]
    h = normmod_fwd(x, p["norm1_g"], sc1, sh1)
    z = matmul(h, p["w_in"], name="mm_in")
    y_sg = sg_fwd(z, p["sg_norm_g"], p["sg_w"], p["sg_bt"])
    y_cv = conv_fwd(z, p["conv_w"])
    mq, mk, mv = mla_prep_fwd(z, cos, sin, p["gq"], p["gkv"], p["wuq"], p["wukv"])
    mot, lse = softmax_att_fwd(mq.transpose(0, 2, 1), mk, _key_blocks(mv, bk), MLA_SCALE)
    y_mla = _from_heads_t(mot).astype(BF16)
    zsb = sb_cast(z)
    sq, sk, sv = [_to_heads(zsb[:, i * WIDTH:(i + 1) * WIDTH], SB_DIM) for i in range(3)]
    sq = sq * SB_SCALE
    sot, rs, got = stick_att_fwd(sq.transpose(0, 2, 1), sk, _key_blocks(sv, bk), comm)
    y_sb = _from_heads_t(sot).astype(BF16)
    ys = (y_sg, y_cv, y_mla, y_sb)
    up, merged = merge_fwd(ys, p["w_branch"], z)
    x1, out = matmul(merged, p["w_out"], name="mm_out", out_dtypes=(F32, F32), epi=_resid_epi, epi_in=((x, "mn"), (g1, "n")))
    h2 = normmod_fwd(x1, p["norm2_g"], sc2, sh2)
    a, r = matmul(h2, p["w1"], name="mm_up", out_dtypes=(F32, BF16),
                  epi=lambda acc: (acc, jnp.square(jnp.maximum(acc, 0.0))))
    x2, mo = matmul(r, p["w2"], name="mm_down", out_dtypes=(F32, F32), epi=_resid_epi, epi_in=((x1, "mn"), (g2, "n")))
    saved = dict(x=x, h=h, z=z, ys=ys, mq=mq, mk=mk, mv=mv, mot=mot, lse=lse, sq=sq, sk=sk, sv=sv, rs=rs, up=up,
                 merged=merged, out=out, x1=x1, h2=h2, a=a, r=r, mo=mo)
    return x2, saved, got


def _layer_bwd(dx2, s, mod, p, cos, sin, comm=None):
    t = dx2.shape[0]
    bk = _att_blocks(t)[1]
    sh1, sc1, g1, sh2, sc2, g2 = [mod[i:i + 1] for i in range(6)]
    g = {}
    dm, dg2 = resid_bwd(dx2, s["mo"], g2)
    da = matmul(dm, p["w2_t"], name="mm_down_dx", out_dtypes=(BF16,), epi=lambda acc, a: (acc * (2.0 * jnp.maximum(a, 0.0)),),
                epi_in=((s["a"], "mn"),))
    g["mlp_w2"] = matmul_tn(s["r"], dm, name="mm_down_dw")
    dh2 = matmul(da, p["w1_t"], name="mm_up_dx")
    g["mlp_w1"] = matmul_tn(s["h2"], da, name="mm_up_dw")
    dx1, g["norm2_g"], dsc2, dsh2 = normmod_bwd(s["x1"], p["norm2_g"], sc2, sh2, dh2, dx2)
    dout, dg1 = resid_bwd(dx1, s["out"], g1)
    dmerged = matmul(dout, p["w_out_t"], name="mm_out_dx")
    g["w_out"] = matmul_tn(s["merged"], dout, name="mm_out_dw")
    dzg, dup = merge_bwd(dmerged, s["up"], s["z"])
    dys = [matmul(dup[n], p["w_branch_t"][n], name="mm_branch_dx") for n in range(N_BRANCH)]
    g["w_branch"] = jnp.stack([matmul_tn(s["ys"][n], dup[n], name="mm_branch_dw") for n in range(N_BRANCH)])
    z = s["z"]
    dz_sg, g["sg_norm_g"], g["sg_w"], dbt = sg_bwd(z, p["sg_norm_g"], p["sg_w"], p["sg_bt"], dys[0])
    g["sg_b"] = dbt.T
    dz_cv, g["conv_w"] = conv_bwd(z, p["conv_w"], dys[1])
    do = _to_heads(dys[2], MLA_V).astype(BF16)
    dqt, dk, dv = softmax_att_bwd(s["mq"].transpose(0, 2, 1), s["mq"], s["mk"], _key_blocks(s["mk"], bk), s["mv"], s["mot"],
                                  s["lse"], do, do.transpose(0, 2, 1), MLA_SCALE)
    dz_mla, g["mla_q_norm_g"], g["mla_kv_norm_g"], g["mla_w_uq"], g["mla_w_ukv"] = mla_prep_bwd(
        z, cos, sin, p["gq"], p["gkv"], p["wuq"], p["wukv"], dqt.transpose(0, 2, 1), dk, dv)
    do = _to_heads(dys[3], SB_DIM).astype(BF16)
    dqt, dk, dv, got = stick_att_bwd(s["sq"].transpose(0, 2, 1), s["sq"], s["sk"], _key_blocks(s["sk"], bk), s["sv"], s["rs"],
                                     do, do.transpose(0, 2, 1), SB_SCALE, comm)
    dz_sb = jnp.concatenate([_from_heads_t(dqt), _from_heads(dk), _from_heads(dv)], axis=1).astype(BF16)
    dz = jnp.concatenate([dz_sg, dz_cv, dz_mla, dz_sb, dzg], axis=1)
    dh = matmul(dz, p["w_in_t"], name="mm_in_dx", tk=2176)
    dw_in = matmul_tn(s["h"], dz, name="mm_in_dw")
    g["w_in"] = jnp.concatenate([dw_in[:, :Z_MLA + MLA_COLS], dw_in[:, Z_SB:]], axis=1)
    dx, g["norm1_g"], dsc1, dsh1 = normmod_bwd(s["x"], p["norm1_g"], sc1, sh1, dh, dx1)
    g["mla_q_norm_g"], g["mla_kv_norm_g"] = g["mla_q_norm_g"][0], g["mla_kv_norm_g"][0]
    g["norm1_g"], g["norm2_g"], g["sg_norm_g"] = g["norm1_g"][0], g["norm2_g"][0], g["sg_norm_g"][0]
    g["ada_b"] = jnp.concatenate([dsh1, dsc1, dg1, dsh2, dsc2, dg2], axis=1)[0]
    return dx, g, got


WEIGHT_NAMES = ("ada_w", "ada_b", "norm1_g", "norm2_g", "w_in", "sg_norm_g", "sg_w", "sg_b", "conv_w", "mla_q_norm_g",
                "mla_w_uq", "mla_kv_norm_g", "mla_w_ukv", "w_branch", "w_out", "mlp_w1", "mlp_w2", "final_norm_g")
SHARDED = (("w_in", 2), ("mla_w_uq", 2), ("mla_w_ukv", 2), ("w_branch", 3), ("w_out", 1), ("mlp_w1", 2), ("mlp_w2", 1),
           ("conv_w", 2))
REPLICATED = ("ada_b", "norm1_g", "norm2_g", "sg_norm_g", "sg_w", "sg_b", "mla_q_norm_g", "mla_kv_norm_g", "final_norm_g")


def kernel(x, c, positions, ada_w, ada_b, norm1_g, norm2_g, w_in, sg_norm_g, sg_w, sg_b, conv_w, mla_q_norm_g, mla_w_uq, mla_kv_norm_g, mla_w_ukv, w_branch, w_out, mlp_w1, mlp_w2, final_norm_g, loss_target, m_ada_w, m_ada_b, m_norm1_g, m_norm2_g, m_w_in, m_sg_norm_g, m_sg_w, m_sg_b, m_conv_w, m_mla_q_norm_g, m_mla_w_uq, m_mla_kv_norm_g, m_mla_w_ukv, m_w_branch, m_w_out, m_mlp_w1, m_mlp_w2, m_final_norm_g, v_ada_w, v_ada_b, v_norm1_g, v_norm2_g, v_w_in, v_sg_norm_g, v_sg_w, v_sg_b, v_conv_w, v_mla_q_norm_g, v_mla_w_uq, v_mla_kv_norm_g, v_mla_w_ukv, v_w_branch, v_w_out, v_mlp_w1, v_mlp_w2, v_final_norm_g):
    given = dict(locals())
    w = {n: given[n] for n in WEIGHT_NAMES}
    m = {n: given["m_" + n] for n in WEIGHT_NAMES}
    v = {n: given["v_" + n] for n in WEIGHT_NAMES}
    xs, tgt = x[0], loss_target[0]
    nl = ada_w.shape[0]
    me = 4 * lax.axis_index("x") + 2 * lax.axis_index("y") + lax.axis_index("c")

    inv_freq = ROPE_THETA ** (-jnp.arange(0, MLA_ROPE, 2, dtype=F32) / MLA_ROPE)
    ang = positions[0].astype(F32)[:, None] * inv_freq
    cos, sin = jnp.cos(ang), jnp.sin(ang)

    c_all, conv_blocks = _exchange([c, conv_w], gather=True, name="gather_c_conv")
    c_all = c_all[:, 0]
    conv_full = _from_blocks(conv_blocks, 2)
    ncol = ada_w.shape[2]
    ada_b_cols = lax.dynamic_slice_in_dim(ada_b, me * ncol, ncol, axis=1)[:, None, :]
    mod_cols = mod_fwd(c_all, ada_w, ada_b_cols)
    got = _exchange([mod_cols], gather=True, name="gather_mod")[0]
    mod = jnp.moveaxis(lax.dynamic_index_in_dim(got, me, axis=2, keepdims=False), 0, 1).reshape(nl, 6, D_MODEL)

    names = [n for n, _ in SHARDED if n != "conv_w"]
    cut = Z_MLA + MLA_COLS

    def weight_shards(lo, hi):
        return [w[n][lo:hi].astype(BF16) for n in names]

    def layer_weights(got, lo, hi):
        full = {n: _from_blocks(seg, ax) for (n, ax), seg in zip(SHARDED, got)}
        w_in_pad = jnp.concatenate([full["w_in"][:, :, :cut], jnp.zeros((hi - lo, D_MODEL, Z_SB - cut), BF16),
                                    full["w_in"][:, :, cut:]], axis=2)
        return [dict(
            norm1_g=norm1_g[l][None], norm2_g=norm2_g[l][None], sg_norm_g=sg_norm_g[l][None], sg_w=sg_w[l], sg_bt=sg_b[l].T,
            conv_w=conv_full[l], gq=mla_q_norm_g[l][None], gkv=mla_kv_norm_g[l][None], wuq=full["mla_w_uq"][i],
            wukv=full["mla_w_ukv"][i], w_in=w_in_pad[i], w_in_t=w_in_pad[i].T, w_branch=full["w_branch"][i],
            w_branch_t=full["w_branch"][i].transpose(0, 2, 1), w_out=full["w_out"][i], w_out_t=full["w_out"][i].T,
            w1=full["mlp_w1"][i], w1_t=full["mlp_w1"][i].T, w2=full["mlp_w2"][i], w2_t=full["mlp_w2"][i].T)
            for i, l in enumerate(range(lo, hi))]

    layers = layer_weights(_exchange(weight_shards(0, 1), gather=True, name="gather_weights_first"), 0, 1)

    saved = []
    xc = xs
    for l in range(nl):
        comm = (weight_shards(1, nl), True) if l == 0 and nl > 1 else None
        xc, s, got = _layer_fwd(xc, mod[l], layers[l], cos, sin, comm)
        saved.append(s)
        if comm is not None:
            layers += layer_weights(got, 1, nl)
    loss_part, dx, dgf = loss_head(xc, final_norm_g[None], tgt)
    loss = lax.psum(loss_part[0, 0], AXES)

    def grad_blocks(g):
        return [_to_blocks(g[n][None], ax).astype(BF16) for n, ax in SHARDED]

    grads, landed = [None] * nl, [None] * nl
    for l in reversed(range(nl)):
        comm = (grad_blocks(grads[l + 1]), False) if l + 1 < nl else None
        dx, grads[l], got = _layer_bwd(dx, saved[l], mod[l], layers[l], cos, sin, comm)
        if comm is not None:
            landed[l + 1] = got
    landed[0] = _exchange(grad_blocks(grads[0]), gather=False, name="exchange_grads_first")
    gfull = {n: jnp.stack([grads[l][n] for l in range(nl)]) for n in REPLICATED if n != "final_norm_g"}
    gfull["final_norm_g"] = dgf[0]

    out = {}

    def update(n, slots):
        for kind, arr in zip(("grad", "delta", "new_m", "new_v"), adamw(slots, w[n], m[n], v[n], name="adamw_" + n)):
            out[kind, n] = arr

    for i, (n, _) in enumerate(SHARDED):
        update(n, jnp.concatenate([landed[l][i] for l in range(nl)], axis=1))
    got = _exchange([gfull[n] for n in REPLICATED], gather=True, name="gather_small_grads")
    for n, slots in zip(REPLICATED, got):
        update(n, slots)
    dmod_cols = jnp.moveaxis(lax.dynamic_slice_in_dim(got[0], me * ncol, ncol, axis=2), 0, 1)
    update("ada_w", ada_w_grad(c_all.T, dmod_cols)[None])

    res = [loss, dx[None]]
    for kind in ("grad", "delta", "new_m", "new_v"):
        res += [out[kind, n] for n in WEIGHT_NAMES]
    return tuple(res)
```

```python
import functools
import math

import jax
import jax.numpy as jnp
from jax import lax
from jax.experimental import pallas as pl
from jax.experimental.pallas import tpu as pltpu

F32 = jnp.float32
BF16 = jnp.bfloat16

D_MODEL = 1024
WIDTH = 512
CHUNK = 128
SG_GROUPS = 4
HEADS = 8
MLA_NOPE = 64
MLA_ROPE = 32
MLA_V = 64
MLA_Q_RANK = 256
MLA_KV_RANK = 128
MLA_QK_PAD = 128
SB_DIM = 64
ROPE_THETA = 10000.0
NORM_EPS = 1e-6
N_BRANCH = 4
D_FF = 4096
Z_SG, Z_CONV, Z_MLA, Z_SB, Z_GATE, Z_COLS = 0, 1024, 2560, 3072, 4608, 8704
IN_COLS = 8608
MLA_COLS = MLA_Q_RANK + MLA_KV_RANK + MLA_ROPE

ADAM_LR, ADAM_B1, ADAM_B2, ADAM_EPS, ADAM_WD, ADAM_STEP = 0.001, 0.9, 0.999, 1e-08, 0.01, 10

N_DEV = 8
AXES = ("x", "y", "c")
MESH = pl.DeviceIdType.MESH
VMEM_LIMIT_V7X = 56 * 1024 * 1024
ATT_BQ = 2048
ATT_BK = 256
NEG_BIG = -1e30


def _cparams(sem=None):
    return pltpu.CompilerParams(dimension_semantics=sem, vmem_limit_bytes=VMEM_LIMIT_V7X)


def _div_tile(n, pref):
    if n <= pref:
        return n
    t = (pref // 128) * 128
    while t >= 128:
        if n % t == 0:
            return t
        t -= 128
    raise ValueError(f"no tile for {n}")


def _row_tile(t):
    return min(512, t)


def _dot(a, b, dims):
    return lax.dot_general(a.astype(BF16), b.astype(BF16), (dims, ((), ())), preferred_element_type=F32)


@jax.custom_vjp
def _mm(a, b):
    return _dot(a, b, ((1,), (0,)))


def _mm_fwd(a, b):
    return _mm(a, b), (a, b)


def _mm_bwd(res, g):
    a, b = res
    return _dot(g, b, ((1,), (1,))).astype(a.dtype), _dot(a, g, ((0,), (0,))).astype(b.dtype)


_mm.defvjp(_mm_fwd, _mm_bwd)


def _rms(x, g):
    return (x * lax.rsqrt(jnp.mean(x * x, axis=-1, keepdims=True) + NORM_EPS)) * g


def _normmod(x, g, sc, sh):
    return _rms(x, g) * (1.0 + sc) + sh


def normmod_fwd(x, g, sc, sh):
    t, d = x.shape
    tt = _row_tile(t)

    def body(x_ref, g_ref, sc_ref, sh_ref, h_ref):
        h_ref[...] = _normmod(x_ref[...], g_ref[...], sc_ref[...], sh_ref[...]).astype(BF16)

    row = pl.BlockSpec((tt, d), lambda i: (i, 0))
    vec = pl.BlockSpec((1, d), lambda i: (0, 0))
    return pl.pallas_call(
        body, grid=(t // tt,), in_specs=[row, vec, vec, vec], out_specs=row,
        out_shape=jax.ShapeDtypeStruct((t, d), BF16), name="normmod_fwd", compiler_params=_cparams(("parallel",)),
    )(x, g, sc, sh)


def normmod_bwd(x, g, sc, sh, dh, dres):
    t, d = x.shape
    tt = _row_tile(t)

    def body(x_ref, g_ref, sc_ref, sh_ref, dh_ref, dres_ref, dx_ref, dg_ref, dsc_ref, dsh_ref):
        _, vjp = jax.vjp(_normmod, x_ref[...], g_ref[...], sc_ref[...], sh_ref[...])
        dx, dg, dsc, dsh = vjp(dh_ref[...])
        dx_ref[...] = dx + dres_ref[...]

        @pl.when(pl.program_id(0) == 0)
        def _():
            dg_ref[...] = jnp.zeros_like(dg_ref)
            dsc_ref[...] = jnp.zeros_like(dsc_ref)
            dsh_ref[...] = jnp.zeros_like(dsh_ref)

        dg_ref[...] += dg
        dsc_ref[...] += dsc
        dsh_ref[...] += dsh

    row = pl.BlockSpec((tt, d), lambda i: (i, 0))
    vec = pl.BlockSpec((1, d), lambda i: (0, 0))
    vs = jax.ShapeDtypeStruct((1, d), F32)
    return pl.pallas_call(
        body, grid=(t // tt,), in_specs=[row, vec, vec, vec, row, row], out_specs=[row, vec, vec, vec],
        out_shape=[jax.ShapeDtypeStruct((t, d), F32), vs, vs, vs], name="normmod_bwd",
        compiler_params=_cparams(("arbitrary",)),
    )(x, g, sc, sh, dh, dres)


def resid_bwd(dxn, m, g):
    t, d = dxn.shape
    tt = _row_tile(t)

    def body(dx_ref, m_ref, g_ref, dm_ref, dg_ref):
        dx = dx_ref[...]
        dm_ref[...] = (dx * g_ref[...]).astype(BF16)

        @pl.when(pl.program_id(0) == 0)
        def _():
            dg_ref[...] = jnp.zeros_like(dg_ref)

        dg_ref[...] += jnp.sum(dx * m_ref[...], axis=0, keepdims=True)

    row = pl.BlockSpec((tt, d), lambda i: (i, 0))
    vec = pl.BlockSpec((1, d), lambda i: (0, 0))
    return pl.pallas_call(
        body, grid=(t // tt,), in_specs=[row, row, vec], out_specs=[row, vec],
        out_shape=[jax.ShapeDtypeStruct((t, d), BF16), jax.ShapeDtypeStruct((1, d), F32)], name="resid_bwd",
        compiler_params=_cparams(("arbitrary",)),
    )(dxn, m, g)


def loss_head(x, gf, tgt):
    t, d = x.shape
    tt = _row_tile(t)

    def f(xv, gv, tv):
        y = _rms(xv, gv)
        return 0.5 * jnp.sum(jnp.mean(jnp.square(y - tv), axis=-1))

    def body(x_ref, g_ref, t_ref, loss_ref, dx_ref, dg_ref):
        val, vjp = jax.vjp(lambda xv, gv: f(xv, gv, t_ref[...]), x_ref[...], g_ref[...])
        dx, dg = vjp(jnp.ones((), F32))
        dx_ref[...] = dx

        @pl.when(pl.program_id(0) == 0)
        def _():
            dg_ref[...] = jnp.zeros_like(dg_ref)
            loss_ref[...] = jnp.zeros_like(loss_ref)

        dg_ref[...] += dg
        loss_ref[...] += jnp.full(loss_ref.shape, val, F32)

    row = pl.BlockSpec((tt, d), lambda i: (i, 0))
    vec = pl.BlockSpec((1, d), lambda i: (0, 0))
    lsp = pl.BlockSpec((1, 128), lambda i: (0, 0))
    return pl.pallas_call(
        body, grid=(t // tt,), in_specs=[row, vec, row], out_specs=[lsp, row, vec],
        out_shape=[jax.ShapeDtypeStruct((1, 128), F32), jax.ShapeDtypeStruct((t, d), F32),
                   jax.ShapeDtypeStruct((1, d), F32)],
        name="loss_head", compiler_params=_cparams(("arbitrary",)),
    )(x, gf, tgt)


def matmul(a, b, *, name, out_dtypes=(F32,), epi=None, epi_in=(), tm=2048, tn=512, tk=1024):
    m, k = a.shape
    n = b.shape[1]
    tm, tn, tk = min(tm, m), _div_tile(n, tn), _div_tile(k, tk)
    nk = k // tk
    n_epi, n_out = len(epi_in), len(out_dtypes)

    def body(*refs):
        a_ref, b_ref = refs[:2]
        e_refs = refs[2:2 + n_epi]
        o_refs = refs[2 + n_epi:2 + n_epi + n_out]
        kk = pl.program_id(2)
        part = jnp.dot(a_ref[...].astype(BF16), b_ref[...].astype(BF16), preferred_element_type=F32)

        def finish(acc):
            outs = (acc,) if epi is None else epi(acc, *[r[...] for r in e_refs])
            for o_ref, o in zip(o_refs, outs):
                o_ref[...] = o.astype(o_ref.dtype)

        if nk == 1:
            finish(part)
            return
        acc_ref = refs[-1]

        @pl.when(kk == 0)
        def _():
            acc_ref[...] = part

        @pl.when((kk > 0) & (kk < nk - 1))
        def _():
            acc_ref[...] += part

        @pl.when(kk == nk - 1)
        def _():
            finish(acc_ref[...] + part)

    in_specs = [pl.BlockSpec((tm, tk), lambda i, j, kk: (i, kk)), pl.BlockSpec((tk, tn), lambda i, j, kk: (kk, j))]
    for _, kind in epi_in:
        in_specs.append(pl.BlockSpec((tm, tn), lambda i, j, kk: (i, j)) if kind == "mn"
                        else pl.BlockSpec((1, tn), lambda i, j, kk: (0, j)))
    out_spec = pl.BlockSpec((tm, tn), lambda i, j, kk: (i, j))
    outs = pl.pallas_call(
        body, grid=(m // tm, n // tn, nk), in_specs=in_specs, out_specs=[out_spec] * n_out,
        out_shape=[jax.ShapeDtypeStruct((m, n), dt) for dt in out_dtypes],
        scratch_shapes=[pltpu.VMEM((tm, tn), F32)] if nk > 1 else [], name=name,
        compiler_params=_cparams(("parallel", "parallel", "arbitrary")),
    )(a, b, *[arr for arr, _ in epi_in])
    return outs[0] if n_out == 1 else outs


def matmul_tn(a, b, *, name, tko=1024, tn=512, tt=2048):
    t, ko = a.shape
    n = b.shape[1]
    tko, tn, tt = _div_tile(ko, tko), _div_tile(n, tn), min(tt, t)
    nt = t // tt

    def body(a_ref, b_ref, o_ref):
        s = pl.program_id(2)
        part = _dot(a_ref[...], b_ref[...], ((0,), (0,)))

        @pl.when(s == 0)
        def _():
            o_ref[...] = part

        @pl.when(s > 0)
        def _():
            o_ref[...] += part

    return pl.pallas_call(
        body, grid=(ko // tko, n // tn, nt),
        in_specs=[pl.BlockSpec((tt, tko), lambda i, j, s: (s, i)), pl.BlockSpec((tt, tn), lambda i, j, s: (s, j))],
        out_specs=pl.BlockSpec((tko, tn), lambda i, j, s: (i, j)), out_shape=jax.ShapeDtypeStruct((ko, n), F32),
        name=name, compiler_params=_cparams(("parallel", "parallel", "arbitrary")),
    )(a, b)


def merge_fwd(ys, w_branch, z):
    t = ys[0].shape[0]
    d = D_MODEL
    tm, tn = min(512, t), 512
    gate0 = Z_GATE // tn

    def body(y0, y1, y2, y3, w_ref, g0, g1, g2, g3, up_ref, mg_ref):
        acc = jnp.zeros((tm, tn), F32)
        for n, (y_ref, g_ref) in enumerate(zip((y0, y1, y2, y3), (g0, g1, g2, g3))):
            up = jnp.dot(y_ref[...], w_ref[n], preferred_element_type=F32)
            up_ref[n] = up.astype(BF16)
            acc = acc + jax.nn.sigmoid(g_ref[...]) * up
        mg_ref[...] = acc.astype(BF16)

    ysp = pl.BlockSpec((tm, WIDTH), lambda i, j: (i, 0))
    gate_specs = [pl.BlockSpec((tm, tn), functools.partial(lambda i, j, n: (i, gate0 + n * (d // tn) + j), n=n))
                  for n in range(N_BRANCH)]
    return pl.pallas_call(
        body, grid=(t // tm, d // tn),
        in_specs=[ysp, ysp, ysp, ysp, pl.BlockSpec((N_BRANCH, WIDTH, tn), lambda i, j: (0, 0, j))] + gate_specs,
        out_specs=[pl.BlockSpec((N_BRANCH, tm, tn), lambda i, j: (0, i, j)), pl.BlockSpec((tm, tn), lambda i, j: (i, j))],
        out_shape=[jax.ShapeDtypeStruct((N_BRANCH, t, d), BF16), jax.ShapeDtypeStruct((t, d), BF16)],
        name="merge_fwd", compiler_params=_cparams(("parallel", "parallel")),
    )(*ys, w_branch, z, z, z, z)


def merge_bwd(dmerged, up, z):
    t, d = dmerged.shape
    tm, tn = min(512, t), 512
    gate0 = Z_GATE // tn
    nj = d // tn

    def body(dm_ref, up_ref, g_ref, dzg_ref, dup_ref):
        dm = dm_ref[...]
        s = jax.nn.sigmoid(g_ref[...])
        dzg_ref[...] = (dm * up_ref[0] * s * (1.0 - s)).astype(BF16)
        dup_ref[0] = (dm * s).astype(BF16)

    blk = pl.BlockSpec((1, tm, tn), lambda i, j, n: (n, i, j))
    return pl.pallas_call(
        body, grid=(t // tm, nj, N_BRANCH),
        in_specs=[pl.BlockSpec((tm, tn), lambda i, j, n: (i, j)), blk,
                  pl.BlockSpec((tm, tn), lambda i, j, n: (i, gate0 + n * nj + j))],
        out_specs=[pl.BlockSpec((tm, tn), lambda i, j, n: (i, n * nj + j)), blk],
        out_shape=[jax.ShapeDtypeStruct((t, N_BRANCH * d), BF16), jax.ShapeDtypeStruct((N_BRANCH, t, d), BF16)],
        name="merge_bwd", compiler_params=_cparams(("parallel", "parallel", "arbitrary")),
    )(dmerged, up, z)


def _sg_tile(zsg, g, w, bt):
    tt = zsg.shape[0]
    a = jax.nn.gelu(zsg)
    u, v = a[:, :WIDTH], a[:, WIDTH:]
    vc = v - jnp.mean(v, axis=-1, keepdims=True)
    v = (vc * lax.rsqrt(jnp.mean(vc * vc, axis=-1, keepdims=True) + NORM_EPS)) * g
    row = lax.broadcasted_iota(jnp.int32, (CHUNK, CHUNK), 0)
    col = lax.broadcasted_iota(jnp.int32, (CHUNK, CHUNK), 1)
    cols = []
    for gi in range(SG_GROUPS):
        wc = jnp.where(row >= col, w[gi], 0.0)
        bias = bt[:, gi:gi + 1]
        rows = [_mm(wc, v[ch * CHUNK:(ch + 1) * CHUNK, gi * CHUNK:(gi + 1) * CHUNK]) + bias for ch in range(tt // CHUNK)]
        cols.append(jnp.concatenate(rows, axis=0) if len(rows) > 1 else rows[0])
    return u * jnp.concatenate(cols, axis=1)


def sg_fwd(z, g, w, bt):
    t = z.shape[0]
    tt = _row_tile(t)

    def body(z_ref, g_ref, w_ref, b_ref, y_ref):
        y_ref[...] = _sg_tile(z_ref[...], g_ref[...], w_ref[...], b_ref[...]).astype(BF16)

    return pl.pallas_call(
        body, grid=(t // tt,),
        in_specs=[pl.BlockSpec((tt, 2 * WIDTH), lambda i: (i, 0)), pl.BlockSpec((1, WIDTH), lambda i: (0, 0)),
                  pl.BlockSpec((SG_GROUPS, CHUNK, CHUNK), lambda i: (0, 0, 0)), pl.BlockSpec((CHUNK, SG_GROUPS), lambda i: (0, 0))],
        out_specs=pl.BlockSpec((tt, WIDTH), lambda i: (i, 0)), out_shape=jax.ShapeDtypeStruct((t, WIDTH), BF16),
        name="sg_fwd", compiler_params=_cparams(("parallel",)),
    )(z, g, w, bt)


def sg_bwd(z, g, w, bt, dy):
    t = z.shape[0]
    tt = _row_tile(t)

    def body(z_ref, g_ref, w_ref, b_ref, dy_ref, dz_ref, dg_ref, dw_ref, db_ref):
        _, vjp = jax.vjp(_sg_tile, z_ref[...], g_ref[...], w_ref[...], b_ref[...])
        dz, dg, dw, db = vjp(dy_ref[...])
        dz_ref[...] = dz.astype(BF16)

        @pl.when(pl.program_id(0) == 0)
        def _():
            dg_ref[...] = jnp.zeros_like(dg_ref)
            dw_ref[...] = jnp.zeros_like(dw_ref)
            db_ref[...] = jnp.zeros_like(db_ref)

        dg_ref[...] += dg
        dw_ref[...] += dw
        db_ref[...] += db

    gs = pl.BlockSpec((1, WIDTH), lambda i: (0, 0))
    ws = pl.BlockSpec((SG_GROUPS, CHUNK, CHUNK), lambda i: (0, 0, 0))
    bs = pl.BlockSpec((CHUNK, SG_GROUPS), lambda i: (0, 0))
    return pl.pallas_call(
        body, grid=(t // tt,),
        in_specs=[pl.BlockSpec((tt, 2 * WIDTH), lambda i: (i, 0)), gs, ws, bs, pl.BlockSpec((tt, WIDTH), lambda i: (i, 0))],
        out_specs=[pl.BlockSpec((tt, 2 * WIDTH), lambda i: (i, 0)), gs, ws, bs],
        out_shape=[jax.ShapeDtypeStruct((t, 2 * WIDTH), BF16), jax.ShapeDtypeStruct((1, WIDTH), F32),
                   jax.ShapeDtypeStruct((SG_GROUPS, CHUNK, CHUNK), F32), jax.ShapeDtypeStruct((CHUNK, SG_GROUPS), F32)],
        name="sg_bwd", compiler_params=_cparams(("arbitrary",)),
    )(z, g, w, bt, dy)


HALO = 8


def _shift_down(halo, cur, k):
    tt = cur.shape[0]
    ext = jnp.concatenate([halo, cur], axis=0)
    return ext[HALO - k:HALO - k + tt]


def _shift_up(cur, halo, k):
    tt = cur.shape[0]
    ext = jnp.concatenate([cur, halo], axis=0)
    return ext[k:k + tt]


def conv_fwd(z, cw):
    t = z.shape[0]
    tt = _row_tile(t)
    cb = Z_CONV // WIDTH
    hb = tt // HALO

    def body(gb_ref, gc_ref, xv_ref, hgc_ref, hxv_ref, w_ref, y_ref):
        first = (pl.program_id(0) == 0)
        tcur = gc_ref[...] * xv_ref[...]
        thalo = jnp.where(first, 0.0, hgc_ref[...] * hxv_ref[...])
        w = w_ref[...]
        y = w[2:3] * tcur + w[1:2] * _shift_down(thalo, tcur, 1) + w[0:1] * _shift_down(thalo, tcur, 2)
        y_ref[...] = (gb_ref[...] * y).astype(BF16)

    def cur(off):
        return pl.BlockSpec((tt, WIDTH), lambda i: (i, cb + off))

    def prev(off):
        return pl.BlockSpec((HALO, WIDTH), lambda i: (jnp.maximum(i * hb - 1, 0), cb + off))

    return pl.pallas_call(
        body, grid=(t // tt,),
        in_specs=[cur(0), cur(1), cur(2), prev(1), prev(2), pl.BlockSpec((3, WIDTH), lambda i: (0, 0))],
        out_specs=pl.BlockSpec((tt, WIDTH), lambda i: (i, 0)), out_shape=jax.ShapeDtypeStruct((t, WIDTH), BF16),
        name="conv_fwd", compiler_params=_cparams(("parallel",)),
    )(z, z, z, z, z, cw)


def conv_bwd(z, cw, dy):
    t = z.shape[0]
    tt = _row_tile(t)
    nt = t // tt
    cb = Z_CONV // WIDTH
    hb = tt // HALO

    def body(gb_ref, gc_ref, xv_ref, hgc_ref, hxv_ref, ngb_ref, dy_ref, ndy_ref, w_ref, dz_ref, dw_ref):
        i = pl.program_id(0)
        gb, gc, xv = gb_ref[...], gc_ref[...], xv_ref[...]
        tcur = gc * xv
        thalo = jnp.where(i == 0, 0.0, hgc_ref[...] * hxv_ref[...])
        t1, t2 = _shift_down(thalo, tcur, 1), _shift_down(thalo, tcur, 2)
        w = w_ref[...]
        y = w[2:3] * tcur + w[1:2] * t1 + w[0:1] * t2
        dy = dy_ref[...]
        dyc = dy * gb
        dyn = jnp.where(i == nt - 1, 0.0, ndy_ref[...] * ngb_ref[...])
        dt = w[2:3] * dyc + w[1:2] * _shift_up(dyc, dyn, 1) + w[0:1] * _shift_up(dyc, dyn, 2)
        dz_ref[...] = jnp.concatenate([dy * y, dt * xv, dt * gc], axis=1).astype(BF16)

        @pl.when(i == 0)
        def _():
            dw_ref[...] = jnp.zeros_like(dw_ref)

        dw_ref[...] += jnp.concatenate([jnp.sum(dyc * t2, axis=0, keepdims=True), jnp.sum(dyc * t1, axis=0, keepdims=True),
                                        jnp.sum(dyc * tcur, axis=0, keepdims=True)], axis=0)

    def cur(off):
        return pl.BlockSpec((tt, WIDTH), lambda i: (i, cb + off))

    def prev(off):
        return pl.BlockSpec((HALO, WIDTH), lambda i: (jnp.maximum(i * hb - 1, 0), cb + off))

    last = t // HALO - 1
    nxt_z = pl.BlockSpec((HALO, WIDTH), lambda i: (jnp.minimum((i + 1) * hb, last), cb))
    nxt_dy = pl.BlockSpec((HALO, WIDTH), lambda i: (jnp.minimum((i + 1) * hb, last), 0))
    return pl.pallas_call(
        body, grid=(nt,),
        in_specs=[cur(0), cur(1), cur(2), prev(1), prev(2), nxt_z, pl.BlockSpec((tt, WIDTH), lambda i: (i, 0)), nxt_dy,
                  pl.BlockSpec((3, WIDTH), lambda i: (0, 0))],
        out_specs=[pl.BlockSpec((tt, 3 * WIDTH), lambda i: (i, 0)), pl.BlockSpec((3, WIDTH), lambda i: (0, 0))],
        out_shape=[jax.ShapeDtypeStruct((t, 3 * WIDTH), BF16), jax.ShapeDtypeStruct((3, WIDTH), F32)],
        name="conv_bwd", compiler_params=_cparams(("arbitrary",)),
    )(z, z, z, z, z, z, dy, dy, cw)


def _mla_prep(zm, cos, sin, gq, gkv, wuq, wukv):
    tt = zm.shape[0]
    cq, ckv, kpe = zm[:, :MLA_Q_RANK], zm[:, MLA_Q_RANK:MLA_Q_RANK + MLA_KV_RANK], zm[:, MLA_Q_RANK + MLA_KV_RANK:MLA_COLS]
    q = _mm(_rms(cq, gq), wuq)
    kv = _mm(_rms(ckv, gkv), wukv)
    half = MLA_ROPE // 2

    def rope(xr):
        x1, x2 = xr[:, :half], xr[:, half:]
        return jnp.concatenate([x1 * cos - x2 * sin, x2 * cos + x1 * sin], axis=-1)

    kpe = rope(kpe)
    pad = jnp.zeros((tt, MLA_QK_PAD - MLA_NOPE - MLA_ROPE), F32)
    dq, dkv = MLA_NOPE + MLA_ROPE, MLA_NOPE + MLA_V
    qs, ks, vs = [], [], []
    for h in range(HEADS):
        qs.append(jnp.concatenate([q[:, h * dq:h * dq + MLA_NOPE], rope(q[:, h * dq + MLA_NOPE:(h + 1) * dq]), pad], axis=-1))
        ks.append(jnp.concatenate([kv[:, h * dkv:h * dkv + MLA_NOPE], kpe, pad], axis=-1))
        vs.append(kv[:, h * dkv + MLA_NOPE:(h + 1) * dkv])
    return qs, ks, vs


def _mla_prep_specs(tt):
    zs = pl.BlockSpec((tt, WIDTH), lambda i: (i, Z_MLA // WIDTH))
    cs = pl.BlockSpec((tt, MLA_ROPE // 2), lambda i: (i, 0))
    return [zs, cs, cs, pl.BlockSpec((1, MLA_Q_RANK), lambda i: (0, 0)), pl.BlockSpec((1, MLA_KV_RANK), lambda i: (0, 0)),
            pl.BlockSpec((MLA_Q_RANK, HEADS * (MLA_NOPE + MLA_ROPE)), lambda i: (0, 0)),
            pl.BlockSpec((MLA_KV_RANK, HEADS * (MLA_NOPE + MLA_V)), lambda i: (0, 0))]


def mla_prep_fwd(z, cos, sin, gq, gkv, wuq, wukv):
    t = z.shape[0]
    tt = _row_tile(t)

    def body(z_ref, c_ref, s_ref, gq_ref, gkv_ref, wq_ref, wkv_ref, q_ref, k_ref, v_ref):
        qs, ks, vs = _mla_prep(z_ref[...], c_ref[...], s_ref[...], gq_ref[...], gkv_ref[...],
                               wq_ref[...].astype(F32), wkv_ref[...].astype(F32))
        for h in range(HEADS):
            q_ref[h] = qs[h].astype(BF16)
            k_ref[h] = ks[h].astype(BF16)
            v_ref[h] = vs[h].astype(BF16)

    hs = pl.BlockSpec((HEADS, tt, MLA_QK_PAD), lambda i: (0, i, 0))
    vsp = pl.BlockSpec((HEADS, tt, MLA_V), lambda i: (0, i, 0))
    return pl.pallas_call(
        body, grid=(t // tt,), in_specs=_mla_prep_specs(tt), out_specs=[hs, hs, vsp],
        out_shape=[jax.ShapeDtypeStruct((HEADS, t, MLA_QK_PAD), BF16), jax.ShapeDtypeStruct((HEADS, t, MLA_QK_PAD), BF16),
                   jax.ShapeDtypeStruct((HEADS, t, MLA_V), BF16)],
        name="mla_prep_fwd", compiler_params=_cparams(("parallel",)),
    )(z, cos, sin, gq, gkv, wuq, wukv)


def mla_prep_bwd(z, cos, sin, gq, gkv, wuq, wukv, dq, dk, dv):
    t = z.shape[0]
    tt = _row_tile(t)

    def body(z_ref, c_ref, s_ref, gq_ref, gkv_ref, wq_ref, wkv_ref, dq_ref, dk_ref, dv_ref,
             dz_ref, dgq_ref, dgkv_ref, dwq_ref, dwkv_ref):
        cos_v, sin_v = c_ref[...], s_ref[...]
        _, vjp = jax.vjp(lambda zm, a, b, wq, wkv: _mla_prep(zm, cos_v, sin_v, a, b, wq, wkv),
                         z_ref[...], gq_ref[...], gkv_ref[...], wq_ref[...].astype(F32), wkv_ref[...].astype(F32))
        cot = ([dq_ref[h] for h in range(HEADS)], [dk_ref[h] for h in range(HEADS)], [dv_ref[h] for h in range(HEADS)])
        dz, dgq, dgkv, dwq, dwkv = vjp(cot)
        dz_ref[...] = dz.astype(BF16)

        @pl.when(pl.program_id(0) == 0)
        def _():
            for r in (dgq_ref, dgkv_ref, dwq_ref, dwkv_ref):
                r[...] = jnp.zeros_like(r)

        dgq_ref[...] += dgq
        dgkv_ref[...] += dgkv
        dwq_ref[...] += dwq
        dwkv_ref[...] += dwkv

    specs = _mla_prep_specs(tt)
    hs = pl.BlockSpec((HEADS, tt, MLA_QK_PAD), lambda i: (0, i, 0))
    vsp = pl.BlockSpec((HEADS, tt, MLA_V), lambda i: (0, i, 0))
    nq, nkv = HEADS * (MLA_NOPE + MLA_ROPE), HEADS * (MLA_NOPE + MLA_V)
    return pl.pallas_call(
        body, grid=(t // tt,), in_specs=specs + [hs, hs, vsp],
        out_specs=[pl.BlockSpec((tt, WIDTH), lambda i: (i, 0)), specs[3], specs[4], specs[5], specs[6]],
        out_shape=[jax.ShapeDtypeStruct((t, WIDTH), BF16), jax.ShapeDtypeStruct((1, MLA_Q_RANK), F32),
                   jax.ShapeDtypeStruct((1, MLA_KV_RANK), F32), jax.ShapeDtypeStruct((MLA_Q_RANK, nq), F32),
                   jax.ShapeDtypeStruct((MLA_KV_RANK, nkv), F32)],
        name="mla_prep_bwd", compiler_params=_cparams(("arbitrary",)),
    )(z, cos, sin, gq, gkv, wuq, wukv, dq, dk, dv)


def _att_blocks(t):
    return min(ATT_BQ, t), min(ATT_BK, t)


def _key_blocks(a, bk):
    h, t, d = a.shape
    return a.reshape(h, t // bk, bk, d).transpose(0, 1, 3, 2)


def _att_positions(kb, qi, bq, bk, lo):
    krow = kb * bk + lax.broadcasted_iota(jnp.int32, (bk, bq - lo), 0)
    qcol = qi * bq + lo + lax.broadcasted_iota(jnp.int32, (bk, bq - lo), 1)
    return krow, qcol


def _tail(x, lo):
    return x if lo == 0 else x[:, lo:]


def _join(old, new_tail, lo):
    return new_tail if lo == 0 else jnp.concatenate([old[:, :lo], new_tail], axis=1)


def _walk(qi, nd, bk, step, carry, diagonal_first=False):
    n_full = qi * nd
    diagonal = [(n_full + d, d * bk) for d in range(nd)]
    if diagonal_first:
        for kb, lo in reversed(diagonal):
            carry = step(kb, carry, True, lo)
        return lax.fori_loop(0, n_full, lambda i, c: step(n_full - 1 - i, c, False, 0), carry)
    carry = lax.fori_loop(0, n_full, lambda kb, c: step(kb, c, False, 0), carry)
    for kb, lo in diagonal:
        carry = step(kb, carry, True, lo)
    return carry


def softmax_att_fwd(qt, k, vtb, scale):
    hh, dk, t = qt.shape
    dv = vtb.shape[2]
    bq, bk = _att_blocks(t)

    def body(qt_ref, k_ref, vt_ref, o_ref, lse_ref):
        qi = pl.program_id(1)
        qtv = qt_ref[0]

        def step(kb, carry, masked, lo):
            m_all, l_all, acc_all = carry
            m, l, acc = _tail(m_all, lo), _tail(l_all, lo), _tail(acc_all, lo)
            ks = pl.multiple_of(kb * bk, bk)
            s = jnp.dot(k_ref[0, pl.ds(ks, bk), :], _tail(qtv, lo), preferred_element_type=F32) * scale
            if masked:
                krow, qcol = _att_positions(kb, qi, bq, bk, lo)
                s = jnp.where(krow <= qcol, s, NEG_BIG)
            m_new = jnp.maximum(m, jnp.max(s, axis=0, keepdims=True))
            p = jnp.exp(s - m_new)
            alpha = jnp.exp(m - m_new)
            l = alpha * l + jnp.sum(p, axis=0, keepdims=True)
            acc = alpha * acc + jnp.dot(vt_ref[0, kb], p.astype(BF16), preferred_element_type=F32)
            return _join(m_all, m_new, lo), _join(l_all, l, lo), _join(acc_all, acc, lo)

        init = (jnp.full((1, bq), NEG_BIG, F32), jnp.zeros((1, bq), F32), jnp.zeros((dv, bq), F32))
        m, l, acc = _walk(qi, bq // bk, bk, step, init)
        o_ref[0] = acc / l
        lse_ref[0] = m + jnp.log(l)

    return pl.pallas_call(
        body, grid=(hh, t // bq),
        in_specs=[pl.BlockSpec((1, dk, bq), lambda h, i: (h, 0, i)), pl.BlockSpec((1, t, dk), lambda h, i: (h, 0, 0)),
                  pl.BlockSpec((1, t // bk, dv, bk), lambda h, i: (h, 0, 0, 0))],
        out_specs=[pl.BlockSpec((1, dv, bq), lambda h, i: (h, 0, i)), pl.BlockSpec((1, 1, bq), lambda h, i: (h, 0, i))],
        out_shape=[jax.ShapeDtypeStruct((hh, dv, t), F32), jax.ShapeDtypeStruct((hh, 1, t), F32)],
        name="softmax_att_fwd", compiler_params=_cparams(("parallel", "arbitrary")),
    )(qt, k, vtb)


def softmax_att_bwd(qt, q, k, ktb, v, ot, lse, do, dot_, scale):
    hh, dk, t = qt.shape
    dv = v.shape[2]
    bq, bk = _att_blocks(t)

    def body(qt_ref, q_ref, k_ref, kt_ref, v_ref, ot_ref, lse_ref, do_ref, dot_ref, dqt_ref, dk_ref, dv_ref):
        qi = pl.program_id(1)

        @pl.when(qi == 0)
        def _():
            dk_ref[...] = jnp.zeros_like(dk_ref)
            dv_ref[...] = jnp.zeros_like(dv_ref)

        qtv, qv, dov, dotv = qt_ref[0], q_ref[0], do_ref[0], dot_ref[0]
        lse_v = lse_ref[0]
        delta = jnp.sum(dotv.astype(F32) * ot_ref[0], axis=0, keepdims=True)

        def step(kb, dqt, masked, lo):
            ks = pl.multiple_of(kb * bk, bk)
            s = jnp.dot(k_ref[0, pl.ds(ks, bk), :], _tail(qtv, lo), preferred_element_type=F32) * scale
            if masked:
                krow, qcol = _att_positions(kb, qi, bq, bk, lo)
                s = jnp.where(krow <= qcol, s, NEG_BIG)
            p = jnp.exp(s - _tail(lse_v, lo))
            dp = jnp.dot(v_ref[0, pl.ds(ks, bk), :], _tail(dotv, lo), preferred_element_type=F32)
            ds = (p * (dp - _tail(delta, lo)) * scale).astype(BF16)
            dk_ref[0, pl.ds(ks, bk), :] += jnp.dot(ds, qv[lo:], preferred_element_type=F32)
            dv_ref[0, pl.ds(ks, bk), :] += jnp.dot(p.astype(BF16), dov[lo:], preferred_element_type=F32)
            return _join(dqt, _tail(dqt, lo) + jnp.dot(kt_ref[0, kb], ds, preferred_element_type=F32), lo)

        dqt_ref[0] = _walk(qi, bq // bk, bk, step, jnp.zeros((dk, bq), F32))

    nkb = t // bk
    qts = pl.BlockSpec((1, dk, bq), lambda h, i: (h, 0, i))
    qs = pl.BlockSpec((1, bq, dk), lambda h, i: (h, i, 0))
    kfull = pl.BlockSpec((1, t, dk), lambda h, i: (h, 0, 0))
    vfull = pl.BlockSpec((1, t, dv), lambda h, i: (h, 0, 0))
    vts = pl.BlockSpec((1, dv, bq), lambda h, i: (h, 0, i))
    return pl.pallas_call(
        body, grid=(hh, t // bq),
        in_specs=[qts, qs, kfull, pl.BlockSpec((1, nkb, dk, bk), lambda h, i: (h, 0, 0, 0)), vfull, vts,
                  pl.BlockSpec((1, 1, bq), lambda h, i: (h, 0, i)), pl.BlockSpec((1, bq, dv), lambda h, i: (h, i, 0)), vts],
        out_specs=[qts, kfull, vfull],
        out_shape=[jax.ShapeDtypeStruct((hh, dk, t), F32), jax.ShapeDtypeStruct((hh, t, dk), F32),
                   jax.ShapeDtypeStruct((hh, t, dv), F32)],
        name="softmax_att_bwd", compiler_params=_cparams(("parallel", "arbitrary")),
    )(qt, q, k, ktb, v, ot, lse, do, dot_)


SUM_ROWS = 8


def _softplus(z):
    return jnp.maximum(z, 0.0) + jnp.log(1.0 + jnp.exp(-jnp.abs(z)))


def _tri_ext(bk, kind):
    r = lax.broadcasted_iota(jnp.int32, (bk + SUM_ROWS, bk), 0)
    c = lax.broadcasted_iota(jnp.int32, (bk + SUM_ROWS, bk), 1)
    if kind == "neg_later":
        return jnp.where((c > r) | (r >= bk), -1.0, 0.0).astype(BF16)
    return jnp.where((c <= r) | (r >= bk), 1.0, 0.0).astype(BF16)


def _split_dot(m01, x):
    hi = lax.bitcast_convert_type(lax.bitcast_convert_type(x, jnp.uint32) & jnp.uint32(0xFFFF0000), F32)
    lo = x - hi
    return (jnp.dot(m01, hi.astype(BF16), preferred_element_type=F32) + jnp.dot(m01, lo.astype(BF16), preferred_element_type=F32))


def sb_cast(z):
    t = z.shape[0]
    tt = _row_tile(t)
    cols = 3 * WIDTH
    assert Z_SB % cols == 0

    def body(z_ref, o_ref):
        o_ref[...] = z_ref[...].astype(BF16)

    return pl.pallas_call(
        body, grid=(t // tt,), in_specs=[pl.BlockSpec((tt, cols), lambda i: (i, Z_SB // cols))],
        out_specs=pl.BlockSpec((tt, cols), lambda i: (i, 0)), out_shape=jax.ShapeDtypeStruct((t, cols), BF16),
        name="sb_cast", compiler_params=_cparams(("parallel",)),
    )(z)


def _first_last_step(nh, nq):
    h, i = pl.program_id(0), pl.program_id(1)
    return (h == 0) & (i == 0), (h == nh - 1) & (i == nq - 1)


def stick_att_fwd(qt, k, vtb, comm=None):
    hh, dk, t = qt.shape
    dv = vtb.shape[2]
    bq, bk = _att_blocks(t)
    nkb = t // bk
    srcs, gather, c_specs, c_shapes, c_sems = _with_exchange(comm)
    n = len(srcs)

    def body(*refs):
        qt_ref, k_ref, vt_ref = refs[:3]
        o_ref, rs_ref = refs[3 + n:5 + n]
        if n:
            start, wait = _exchange_ops(refs[3:3 + n], refs[5 + n:5 + 2 * n], *refs[5 + 2 * n:], gather)
            first, last = _first_last_step(hh, t // bq)
            pl.when(first)(start)
        qi = pl.program_id(1)
        qtv = qt_ref[0]
        neg_later = _tri_ext(bk, "neg_later")

        def step(kb, carry, masked, lo):
            r_all, acc_all = carry
            r, acc = _tail(r_all, lo), _tail(acc_all, lo)
            ks = pl.multiple_of(kb * bk, bk)
            z = jnp.dot(k_ref[0, pl.ds(ks, bk), :], _tail(qtv, lo), preferred_element_type=F32)
            sp = _softplus(z)
            la = z - sp
            if masked:
                krow, qcol = _att_positions(kb, qi, bq, bk, lo)
                mask = krow < qcol
                sp = jnp.where(mask, sp, 0.0)
            rs_ref[0, kb] = r_all
            sums = _split_dot(neg_later, sp)
            a = jnp.exp(la + sums[:bk])
            if masked:
                a = jnp.where(mask, a, 0.0)
            acc = acc + jnp.dot(vt_ref[0, kb], a.astype(BF16), preferred_element_type=F32) * jnp.exp(r)
            return _join(r_all, r + sums[bk:bk + 1], lo), _join(acc_all, acc, lo)

        init = (jnp.zeros((1, bq), F32), jnp.zeros((dv, bq), F32))
        o_ref[0] = _walk(qi, bq // bk, bk, step, init, diagonal_first=True)[1]
        if n:
            pl.when(last)(wait)

    outs = pl.pallas_call(
        body, grid=(hh, t // bq),
        in_specs=[pl.BlockSpec((1, dk, bq), lambda h, i: (h, 0, i)), pl.BlockSpec((1, t, dk), lambda h, i: (h, 0, 0)),
                  pl.BlockSpec((1, nkb, dv, bk), lambda h, i: (h, 0, 0, 0))] + c_specs,
        out_specs=[pl.BlockSpec((1, dv, bq), lambda h, i: (h, 0, i)),
                   pl.BlockSpec((1, nkb, 1, bq), lambda h, i: (h, 0, 0, i))] + c_specs,
        out_shape=[jax.ShapeDtypeStruct((hh, dv, t), F32), jax.ShapeDtypeStruct((hh, nkb, 1, t), F32)] + c_shapes,
        scratch_shapes=c_sems, name="stick_att_fwd_exchange" if n else "stick_att_fwd",
        compiler_params=pltpu.CompilerParams(dimension_semantics=("arbitrary", "arbitrary") if n else ("parallel", "arbitrary"),
                                             vmem_limit_bytes=VMEM_LIMIT_V7X, has_side_effects=bool(n)),
    )(qt, k, vtb, *srcs)
    return outs[0], outs[1], outs[2:]


def stick_att_bwd(qt, q, k, ktb, v, rs, do, dot_, scale, comm=None):
    hh, dk, t = qt.shape
    dv = v.shape[2]
    bq, bk = _att_blocks(t)
    nkb = t // bk
    srcs, gather, c_specs, c_shapes, c_sems = _with_exchange(comm)
    n = len(srcs)

    def body(*refs):
        qt_ref, q_ref, k_ref, kt_ref, v_ref, rs_ref, do_ref, dot_ref = refs[:8]
        dqt_ref, dk_ref, dv_ref = refs[8 + n:11 + n]
        if n:
            start, wait = _exchange_ops(refs[8:8 + n], refs[11 + n:11 + 2 * n], *refs[11 + 2 * n:], gather)
            first, last = _first_last_step(hh, t // bq)
            pl.when(first)(start)
        qi = pl.program_id(1)

        @pl.when(qi == 0)
        def _():
            dk_ref[...] = jnp.zeros_like(dk_ref)
            dv_ref[...] = jnp.zeros_like(dv_ref)

        qtv, qv, dov, dotv = qt_ref[0], q_ref[0], do_ref[0], dot_ref[0]
        neg_later, upto = _tri_ext(bk, "neg_later"), _tri_ext(bk, "upto")

        def step(kb, carry, masked, lo):
            pre_all, dqt_all = carry
            pre, dqt = _tail(pre_all, lo), _tail(dqt_all, lo)
            ks = pl.multiple_of(kb * bk, bk)
            z = jnp.dot(k_ref[0, pl.ds(ks, bk), :], _tail(qtv, lo), preferred_element_type=F32)
            sp = _softplus(z)
            la = z - sp
            if masked:
                krow, qcol = _att_positions(kb, qi, bq, bk, lo)
                mask = krow < qcol
                sp = jnp.where(mask, sp, 0.0)
            a = jnp.exp(la + _split_dot(neg_later, sp)[:bk] + _tail(rs_ref[0, kb], lo))
            if masked:
                a = jnp.where(mask, a, 0.0)
            beta = jnp.exp(la)
            g = a * jnp.dot(v_ref[0, pl.ds(ks, bk), :], _tail(dotv, lo), preferred_element_type=F32)
            sums = jnp.dot(upto, g.astype(BF16), preferred_element_type=F32)
            dz = g - beta * (pre + sums[:bk])
            if masked:
                dz = jnp.where(mask, dz, 0.0)
            dzb = dz.astype(BF16)
            dk_ref[0, pl.ds(ks, bk), :] += jnp.dot(dzb, qv[lo:], preferred_element_type=F32)
            dv_ref[0, pl.ds(ks, bk), :] += jnp.dot(a.astype(BF16), dov[lo:], preferred_element_type=F32)
            dqt = dqt + jnp.dot(kt_ref[0, kb], dzb, preferred_element_type=F32)
            return _join(pre_all, pre + sums[bk:bk + 1], lo), _join(dqt_all, dqt, lo)

        init = (jnp.zeros((1, bq), F32), jnp.zeros((dk, bq), F32))
        dqt_ref[0] = _walk(qi, bq // bk, bk, step, init)[1] * scale
        if n:
            pl.when(last)(wait)

    qts = pl.BlockSpec((1, dk, bq), lambda h, i: (h, 0, i))
    qs = pl.BlockSpec((1, bq, dk), lambda h, i: (h, i, 0))
    kfull = pl.BlockSpec((1, t, dk), lambda h, i: (h, 0, 0))
    vfull = pl.BlockSpec((1, t, dv), lambda h, i: (h, 0, 0))
    outs = pl.pallas_call(
        body, grid=(hh, t // bq),
        in_specs=[qts, qs, kfull, pl.BlockSpec((1, nkb, dk, bk), lambda h, i: (h, 0, 0, 0)), vfull,
                  pl.BlockSpec((1, nkb, 1, bq), lambda h, i: (h, 0, 0, i)), pl.BlockSpec((1, bq, dv), lambda h, i: (h, i, 0)),
                  pl.BlockSpec((1, dv, bq), lambda h, i: (h, 0, i))] + c_specs,
        out_specs=[qts, kfull, vfull] + c_specs,
        out_shape=[jax.ShapeDtypeStruct((hh, dk, t), F32), jax.ShapeDtypeStruct((hh, t, dk), F32),
                   jax.ShapeDtypeStruct((hh, t, dv), F32)] + c_shapes,
        scratch_shapes=c_sems, name="stick_att_bwd_exchange" if n else "stick_att_bwd",
        compiler_params=pltpu.CompilerParams(dimension_semantics=("arbitrary", "arbitrary") if n else ("parallel", "arbitrary"),
                                             vmem_limit_bytes=VMEM_LIMIT_V7X, has_side_effects=bool(n)),
    )(qt, q, k, ktb, v, rs, do, dot_, *srcs)
    return outs[0], outs[1], outs[2], outs[3:]


def mod_fwd(c_all, ada_w, ada_b_cols):
    nl, d, n = ada_w.shape

    def body(c_ref, w_ref, b_ref, o_ref):
        act = jax.nn.silu(c_ref[...])
        o_ref[0] = jnp.dot(act, w_ref[0], precision=lax.Precision.HIGHEST, preferred_element_type=F32) + b_ref[0]

    return pl.pallas_call(
        body, grid=(nl,),
        in_specs=[pl.BlockSpec((N_DEV, d), lambda l: (0, 0)), pl.BlockSpec((1, d, n), lambda l: (l, 0, 0)),
                  pl.BlockSpec((1, 1, n), lambda l: (l, 0, 0))],
        out_specs=pl.BlockSpec((1, N_DEV, n), lambda l: (l, 0, 0)), out_shape=jax.ShapeDtypeStruct((nl, N_DEV, n), F32),
        name="mod_fwd", compiler_params=_cparams(("parallel",)),
    )(c_all, ada_w, ada_b_cols)


def ada_w_grad(c_all_t, dmod_cols):
    d = c_all_t.shape[0]
    nl, _, n = dmod_cols.shape

    def body(c_ref, dm_ref, o_ref):
        act = jax.nn.silu(c_ref[...])
        dm = dm_ref[0]
        acc = act[:, 0:1] * dm[0:1, :]
        for e in range(1, N_DEV):
            acc = acc + act[:, e:e + 1] * dm[e:e + 1, :]
        o_ref[0] = acc

    return pl.pallas_call(
        body, grid=(nl,),
        in_specs=[pl.BlockSpec((d, N_DEV), lambda l: (0, 0)), pl.BlockSpec((1, N_DEV, n), lambda l: (l, 0, 0))],
        out_specs=pl.BlockSpec((1, d, n), lambda l: (l, 0, 0)), out_shape=jax.ShapeDtypeStruct((nl, d, n), F32),
        name="ada_w_grad", compiler_params=_cparams(("parallel",)),
    )(c_all_t, dmod_cols)


ADAM_BLOCK_BYTES = 6 * 1024 * 1024


def _rows2d(a, lead=0):
    return a.reshape(a.shape[:lead] + (-1, a.shape[-1])) if a.ndim > lead + 1 else a.reshape(a.shape[:lead] + (1, -1))


def adamw(slots, w, m, v, *, name):
    shape = w.shape
    s2 = _rows2d(slots, 1)
    w2, m2, v2 = _rows2d(w), _rows2d(m), _rows2d(v)
    ns, r, c = s2.shape
    per_row = c * (ns * s2.dtype.itemsize + 7 * 4)
    tr = r
    if r * per_row > ADAM_BLOCK_BYTES:
        tr = max(t for t in range(8, r, 8) if r % t == 0 and t * per_row <= ADAM_BLOCK_BYTES)

    def body(s_ref, w_ref, m_ref, v_ref, g_ref, d_ref, mo_ref, vo_ref):
        g = s_ref[0].astype(F32)
        for i in range(1, ns):
            g = g + s_ref[i].astype(F32)
        mn = ADAM_B1 * m_ref[...] + (1.0 - ADAM_B1) * g
        vn = ADAM_B2 * v_ref[...] + (1.0 - ADAM_B2) * jnp.square(g)
        m_hat = mn / (1.0 - ADAM_B1 ** ADAM_STEP)
        v_hat = vn / (1.0 - ADAM_B2 ** ADAM_STEP)
        g_ref[...] = g
        d_ref[...] = -ADAM_LR * (m_hat / (jnp.sqrt(v_hat) + ADAM_EPS) + ADAM_WD * w_ref[...])
        mo_ref[...] = mn
        vo_ref[...] = vn

    row = pl.BlockSpec((tr, c), lambda i: (i, 0))
    outs = pl.pallas_call(
        body, grid=(r // tr,), in_specs=[pl.BlockSpec((ns, tr, c), lambda i: (0, i, 0)), row, row, row],
        out_specs=[row] * 4, out_shape=[jax.ShapeDtypeStruct((r, c), F32)] * 4, name=name,
        compiler_params=_cparams(("parallel",)),
    )(s2, w2, m2, v2)
    return [o.reshape(shape) for o in outs]


def _exchange(srcs, *, gather, name):
    n = len(srcs)

    def body(*refs):
        start, wait = _exchange_ops(refs[:n], refs[n:2 * n], *refs[2 * n:], gather)
        start()
        wait()

    return pl.pallas_call(
        body, in_specs=[ANY_SPEC] * n, out_specs=[ANY_SPEC] * n, out_shape=_exchange_out_shapes(srcs, gather),
        scratch_shapes=_exchange_sems(n), name=name, compiler_params=pltpu.CompilerParams(has_side_effects=True),
    )(*srcs)


ANY_SPEC = pl.BlockSpec(memory_space=pl.ANY)


def _exchange_out_shapes(srcs, gather):
    return [jax.ShapeDtypeStruct((N_DEV,) + (tuple(s.shape) if gather else tuple(s.shape[1:])), s.dtype) for s in srcs]


def _exchange_sems(n):
    return [pltpu.SemaphoreType.DMA((N_DEV - 1, n)), pltpu.SemaphoreType.DMA((N_DEV - 1, n)), pltpu.SemaphoreType.DMA((n,))]


def _exchange_ops(src_refs, out_refs, send_sems, recv_sems, local_sems, gather):
    n = len(src_refs)
    mx, my, mc = lax.axis_index("x"), lax.axis_index("y"), lax.axis_index("c")
    me = 4 * mx + 2 * my + mc
    peers = []
    for k in range(1, N_DEV):
        px = 1 - mx if k & 4 else mx
        py = 1 - my if k & 2 else my
        pc = 1 - mc if k & 1 else mc
        peers.append(((px, py, pc), 4 * px + 2 * py + pc))

    def part(a, idx):
        return src_refs[a] if gather else src_refs[a].at[idx]

    def copy(k, a, slot):
        dev, pid = peers[k]
        return pltpu.make_async_remote_copy(src_ref=part(a, pid), dst_ref=out_refs[a].at[slot], send_sem=send_sems.at[k, a],
                                            recv_sem=recv_sems.at[k, a], device_id=dev, device_id_type=MESH)

    def local(a):
        return pltpu.make_async_copy(part(a, me), out_refs[a].at[me], local_sems.at[a])

    def start():
        for a in range(n):
            local(a).start()
            for k in range(N_DEV - 1):
                copy(k, a, me).start()

    def wait():
        for a in range(n):
            for k in range(N_DEV - 1):
                copy(k, a, me).wait_send()
                copy(k, a, peers[k][1]).wait_recv()
            local(a).wait()

    return start, wait


def _with_exchange(comm):
    srcs, gather = comm if comm is not None else ((), True)
    n = len(srcs)
    return (list(srcs), gather, [ANY_SPEC] * n, _exchange_out_shapes(srcs, gather) if n else [], _exchange_sems(n) if n else [])


def _to_blocks(full, ax):
    shp = full.shape
    g = full.reshape(shp[:ax] + (N_DEV, shp[ax] // N_DEV) + shp[ax + 1:])
    return jnp.moveaxis(g, ax, 0)


def _from_blocks(seg, ax):
    g = jnp.moveaxis(seg, 0, ax)
    shp = g.shape
    return g.reshape(shp[:ax] + (shp[ax] * shp[ax + 1],) + shp[ax + 2:])


def _to_heads(a, dh):
    return a.reshape(a.shape[0], -1, dh).transpose(1, 0, 2)


def _from_heads(a):
    return a.transpose(1, 0, 2).reshape(a.shape[1], -1)


def _from_heads_t(a):
    return a.transpose(2, 0, 1).reshape(a.shape[2], -1)


MLA_SCALE = (MLA_NOPE + MLA_ROPE) ** -0.5
SB_SCALE = SB_DIM ** -0.5
assert math.frexp(SB_SCALE)[0] == 0.5


def _resid_epi(acc, x, g):
    return x + g * acc, acc


def _layer_fwd(x, mod, p, cos, sin, comm=None):
    t = x.shape[0]
    bk = _att_blocks(t)[1]
    sh1, sc1, g1, sh2, sc2, g2 = [mod[i:i + 1] for i in range(6)]
    h = normmod_fwd(x, p["norm1_g"], sc1, sh1)
    z = matmul(h, p["w_in"], name="mm_in")
    y_sg = sg_fwd(z, p["sg_norm_g"], p["sg_w"], p["sg_bt"])
    y_cv = conv_fwd(z, p["conv_w"])
    mq, mk, mv = mla_prep_fwd(z, cos, sin, p["gq"], p["gkv"], p["wuq"], p["wukv"])
    mot, lse = softmax_att_fwd(mq.transpose(0, 2, 1), mk, _key_blocks(mv, bk), MLA_SCALE)
    y_mla = _from_heads_t(mot).astype(BF16)
    zsb = sb_cast(z)
    sq, sk, sv = [_to_heads(zsb[:, i * WIDTH:(i + 1) * WIDTH], SB_DIM) for i in range(3)]
    sq = sq * SB_SCALE
    sot, rs, got = stick_att_fwd(sq.transpose(0, 2, 1), sk, _key_blocks(sv, bk), comm)
    y_sb = _from_heads_t(sot).astype(BF16)
    ys = (y_sg, y_cv, y_mla, y_sb)
    up, merged = merge_fwd(ys, p["w_branch"], z)
    x1, out = matmul(merged, p["w_out"], name="mm_out", out_dtypes=(F32, F32), epi=_resid_epi, epi_in=((x, "mn"), (g1, "n")))
    h2 = normmod_fwd(x1, p["norm2_g"], sc2, sh2)
    a, r = matmul(h2, p["w1"], name="mm_up", out_dtypes=(F32, BF16),
                  epi=lambda acc: (acc, jnp.square(jnp.maximum(acc, 0.0))))
    x2, mo = matmul(r, p["w2"], name="mm_down", out_dtypes=(F32, F32), epi=_resid_epi, epi_in=((x1, "mn"), (g2, "n")))
    saved = dict(x=x, h=h, z=z, ys=ys, mq=mq, mk=mk, mv=mv, mot=mot, lse=lse, sq=sq, sk=sk, sv=sv, rs=rs, up=up,
                 merged=merged, out=out, x1=x1, h2=h2, a=a, r=r, mo=mo)
    return x2, saved, got


def _layer_bwd(dx2, s, mod, p, cos, sin, comm=None):
    t = dx2.shape[0]
    bk = _att_blocks(t)[1]
    sh1, sc1, g1, sh2, sc2, g2 = [mod[i:i + 1] for i in range(6)]
    g = {}
    dm, dg2 = resid_bwd(dx2, s["mo"], g2)
    da = matmul(dm, p["w2_t"], name="mm_down_dx", out_dtypes=(BF16,), epi=lambda acc, a: (acc * (2.0 * jnp.maximum(a, 0.0)),),
                epi_in=((s["a"], "mn"),))
    g["mlp_w2"] = matmul_tn(s["r"], dm, name="mm_down_dw")
    dh2 = matmul(da, p["w1_t"], name="mm_up_dx")
    g["mlp_w1"] = matmul_tn(s["h2"], da, name="mm_up_dw")
    dx1, g["norm2_g"], dsc2, dsh2 = normmod_bwd(s["x1"], p["norm2_g"], sc2, sh2, dh2, dx2)
    dout, dg1 = resid_bwd(dx1, s["out"], g1)
    dmerged = matmul(dout, p["w_out_t"], name="mm_out_dx")
    g["w_out"] = matmul_tn(s["merged"], dout, name="mm_out_dw")
    dzg, dup = merge_bwd(dmerged, s["up"], s["z"])
    dys = [matmul(dup[n], p["w_branch_t"][n], name="mm_branch_dx") for n in range(N_BRANCH)]
    g["w_branch"] = jnp.stack([matmul_tn(s["ys"][n], dup[n], name="mm_branch_dw") for n in range(N_BRANCH)])
    z = s["z"]
    dz_sg, g["sg_norm_g"], g["sg_w"], dbt = sg_bwd(z, p["sg_norm_g"], p["sg_w"], p["sg_bt"], dys[0])
    g["sg_b"] = dbt.T
    dz_cv, g["conv_w"] = conv_bwd(z, p["conv_w"], dys[1])
    do = _to_heads(dys[2], MLA_V).astype(BF16)
    dqt, dk, dv = softmax_att_bwd(s["mq"].transpose(0, 2, 1), s["mq"], s["mk"], _key_blocks(s["mk"], bk), s["mv"], s["mot"],
                                  s["lse"], do, do.transpose(0, 2, 1), MLA_SCALE)
    dz_mla, g["mla_q_norm_g"], g["mla_kv_norm_g"], g["mla_w_uq"], g["mla_w_ukv"] = mla_prep_bwd(
        z, cos, sin, p["gq"], p["gkv"], p["wuq"], p["wukv"], dqt.transpose(0, 2, 1), dk, dv)
    do = _to_heads(dys[3], SB_DIM).astype(BF16)
    dqt, dk, dv, got = stick_att_bwd(s["sq"].transpose(0, 2, 1), s["sq"], s["sk"], _key_blocks(s["sk"], bk), s["sv"], s["rs"],
                                     do, do.transpose(0, 2, 1), SB_SCALE, comm)
    dz_sb = jnp.concatenate([_from_heads_t(dqt), _from_heads(dk), _from_heads(dv)], axis=1).astype(BF16)
    dz = jnp.concatenate([dz_sg, dz_cv, dz_mla, dz_sb, dzg], axis=1)
    dh = matmul(dz, p["w_in_t"], name="mm_in_dx", tk=2176)
    dw_in = matmul_tn(s["h"], dz, name="mm_in_dw")
    g["w_in"] = jnp.concatenate([dw_in[:, :Z_MLA + MLA_COLS], dw_in[:, Z_SB:]], axis=1)
    dx, g["norm1_g"], dsc1, dsh1 = normmod_bwd(s["x"], p["norm1_g"], sc1, sh1, dh, dx1)
    g["mla_q_norm_g"], g["mla_kv_norm_g"] = g["mla_q_norm_g"][0], g["mla_kv_norm_g"][0]
    g["norm1_g"], g["norm2_g"], g["sg_norm_g"] = g["norm1_g"][0], g["norm2_g"][0], g["sg_norm_g"][0]
    g["ada_b"] = jnp.concatenate([dsh1, dsc1, dg1, dsh2, dsc2, dg2], axis=1)[0]
    return dx, g, got


WEIGHT_NAMES = ("ada_w", "ada_b", "norm1_g", "norm2_g", "w_in", "sg_norm_g", "sg_w", "sg_b", "conv_w", "mla_q_norm_g",
                "mla_w_uq", "mla_kv_norm_g", "mla_w_ukv", "w_branch", "w_out", "mlp_w1", "mlp_w2", "final_norm_g")
SHARDED = (("w_in", 2), ("mla_w_uq", 2), ("mla_w_ukv", 2), ("w_branch", 3), ("w_out", 1), ("mlp_w1", 2), ("mlp_w2", 1),
           ("conv_w", 2))
REPLICATED = ("ada_b", "norm1_g", "norm2_g", "sg_norm_g", "sg_w", "sg_b", "mla_q_norm_g", "mla_kv_norm_g", "final_norm_g")


def kernel(x, c, positions, ada_w, ada_b, norm1_g, norm2_g, w_in, sg_norm_g, sg_w, sg_b, conv_w, mla_q_norm_g, mla_w_uq, mla_kv_norm_g, mla_w_ukv, w_branch, w_out, mlp_w1, mlp_w2, final_norm_g, loss_target, m_ada_w, m_ada_b, m_norm1_g, m_norm2_g, m_w_in, m_sg_norm_g, m_sg_w, m_sg_b, m_conv_w, m_mla_q_norm_g, m_mla_w_uq, m_mla_kv_norm_g, m_mla_w_ukv, m_w_branch, m_w_out, m_mlp_w1, m_mlp_w2, m_final_norm_g, v_ada_w, v_ada_b, v_norm1_g, v_norm2_g, v_w_in, v_sg_norm_g, v_sg_w, v_sg_b, v_conv_w, v_mla_q_norm_g, v_mla_w_uq, v_mla_kv_norm_g, v_mla_w_ukv, v_w_branch, v_w_out, v_mlp_w1, v_mlp_w2, v_final_norm_g):
    given = dict(locals())
    w = {n: given[n] for n in WEIGHT_NAMES}
    m = {n: given["m_" + n] for n in WEIGHT_NAMES}
    v = {n: given["v_" + n] for n in WEIGHT_NAMES}
    xs, tgt = x[0], loss_target[0]
    nl = ada_w.shape[0]
    me = 4 * lax.axis_index("x") + 2 * lax.axis_index("y") + lax.axis_index("c")

    inv_freq = ROPE_THETA ** (-jnp.arange(0, MLA_ROPE, 2, dtype=F32) / MLA_ROPE)
    ang = positions[0].astype(F32)[:, None] * inv_freq
    cos, sin = jnp.cos(ang), jnp.sin(ang)

    c_all, conv_blocks = _exchange([c, conv_w], gather=True, name="gather_c_conv")
    c_all = c_all[:, 0]
    conv_full = _from_blocks(conv_blocks, 2)
    ncol = ada_w.shape[2]
    ada_b_cols = lax.dynamic_slice_in_dim(ada_b, me * ncol, ncol, axis=1)[:, None, :]
    mod_cols = mod_fwd(c_all, ada_w, ada_b_cols)
    got = _exchange([mod_cols], gather=True, name="gather_mod")[0]
    mod = jnp.moveaxis(lax.dynamic_index_in_dim(got, me, axis=2, keepdims=False), 0, 1).reshape(nl, 6, D_MODEL)

    names = [n for n, _ in SHARDED if n != "conv_w"]
    cut = Z_MLA + MLA_COLS

    def weight_shards(lo, hi):
        return [w[n][lo:hi].astype(BF16) for n in names]

    def layer_weights(got, lo, hi):
        full = {n: _from_blocks(seg, ax) for (n, ax), seg in zip(SHARDED, got)}
        w_in_pad = jnp.concatenate([full["w_in"][:, :, :cut], jnp.zeros((hi - lo, D_MODEL, Z_SB - cut), BF16),
                                    full["w_in"][:, :, cut:]], axis=2)
        return [dict(
            norm1_g=norm1_g[l][None], norm2_g=norm2_g[l][None], sg_norm_g=sg_norm_g[l][None], sg_w=sg_w[l], sg_bt=sg_b[l].T,
            conv_w=conv_full[l], gq=mla_q_norm_g[l][None], gkv=mla_kv_norm_g[l][None], wuq=full["mla_w_uq"][i],
            wukv=full["mla_w_ukv"][i], w_in=w_in_pad[i], w_in_t=w_in_pad[i].T, w_branch=full["w_branch"][i],
            w_branch_t=full["w_branch"][i].transpose(0, 2, 1), w_out=full["w_out"][i], w_out_t=full["w_out"][i].T,
            w1=full["mlp_w1"][i], w1_t=full["mlp_w1"][i].T, w2=full["mlp_w2"][i], w2_t=full["mlp_w2"][i].T)
            for i, l in enumerate(range(lo, hi))]

    layers = layer_weights(_exchange(weight_shards(0, 1), gather=True, name="gather_weights_first"), 0, 1)

    saved = []
    xc = xs
    for l in range(nl):
        comm = (weight_shards(1, nl), True) if l == 0 and nl > 1 else None
        xc, s, got = _layer_fwd(xc, mod[l], layers[l], cos, sin, comm)
        saved.append(s)
        if comm is not None:
            layers += layer_weights(got, 1, nl)
    loss_part, dx, dgf = loss_head(xc, final_norm_g[None], tgt)
    loss = lax.psum(loss_part[0, 0], AXES)

    def grad_blocks(g):
        return [_to_blocks(g[n][None], ax).astype(BF16) for n, ax in SHARDED]

    grads, landed = [None] * nl, [None] * nl
    for l in reversed(range(nl)):
        comm = (grad_blocks(grads[l + 1]), False) if l + 1 < nl else None
        dx, grads[l], got = _layer_bwd(dx, saved[l], mod[l], layers[l], cos, sin, comm)
        if comm is not None:
            landed[l + 1] = got
    landed[0] = _exchange(grad_blocks(grads[0]), gather=False, name="exchange_grads_first")
    gfull = {n: jnp.stack([grads[l][n] for l in range(nl)]) for n in REPLICATED if n != "final_norm_g"}
    gfull["final_norm_g"] = dgf[0]

    out = {}

    def update(n, slots):
        for kind, arr in zip(("grad", "delta", "new_m", "new_v"), adamw(slots, w[n], m[n], v[n], name="adamw_" + n)):
            out[kind, n] = arr

    for i, (n, _) in enumerate(SHARDED):
        update(n, jnp.concatenate([landed[l][i] for l in range(nl)], axis=1))
    got = _exchange([gfull[n] for n in REPLICATED], gather=True, name="gather_small_grads")
    for n, slots in zip(REPLICATED, got):
        update(n, slots)
    dmod_cols = jnp.moveaxis(lax.dynamic_slice_in_dim(got[0], me * ncol, ncol, axis=2), 0, 1)
    update("ada_w", ada_w_grad(c_all.T, dmod_cols)[None])

    res = [loss, dx[None]]
    for kind in ("grad", "delta", "new_m", "new_v"):
        res += [out[kind, n] for n in WEIGHT_NAMES]
    return tuple(res)
```

```python
import functools
import math

import jax
import jax.numpy as jnp
from jax import lax
from jax.experimental import pallas as pl
from jax.experimental.pallas import tpu as pltpu

F32 = jnp.float32
BF16 = jnp.bfloat16

D_MODEL = 1024
WIDTH = 512
CHUNK = 128
SG_GROUPS = 4
HEADS = 8
MLA_NOPE = 64
MLA_ROPE = 32
MLA_V = 64
MLA_Q_RANK = 256
MLA_KV_RANK = 128
MLA_QK_PAD = 128
SB_DIM = 64
ROPE_THETA = 10000.0
NORM_EPS = 1e-6
N_BRANCH = 4
D_FF = 4096
Z_SG, Z_CONV, Z_MLA, Z_SB, Z_GATE, Z_COLS = 0, 1024, 2560, 3072, 4608, 8704
IN_COLS = 8608
MLA_COLS = MLA_Q_RANK + MLA_KV_RANK + MLA_ROPE

ADAM_LR, ADAM_B1, ADAM_B2, ADAM_EPS, ADAM_WD, ADAM_STEP = 0.001, 0.9, 0.999, 1e-08, 0.01, 10

N_DEV = 8
AXES = ("x", "y", "c")
MESH = pl.DeviceIdType.MESH
VMEM_LIMIT_V7X = 56 * 1024 * 1024
ATT_BQ = 2048
ATT_BK = 256
NEG_BIG = -1e30


def _cparams(sem=None):
    return pltpu.CompilerParams(dimension_semantics=sem, vmem_limit_bytes=VMEM_LIMIT_V7X)


def _div_tile(n, pref):
    if n <= pref:
        return n
    t = (pref // 128) * 128
    while t >= 128:
        if n % t == 0:
            return t
        t -= 128
    raise ValueError(f"no tile for {n}")


def _row_tile(t):
    return min(512, t)


def _dot(a, b, dims):
    return lax.dot_general(a.astype(BF16), b.astype(BF16), (dims, ((), ())), preferred_element_type=F32)


@jax.custom_vjp
def _mm(a, b):
    return _dot(a, b, ((1,), (0,)))


def _mm_fwd(a, b):
    return _mm(a, b), (a, b)


def _mm_bwd(res, g):
    a, b = res
    return _dot(g, b, ((1,), (1,))).astype(a.dtype), _dot(a, g, ((0,), (0,))).astype(b.dtype)


_mm.defvjp(_mm_fwd, _mm_bwd)


def _rms(x, g):
    return (x * lax.rsqrt(jnp.mean(x * x, axis=-1, keepdims=True) + NORM_EPS)) * g


def _normmod(x, g, sc, sh):
    return _rms(x, g) * (1.0 + sc) + sh


def normmod_fwd(x, g, sc, sh):
    t, d = x.shape
    tt = _row_tile(t)

    def body(x_ref, g_ref, sc_ref, sh_ref, h_ref):
        h_ref[...] = _normmod(x_ref[...], g_ref[...], sc_ref[...], sh_ref[...]).astype(BF16)

    row = pl.BlockSpec((tt, d), lambda i: (i, 0))
    vec = pl.BlockSpec((1, d), lambda i: (0, 0))
    return pl.pallas_call(
        body, grid=(t // tt,), in_specs=[row, vec, vec, vec], out_specs=row,
        out_shape=jax.ShapeDtypeStruct((t, d), BF16), name="normmod_fwd", compiler_params=_cparams(("parallel",)),
    )(x, g, sc, sh)


def normmod_bwd(x, g, sc, sh, dh, dres):
    t, d = x.shape
    tt = _row_tile(t)

    def body(x_ref, g_ref, sc_ref, sh_ref, dh_ref, dres_ref, dx_ref, dg_ref, dsc_ref, dsh_ref):
        _, vjp = jax.vjp(_normmod, x_ref[...], g_ref[...], sc_ref[...], sh_ref[...])
        dx, dg, dsc, dsh = vjp(dh_ref[...])
        dx_ref[...] = dx + dres_ref[...]

        @pl.when(pl.program_id(0) == 0)
        def _():
            dg_ref[...] = jnp.zeros_like(dg_ref)
            dsc_ref[...] = jnp.zeros_like(dsc_ref)
            dsh_ref[...] = jnp.zeros_like(dsh_ref)

        dg_ref[...] += dg
        dsc_ref[...] += dsc
        dsh_ref[...] += dsh

    row = pl.BlockSpec((tt, d), lambda i: (i, 0))
    vec = pl.BlockSpec((1, d), lambda i: (0, 0))
    vs = jax.ShapeDtypeStruct((1, d), F32)
    return pl.pallas_call(
        body, grid=(t // tt,), in_specs=[row, vec, vec, vec, row, row], out_specs=[row, vec, vec, vec],
        out_shape=[jax.ShapeDtypeStruct((t, d), F32), vs, vs, vs], name="normmod_bwd",
        compiler_params=_cparams(("arbitrary",)),
    )(x, g, sc, sh, dh, dres)


def resid_bwd(dxn, m, g):
    t, d = dxn.shape
    tt = _row_tile(t)

    def body(dx_ref, m_ref, g_ref, dm_ref, dg_ref):
        dx = dx_ref[...]
        dm_ref[...] = (dx * g_ref[...]).astype(BF16)

        @pl.when(pl.program_id(0) == 0)
        def _():
            dg_ref[...] = jnp.zeros_like(dg_ref)

        dg_ref[...] += jnp.sum(dx * m_ref[...], axis=0, keepdims=True)

    row = pl.BlockSpec((tt, d), lambda i: (i, 0))
    vec = pl.BlockSpec((1, d), lambda i: (0, 0))
    return pl.pallas_call(
        body, grid=(t // tt,), in_specs=[row, row, vec], out_specs=[row, vec],
        out_shape=[jax.ShapeDtypeStruct((t, d), BF16), jax.ShapeDtypeStruct((1, d), F32)], name="resid_bwd",
        compiler_params=_cparams(("arbitrary",)),
    )(dxn, m, g)


def loss_head(x, gf, tgt):
    t, d = x.shape
    tt = _row_tile(t)

    def f(xv, gv, tv):
        y = _rms(xv, gv)
        return 0.5 * jnp.sum(jnp.mean(jnp.square(y - tv), axis=-1))

    def body(x_ref, g_ref, t_ref, loss_ref, dx_ref, dg_ref):
        val, vjp = jax.vjp(lambda xv, gv: f(xv, gv, t_ref[...]), x_ref[...], g_ref[...])
        dx, dg = vjp(jnp.ones((), F32))
        dx_ref[...] = dx

        @pl.when(pl.program_id(0) == 0)
        def _():
            dg_ref[...] = jnp.zeros_like(dg_ref)
            loss_ref[...] = jnp.zeros_like(loss_ref)

        dg_ref[...] += dg
        loss_ref[...] += jnp.full(loss_ref.shape, val, F32)

    row = pl.BlockSpec((tt, d), lambda i: (i, 0))
    vec = pl.BlockSpec((1, d), lambda i: (0, 0))
    lsp = pl.BlockSpec((1, 128), lambda i: (0, 0))
    return pl.pallas_call(
        body, grid=(t // tt,), in_specs=[row, vec, row], out_specs=[lsp, row, vec],
        out_shape=[jax.ShapeDtypeStruct((1, 128), F32), jax.ShapeDtypeStruct((t, d), F32),
                   jax.ShapeDtypeStruct((1, d), F32)],
        name="loss_head", compiler_params=_cparams(("arbitrary",)),
    )(x, gf, tgt)


def matmul(a, b, *, name, out_dtypes=(F32,), epi=None, epi_in=(), tm=2048, tn=512, tk=1024):
    m, k = a.shape
    n = b.shape[1]
    tm, tn, tk = min(tm, m), _div_tile(n, tn), _div_tile(k, tk)
    nk = k // tk
    n_epi, n_out = len(epi_in), len(out_dtypes)

    def body(*refs):
        a_ref, b_ref = refs[:2]
        e_refs = refs[2:2 + n_epi]
        o_refs = refs[2 + n_epi:2 + n_epi + n_out]
        kk = pl.program_id(2)
        part = jnp.dot(a_ref[...].astype(BF16), b_ref[...].astype(BF16), preferred_element_type=F32)

        def finish(acc):
            outs = (acc,) if epi is None else epi(acc, *[r[...] for r in e_refs])
            for o_ref, o in zip(o_refs, outs):
                o_ref[...] = o.astype(o_ref.dtype)

        if nk == 1:
            finish(part)
            return
        acc_ref = refs[-1]

        @pl.when(kk == 0)
        def _():
            acc_ref[...] = part

        @pl.when((kk > 0) & (kk < nk - 1))
        def _():
            acc_ref[...] += part

        @pl.when(kk == nk - 1)
        def _():
            finish(acc_ref[...] + part)

    in_specs = [pl.BlockSpec((tm, tk), lambda i, j, kk: (i, kk)), pl.BlockSpec((tk, tn), lambda i, j, kk: (kk, j))]
    for _, kind in epi_in:
        in_specs.append(pl.BlockSpec((tm, tn), lambda i, j, kk: (i, j)) if kind == "mn"
                        else pl.BlockSpec((1, tn), lambda i, j, kk: (0, j)))
    out_spec = pl.BlockSpec((tm, tn), lambda i, j, kk: (i, j))
    outs = pl.pallas_call(
        body, grid=(m // tm, n // tn, nk), in_specs=in_specs, out_specs=[out_spec] * n_out,
        out_shape=[jax.ShapeDtypeStruct((m, n), dt) for dt in out_dtypes],
        scratch_shapes=[pltpu.VMEM((tm, tn), F32)] if nk > 1 else [], name=name,
        compiler_params=_cparams(("parallel", "parallel", "arbitrary")),
    )(a, b, *[arr for arr, _ in epi_in])
    return outs[0] if n_out == 1 else outs


def matmul_tn(a, b, *, name, tko=1024, tn=512, tt=2048):
    t, ko = a.shape
    n = b.shape[1]
    tko, tn, tt = _div_tile(ko, tko), _div_tile(n, tn), min(tt, t)
    nt = t // tt

    def body(a_ref, b_ref, o_ref):
        s = pl.program_id(2)
        part = _dot(a_ref[...], b_ref[...], ((0,), (0,)))

        @pl.when(s == 0)
        def _():
            o_ref[...] = part

        @pl.when(s > 0)
        def _():
            o_ref[...] += part

    return pl.pallas_call(
        body, grid=(ko // tko, n // tn, nt),
        in_specs=[pl.BlockSpec((tt, tko), lambda i, j, s: (s, i)), pl.BlockSpec((tt, tn), lambda i, j, s: (s, j))],
        out_specs=pl.BlockSpec((tko, tn), lambda i, j, s: (i, j)), out_shape=jax.ShapeDtypeStruct((ko, n), F32),
        name=name, compiler_params=_cparams(("parallel", "parallel", "arbitrary")),
    )(a, b)


def merge_fwd(ys, w_branch, z):
    t = ys[0].shape[0]
    d = D_MODEL
    tm, tn = min(512, t), 512
    gate0 = Z_GATE // tn

    def body(y0, y1, y2, y3, w_ref, g0, g1, g2, g3, up_ref, mg_ref):
        acc = jnp.zeros((tm, tn), F32)
        for n, (y_ref, g_ref) in enumerate(zip((y0, y1, y2, y3), (g0, g1, g2, g3))):
            up = jnp.dot(y_ref[...], w_ref[n], preferred_element_type=F32)
            up_ref[n] = up.astype(BF16)
            acc = acc + jax.nn.sigmoid(g_ref[...]) * up
        mg_ref[...] = acc.astype(BF16)

    ysp = pl.BlockSpec((tm, WIDTH), lambda i, j: (i, 0))
    gate_specs = [pl.BlockSpec((tm, tn), functools.partial(lambda i, j, n: (i, gate0 + n * (d // tn) + j), n=n))
                  for n in range(N_BRANCH)]
    return pl.pallas_call(
        body, grid=(t // tm, d // tn),
        in_specs=[ysp, ysp, ysp, ysp, pl.BlockSpec((N_BRANCH, WIDTH, tn), lambda i, j: (0, 0, j))] + gate_specs,
        out_specs=[pl.BlockSpec((N_BRANCH, tm, tn), lambda i, j: (0, i, j)), pl.BlockSpec((tm, tn), lambda i, j: (i, j))],
        out_shape=[jax.ShapeDtypeStruct((N_BRANCH, t, d), BF16), jax.ShapeDtypeStruct((t, d), BF16)],
        name="merge_fwd", compiler_params=_cparams(("parallel", "parallel")),
    )(*ys, w_branch, z, z, z, z)


def merge_bwd(dmerged, up, z):
    t, d = dmerged.shape
    tm, tn = min(512, t), 512
    gate0 = Z_GATE // tn
    nj = d // tn

    def body(dm_ref, up_ref, g_ref, dzg_ref, dup_ref):
        dm = dm_ref[...]
        s = jax.nn.sigmoid(g_ref[...])
        dzg_ref[...] = (dm * up_ref[0] * s * (1.0 - s)).astype(BF16)
        dup_ref[0] = (dm * s).astype(BF16)

    blk = pl.BlockSpec((1, tm, tn), lambda i, j, n: (n, i, j))
    return pl.pallas_call(
        body, grid=(t // tm, nj, N_BRANCH),
        in_specs=[pl.BlockSpec((tm, tn), lambda i, j, n: (i, j)), blk,
                  pl.BlockSpec((tm, tn), lambda i, j, n: (i, gate0 + n * nj + j))],
        out_specs=[pl.BlockSpec((tm, tn), lambda i, j, n: (i, n * nj + j)), blk],
        out_shape=[jax.ShapeDtypeStruct((t, N_BRANCH * d), BF16), jax.ShapeDtypeStruct((N_BRANCH, t, d), BF16)],
        name="merge_bwd", compiler_params=_cparams(("parallel", "parallel", "arbitrary")),
    )(dmerged, up, z)


def _sg_tile(zsg, g, w, bt):
    tt = zsg.shape[0]
    a = jax.nn.gelu(zsg)
    u, v = a[:, :WIDTH], a[:, WIDTH:]
    vc = v - jnp.mean(v, axis=-1, keepdims=True)
    v = (vc * lax.rsqrt(jnp.mean(vc * vc, axis=-1, keepdims=True) + NORM_EPS)) * g
    row = lax.broadcasted_iota(jnp.int32, (CHUNK, CHUNK), 0)
    col = lax.broadcasted_iota(jnp.int32, (CHUNK, CHUNK), 1)
    cols = []
    for gi in range(SG_GROUPS):
        wc = jnp.where(row >= col, w[gi], 0.0)
        bias = bt[:, gi:gi + 1]
        rows = [_mm(wc, v[ch * CHUNK:(ch + 1) * CHUNK, gi * CHUNK:(gi + 1) * CHUNK]) + bias for ch in range(tt // CHUNK)]
        cols.append(jnp.concatenate(rows, axis=0) if len(rows) > 1 else rows[0])
    return u * jnp.concatenate(cols, axis=1)


def sg_fwd(z, g, w, bt):
    t = z.shape[0]
    tt = _row_tile(t)

    def body(z_ref, g_ref, w_ref, b_ref, y_ref):
        y_ref[...] = _sg_tile(z_ref[...], g_ref[...], w_ref[...], b_ref[...]).astype(BF16)

    return pl.pallas_call(
        body, grid=(t // tt,),
        in_specs=[pl.BlockSpec((tt, 2 * WIDTH), lambda i: (i, 0)), pl.BlockSpec((1, WIDTH), lambda i: (0, 0)),
                  pl.BlockSpec((SG_GROUPS, CHUNK, CHUNK), lambda i: (0, 0, 0)), pl.BlockSpec((CHUNK, SG_GROUPS), lambda i: (0, 0))],
        out_specs=pl.BlockSpec((tt, WIDTH), lambda i: (i, 0)), out_shape=jax.ShapeDtypeStruct((t, WIDTH), BF16),
        name="sg_fwd", compiler_params=_cparams(("parallel",)),
    )(z, g, w, bt)


def sg_bwd(z, g, w, bt, dy):
    t = z.shape[0]
    tt = _row_tile(t)

    def body(z_ref, g_ref, w_ref, b_ref, dy_ref, dz_ref, dg_ref, dw_ref, db_ref):
        _, vjp = jax.vjp(_sg_tile, z_ref[...], g_ref[...], w_ref[...], b_ref[...])
        dz, dg, dw, db = vjp(dy_ref[...])
        dz_ref[...] = dz.astype(BF16)

        @pl.when(pl.program_id(0) == 0)
        def _():
            dg_ref[...] = jnp.zeros_like(dg_ref)
            dw_ref[...] = jnp.zeros_like(dw_ref)
            db_ref[...] = jnp.zeros_like(db_ref)

        dg_ref[...] += dg
        dw_ref[...] += dw
        db_ref[...] += db

    gs = pl.BlockSpec((1, WIDTH), lambda i: (0, 0))
    ws = pl.BlockSpec((SG_GROUPS, CHUNK, CHUNK), lambda i: (0, 0, 0))
    bs = pl.BlockSpec((CHUNK, SG_GROUPS), lambda i: (0, 0))
    return pl.pallas_call(
        body, grid=(t // tt,),
        in_specs=[pl.BlockSpec((tt, 2 * WIDTH), lambda i: (i, 0)), gs, ws, bs, pl.BlockSpec((tt, WIDTH), lambda i: (i, 0))],
        out_specs=[pl.BlockSpec((tt, 2 * WIDTH), lambda i: (i, 0)), gs, ws, bs],
        out_shape=[jax.ShapeDtypeStruct((t, 2 * WIDTH), BF16), jax.ShapeDtypeStruct((1, WIDTH), F32),
                   jax.ShapeDtypeStruct((SG_GROUPS, CHUNK, CHUNK), F32), jax.ShapeDtypeStruct((CHUNK, SG_GROUPS), F32)],
        name="sg_bwd", compiler_params=_cparams(("arbitrary",)),
    )(z, g, w, bt, dy)


HALO = 8


def _shift_down(halo, cur, k):
    tt = cur.shape[0]
    ext = jnp.concatenate([halo, cur], axis=0)
    return ext[HALO - k:HALO - k + tt]


def _shift_up(cur, halo, k):
    tt = cur.shape[0]
    ext = jnp.concatenate([cur, halo], axis=0)
    return ext[k:k + tt]


def conv_fwd(z, cw):
    t = z.shape[0]
    tt = _row_tile(t)
    cb = Z_CONV // WIDTH
    hb = tt // HALO

    def body(gb_ref, gc_ref, xv_ref, hgc_ref, hxv_ref, w_ref, y_ref):
        first = (pl.program_id(0) == 0)
        tcur = gc_ref[...] * xv_ref[...]
        thalo = jnp.where(first, 0.0, hgc_ref[...] * hxv_ref[...])
        w = w_ref[...]
        y = w[2:3] * tcur + w[1:2] * _shift_down(thalo, tcur, 1) + w[0:1] * _shift_down(thalo, tcur, 2)
        y_ref[...] = (gb_ref[...] * y).astype(BF16)

    def cur(off):
        return pl.BlockSpec((tt, WIDTH), lambda i: (i, cb + off))

    def prev(off):
        return pl.BlockSpec((HALO, WIDTH), lambda i: (jnp.maximum(i * hb - 1, 0), cb + off))

    return pl.pallas_call(
        body, grid=(t // tt,),
        in_specs=[cur(0), cur(1), cur(2), prev(1), prev(2), pl.BlockSpec((3, WIDTH), lambda i: (0, 0))],
        out_specs=pl.BlockSpec((tt, WIDTH), lambda i: (i, 0)), out_shape=jax.ShapeDtypeStruct((t, WIDTH), BF16),
        name="conv_fwd", compiler_params=_cparams(("parallel",)),
    )(z, z, z, z, z, cw)


def conv_bwd(z, cw, dy):
    t = z.shape[0]
    tt = _row_tile(t)
    nt = t // tt
    cb = Z_CONV // WIDTH
    hb = tt // HALO

    def body(gb_ref, gc_ref, xv_ref, hgc_ref, hxv_ref, ngb_ref, dy_ref, ndy_ref, w_ref, dz_ref, dw_ref):
        i = pl.program_id(0)
        gb, gc, xv = gb_ref[...], gc_ref[...], xv_ref[...]
        tcur = gc * xv
        thalo = jnp.where(i == 0, 0.0, hgc_ref[...] * hxv_ref[...])
        t1, t2 = _shift_down(thalo, tcur, 1), _shift_down(thalo, tcur, 2)
        w = w_ref[...]
        y = w[2:3] * tcur + w[1:2] * t1 + w[0:1] * t2
        dy = dy_ref[...]
        dyc = dy * gb
        dyn = jnp.where(i == nt - 1, 0.0, ndy_ref[...] * ngb_ref[...])
        dt = w[2:3] * dyc + w[1:2] * _shift_up(dyc, dyn, 1) + w[0:1] * _shift_up(dyc, dyn, 2)
        dz_ref[...] = jnp.concatenate([dy * y, dt * xv, dt * gc], axis=1).astype(BF16)

        @pl.when(i == 0)
        def _():
            dw_ref[...] = jnp.zeros_like(dw_ref)

        dw_ref[...] += jnp.concatenate([jnp.sum(dyc * t2, axis=0, keepdims=True), jnp.sum(dyc * t1, axis=0, keepdims=True),
                                        jnp.sum(dyc * tcur, axis=0, keepdims=True)], axis=0)

    def cur(off):
        return pl.BlockSpec((tt, WIDTH), lambda i: (i, cb + off))

    def prev(off):
        return pl.BlockSpec((HALO, WIDTH), lambda i: (jnp.maximum(i * hb - 1, 0), cb + off))

    last = t // HALO - 1
    nxt_z = pl.BlockSpec((HALO, WIDTH), lambda i: (jnp.minimum((i + 1) * hb, last), cb))
    nxt_dy = pl.BlockSpec((HALO, WIDTH), lambda i: (jnp.minimum((i + 1) * hb, last), 0))
    return pl.pallas_call(
        body, grid=(nt,),
        in_specs=[cur(0), cur(1), cur(2), prev(1), prev(2), nxt_z, pl.BlockSpec((tt, WIDTH), lambda i: (i, 0)), nxt_dy,
                  pl.BlockSpec((3, WIDTH), lambda i: (0, 0))],
        out_specs=[pl.BlockSpec((tt, 3 * WIDTH), lambda i: (i, 0)), pl.BlockSpec((3, WIDTH), lambda i: (0, 0))],
        out_shape=[jax.ShapeDtypeStruct((t, 3 * WIDTH), BF16), jax.ShapeDtypeStruct((3, WIDTH), F32)],
        name="conv_bwd", compiler_params=_cparams(("arbitrary",)),
    )(z, z, z, z, z, z, dy, dy, cw)


def _mla_prep(zm, cos, sin, gq, gkv, wuq, wukv):
    tt = zm.shape[0]
    cq, ckv, kpe = zm[:, :MLA_Q_RANK], zm[:, MLA_Q_RANK:MLA_Q_RANK + MLA_KV_RANK], zm[:, MLA_Q_RANK + MLA_KV_RANK:MLA_COLS]
    q = _mm(_rms(cq, gq), wuq)
    kv = _mm(_rms(ckv, gkv), wukv)
    half = MLA_ROPE // 2

    def rope(xr):
        x1, x2 = xr[:, :half], xr[:, half:]
        return jnp.concatenate([x1 * cos - x2 * sin, x2 * cos + x1 * sin], axis=-1)

    kpe = rope(kpe)
    pad = jnp.zeros((tt, MLA_QK_PAD - MLA_NOPE - MLA_ROPE), F32)
    dq, dkv = MLA_NOPE + MLA_ROPE, MLA_NOPE + MLA_V
    qs, ks, vs = [], [], []
    for h in range(HEADS):
        qs.append(jnp.concatenate([q[:, h * dq:h * dq + MLA_NOPE], rope(q[:, h * dq + MLA_NOPE:(h + 1) * dq]), pad], axis=-1))
        ks.append(jnp.concatenate([kv[:, h * dkv:h * dkv + MLA_NOPE], kpe, pad], axis=-1))
        vs.append(kv[:, h * dkv + MLA_NOPE:(h + 1) * dkv])
    return qs, ks, vs


def _mla_prep_specs(tt):
    zs = pl.BlockSpec((tt, WIDTH), lambda i: (i, Z_MLA // WIDTH))
    cs = pl.BlockSpec((tt, MLA_ROPE // 2), lambda i: (i, 0))
    return [zs, cs, cs, pl.BlockSpec((1, MLA_Q_RANK), lambda i: (0, 0)), pl.BlockSpec((1, MLA_KV_RANK), lambda i: (0, 0)),
            pl.BlockSpec((MLA_Q_RANK, HEADS * (MLA_NOPE + MLA_ROPE)), lambda i: (0, 0)),
            pl.BlockSpec((MLA_KV_RANK, HEADS * (MLA_NOPE + MLA_V)), lambda i: (0, 0))]


def mla_prep_fwd(z, cos, sin, gq, gkv, wuq, wukv):
    t = z.shape[0]
    tt = _row_tile(t)

    def body(z_ref, c_ref, s_ref, gq_ref, gkv_ref, wq_ref, wkv_ref, q_ref, k_ref, v_ref):
        qs, ks, vs = _mla_prep(z_ref[...], c_ref[...], s_ref[...], gq_ref[...], gkv_ref[...],
                               wq_ref[...].astype(F32), wkv_ref[...].astype(F32))
        for h in range(HEADS):
            q_ref[h] = qs[h].astype(BF16)
            k_ref[h] = ks[h].astype(BF16)
            v_ref[h] = vs[h].astype(BF16)

    hs = pl.BlockSpec((HEADS, tt, MLA_QK_PAD), lambda i: (0, i, 0))
    vsp = pl.BlockSpec((HEADS, tt, MLA_V), lambda i: (0, i, 0))
    return pl.pallas_call(
        body, grid=(t // tt,), in_specs=_mla_prep_specs(tt), out_specs=[hs, hs, vsp],
        out_shape=[jax.ShapeDtypeStruct((HEADS, t, MLA_QK_PAD), BF16), jax.ShapeDtypeStruct((HEADS, t, MLA_QK_PAD), BF16),
                   jax.ShapeDtypeStruct((HEADS, t, MLA_V), BF16)],
        name="mla_prep_fwd", compiler_params=_cparams(("parallel",)),
    )(z, cos, sin, gq, gkv, wuq, wukv)


def mla_prep_bwd(z, cos, sin, gq, gkv, wuq, wukv, dq, dk, dv):
    t = z.shape[0]
    tt = _row_tile(t)

    def body(z_ref, c_ref, s_ref, gq_ref, gkv_ref, wq_ref, wkv_ref, dq_ref, dk_ref, dv_ref,
             dz_ref, dgq_ref, dgkv_ref, dwq_ref, dwkv_ref):
        cos_v, sin_v = c_ref[...], s_ref[...]
        _, vjp = jax.vjp(lambda zm, a, b, wq, wkv: _mla_prep(zm, cos_v, sin_v, a, b, wq, wkv),
                         z_ref[...], gq_ref[...], gkv_ref[...], wq_ref[...].astype(F32), wkv_ref[...].astype(F32))
        cot = ([dq_ref[h] for h in range(HEADS)], [dk_ref[h] for h in range(HEADS)], [dv_ref[h] for h in range(HEADS)])
        dz, dgq, dgkv, dwq, dwkv = vjp(cot)
        dz_ref[...] = dz.astype(BF16)

        @pl.when(pl.program_id(0) == 0)
        def _():
            for r in (dgq_ref, dgkv_ref, dwq_ref, dwkv_ref):
                r[...] = jnp.zeros_like(r)

        dgq_ref[...] += dgq
        dgkv_ref[...] += dgkv
        dwq_ref[...] += dwq
        dwkv_ref[...] += dwkv

    specs = _mla_prep_specs(tt)
    hs = pl.BlockSpec((HEADS, tt, MLA_QK_PAD), lambda i: (0, i, 0))
    vsp = pl.BlockSpec((HEADS, tt, MLA_V), lambda i: (0, i, 0))
    nq, nkv = HEADS * (MLA_NOPE + MLA_ROPE), HEADS * (MLA_NOPE + MLA_V)
    return pl.pallas_call(
        body, grid=(t // tt,), in_specs=specs + [hs, hs, vsp],
        out_specs=[pl.BlockSpec((tt, WIDTH), lambda i: (i, 0)), specs[3], specs[4], specs[5], specs[6]],
        out_shape=[jax.ShapeDtypeStruct((t, WIDTH), BF16), jax.ShapeDtypeStruct((1, MLA_Q_RANK), F32),
                   jax.ShapeDtypeStruct((1, MLA_KV_RANK), F32), jax.ShapeDtypeStruct((MLA_Q_RANK, nq), F32),
                   jax.ShapeDtypeStruct((MLA_KV_RANK, nkv), F32)],
        name="mla_prep_bwd", compiler_params=_cparams(("arbitrary",)),
    )(z, cos, sin, gq, gkv, wuq, wukv, dq, dk, dv)


def _att_blocks(t):
    return min(ATT_BQ, t), min(ATT_BK, t)


def _key_blocks(a, bk):
    h, t, d = a.shape
    return a.reshape(h, t // bk, bk, d).transpose(0, 1, 3, 2)


def _att_positions(kb, qi, bq, bk, lo):
    krow = kb * bk + lax.broadcasted_iota(jnp.int32, (bk, bq - lo), 0)
    qcol = qi * bq + lo + lax.broadcasted_iota(jnp.int32, (bk, bq - lo), 1)
    return krow, qcol


def _tail(x, lo):
    return x if lo == 0 else x[:, lo:]


def _join(old, new_tail, lo):
    return new_tail if lo == 0 else jnp.concatenate([old[:, :lo], new_tail], axis=1)


def _walk(qi, nd, bk, step, carry, diagonal_first=False):
    n_full = qi * nd
    diagonal = [(n_full + d, d * bk) for d in range(nd)]
    if diagonal_first:
        for kb, lo in reversed(diagonal):
            carry = step(kb, carry, True, lo)
        return lax.fori_loop(0, n_full, lambda i, c: step(n_full - 1 - i, c, False, 0), carry)
    carry = lax.fori_loop(0, n_full, lambda kb, c: step(kb, c, False, 0), carry)
    for kb, lo in diagonal:
        carry = step(kb, carry, True, lo)
    return carry


def softmax_att_fwd(qt, k, vtb, scale):
    hh, dk, t = qt.shape
    dv = vtb.shape[2]
    bq, bk = _att_blocks(t)

    def body(qt_ref, k_ref, vt_ref, o_ref, lse_ref):
        qi = pl.program_id(1)
        qtv = qt_ref[0]

        def step(kb, carry, masked, lo):
            m_all, l_all, acc_all = carry
            m, l, acc = _tail(m_all, lo), _tail(l_all, lo), _tail(acc_all, lo)
            ks = pl.multiple_of(kb * bk, bk)
            s = jnp.dot(k_ref[0, pl.ds(ks, bk), :], _tail(qtv, lo), preferred_element_type=F32) * scale
            if masked:
                krow, qcol = _att_positions(kb, qi, bq, bk, lo)
                s = jnp.where(krow <= qcol, s, NEG_BIG)
            m_new = jnp.maximum(m, jnp.max(s, axis=0, keepdims=True))
            p = jnp.exp(s - m_new)
            alpha = jnp.exp(m - m_new)
            l = alpha * l + jnp.sum(p, axis=0, keepdims=True)
            acc = alpha * acc + jnp.dot(vt_ref[0, kb], p.astype(BF16), preferred_element_type=F32)
            return _join(m_all, m_new, lo), _join(l_all, l, lo), _join(acc_all, acc, lo)

        init = (jnp.full((1, bq), NEG_BIG, F32), jnp.zeros((1, bq), F32), jnp.zeros((dv, bq), F32))
        m, l, acc = _walk(qi, bq // bk, bk, step, init)
        o_ref[0] = acc / l
        lse_ref[0] = m + jnp.log(l)

    return pl.pallas_call(
        body, grid=(hh, t // bq),
        in_specs=[pl.BlockSpec((1, dk, bq), lambda h, i: (h, 0, i)), pl.BlockSpec((1, t, dk), lambda h, i: (h, 0, 0)),
                  pl.BlockSpec((1, t // bk, dv, bk), lambda h, i: (h, 0, 0, 0))],
        out_specs=[pl.BlockSpec((1, dv, bq), lambda h, i: (h, 0, i)), pl.BlockSpec((1, 1, bq), lambda h, i: (h, 0, i))],
        out_shape=[jax.ShapeDtypeStruct((hh, dv, t), F32), jax.ShapeDtypeStruct((hh, 1, t), F32)],
        name="softmax_att_fwd", compiler_params=_cparams(("parallel", "arbitrary")),
    )(qt, k, vtb)


def softmax_att_bwd(qt, q, k, ktb, v, ot, lse, do, dot_, scale):
    hh, dk, t = qt.shape
    dv = v.shape[2]
    bq, bk = _att_blocks(t)

    def body(qt_ref, q_ref, k_ref, kt_ref, v_ref, ot_ref, lse_ref, do_ref, dot_ref, dqt_ref, dk_ref, dv_ref):
        qi = pl.program_id(1)

        @pl.when(qi == 0)
        def _():
            dk_ref[...] = jnp.zeros_like(dk_ref)
            dv_ref[...] = jnp.zeros_like(dv_ref)

        qtv, qv, dov, dotv = qt_ref[0], q_ref[0], do_ref[0], dot_ref[0]
        lse_v = lse_ref[0]
        delta = jnp.sum(dotv.astype(F32) * ot_ref[0], axis=0, keepdims=True)

        def step(kb, dqt, masked, lo):
            ks = pl.multiple_of(kb * bk, bk)
            s = jnp.dot(k_ref[0, pl.ds(ks, bk), :], _tail(qtv, lo), preferred_element_type=F32) * scale
            if masked:
                krow, qcol = _att_positions(kb, qi, bq, bk, lo)
                s = jnp.where(krow <= qcol, s, NEG_BIG)
            p = jnp.exp(s - _tail(lse_v, lo))
            dp = jnp.dot(v_ref[0, pl.ds(ks, bk), :], _tail(dotv, lo), preferred_element_type=F32)
            ds = (p * (dp - _tail(delta, lo)) * scale).astype(BF16)
            dk_ref[0, pl.ds(ks, bk), :] += jnp.dot(ds, qv[lo:], preferred_element_type=F32)
            dv_ref[0, pl.ds(ks, bk), :] += jnp.dot(p.astype(BF16), dov[lo:], preferred_element_type=F32)
            return _join(dqt, _tail(dqt, lo) + jnp.dot(kt_ref[0, kb], ds, preferred_element_type=F32), lo)

        dqt_ref[0] = _walk(qi, bq // bk, bk, step, jnp.zeros((dk, bq), F32))

    nkb = t // bk
    qts = pl.BlockSpec((1, dk, bq), lambda h, i: (h, 0, i))
    qs = pl.BlockSpec((1, bq, dk), lambda h, i: (h, i, 0))
    kfull = pl.BlockSpec((1, t, dk), lambda h, i: (h, 0, 0))
    vfull = pl.BlockSpec((1, t, dv), lambda h, i: (h, 0, 0))
    vts = pl.BlockSpec((1, dv, bq), lambda h, i: (h, 0, i))
    return pl.pallas_call(
        body, grid=(hh, t // bq),
        in_specs=[qts, qs, kfull, pl.BlockSpec((1, nkb, dk, bk), lambda h, i: (h, 0, 0, 0)), vfull, vts,
                  pl.BlockSpec((1, 1, bq), lambda h, i: (h, 0, i)), pl.BlockSpec((1, bq, dv), lambda h, i: (h, i, 0)), vts],
        out_specs=[qts, kfull, vfull],
        out_shape=[jax.ShapeDtypeStruct((hh, dk, t), F32), jax.ShapeDtypeStruct((hh, t, dk), F32),
                   jax.ShapeDtypeStruct((hh, t, dv), F32)],
        name="softmax_att_bwd", compiler_params=_cparams(("parallel", "arbitrary")),
    )(qt, q, k, ktb, v, ot, lse, do, dot_)


SUM_ROWS = 8


def _softplus(z):
    return jnp.maximum(z, 0.0) + jnp.log(1.0 + jnp.exp(-jnp.abs(z)))


def _tri_ext(bk, kind):
    r = lax.broadcasted_iota(jnp.int32, (bk + SUM_ROWS, bk), 0)
    c = lax.broadcasted_iota(jnp.int32, (bk + SUM_ROWS, bk), 1)
    if kind == "neg_later":
        return jnp.where((c > r) | (r >= bk), -1.0, 0.0).astype(BF16)
    return jnp.where((c <= r) | (r >= bk), 1.0, 0.0).astype(BF16)


def _split_dot(m01, x):
    hi = lax.bitcast_convert_type(lax.bitcast_convert_type(x, jnp.uint32) & jnp.uint32(0xFFFF0000), F32)
    lo = x - hi
    return (jnp.dot(m01, hi.astype(BF16), preferred_element_type=F32) + jnp.dot(m01, lo.astype(BF16), preferred_element_type=F32))


def sb_cast(z):
    t = z.shape[0]
    tt = _row_tile(t)
    cols = 3 * WIDTH
    assert Z_SB % cols == 0

    def body(z_ref, o_ref):
        o_ref[...] = z_ref[...].astype(BF16)

    return pl.pallas_call(
        body, grid=(t // tt,), in_specs=[pl.BlockSpec((tt, cols), lambda i: (i, Z_SB // cols))],
        out_specs=pl.BlockSpec((tt, cols), lambda i: (i, 0)), out_shape=jax.ShapeDtypeStruct((t, cols), BF16),
        name="sb_cast", compiler_params=_cparams(("parallel",)),
    )(z)


def _first_last_step(nh, nq):
    h, i = pl.program_id(0), pl.program_id(1)
    return (h == 0) & (i == 0), (h == nh - 1) & (i == nq - 1)


def stick_att_fwd(qt, k, vtb, comm=None):
    hh, dk, t = qt.shape
    dv = vtb.shape[2]
    bq, bk = _att_blocks(t)
    nkb = t // bk
    srcs, gather, c_specs, c_shapes, c_sems = _with_exchange(comm)
    n = len(srcs)

    def body(*refs):
        qt_ref, k_ref, vt_ref = refs[:3]
        o_ref, rs_ref = refs[3 + n:5 + n]
        if n:
            start, wait = _exchange_ops(refs[3:3 + n], refs[5 + n:5 + 2 * n], *refs[5 + 2 * n:], gather)
            first, last = _first_last_step(hh, t // bq)
            pl.when(first)(start)
        qi = pl.program_id(1)
        qtv = qt_ref[0]
        neg_later = _tri_ext(bk, "neg_later")

        def step(kb, carry, masked, lo):
            r_all, acc_all = carry
            r, acc = _tail(r_all, lo), _tail(acc_all, lo)
            ks = pl.multiple_of(kb * bk, bk)
            z = jnp.dot(k_ref[0, pl.ds(ks, bk), :], _tail(qtv, lo), preferred_element_type=F32)
            sp = _softplus(z)
            la = z - sp
            if masked:
                krow, qcol = _att_positions(kb, qi, bq, bk, lo)
                mask = krow < qcol
                sp = jnp.where(mask, sp, 0.0)
            rs_ref[0, kb] = r_all
            sums = _split_dot(neg_later, sp)
            a = jnp.exp(la + sums[:bk])
            if masked:
                a = jnp.where(mask, a, 0.0)
            acc = acc + jnp.dot(vt_ref[0, kb], a.astype(BF16), preferred_element_type=F32) * jnp.exp(r)
            return _join(r_all, r + sums[bk:bk + 1], lo), _join(acc_all, acc, lo)

        init = (jnp.zeros((1, bq), F32), jnp.zeros((dv, bq), F32))
        o_ref[0] = _walk(qi, bq // bk, bk, step, init, diagonal_first=True)[1]
        if n:
            pl.when(last)(wait)

    outs = pl.pallas_call(
        body, grid=(hh, t // bq),
        in_specs=[pl.BlockSpec((1, dk, bq), lambda h, i: (h, 0, i)), pl.BlockSpec((1, t, dk), lambda h, i: (h, 0, 0)),
                  pl.BlockSpec((1, nkb, dv, bk), lambda h, i: (h, 0, 0, 0))] + c_specs,
        out_specs=[pl.BlockSpec((1, dv, bq), lambda h, i: (h, 0, i)),
                   pl.BlockSpec((1, nkb, 1, bq), lambda h, i: (h, 0, 0, i))] + c_specs,
        out_shape=[jax.ShapeDtypeStruct((hh, dv, t), F32), jax.ShapeDtypeStruct((hh, nkb, 1, t), F32)] + c_shapes,
        scratch_shapes=c_sems, name="stick_att_fwd_exchange" if n else "stick_att_fwd",
        compiler_params=pltpu.CompilerParams(dimension_semantics=("arbitrary", "arbitrary") if n else ("parallel", "arbitrary"),
                                             vmem_limit_bytes=VMEM_LIMIT_V7X, has_side_effects=bool(n)),
    )(qt, k, vtb, *srcs)
    return outs[0], outs[1], outs[2:]


def stick_att_bwd(qt, q, k, ktb, v, rs, do, dot_, scale, comm=None):
    hh, dk, t = qt.shape
    dv = v.shape[2]
    bq, bk = _att_blocks(t)
    nkb = t // bk
    srcs, gather, c_specs, c_shapes, c_sems = _with_exchange(comm)
    n = len(srcs)

    def body(*refs):
        qt_ref, q_ref, k_ref, kt_ref, v_ref, rs_ref, do_ref, dot_ref = refs[:8]
        dqt_ref, dk_ref, dv_ref = refs[8 + n:11 + n]
        if n:
            start, wait = _exchange_ops(refs[8:8 + n], refs[11 + n:11 + 2 * n], *refs[11 + 2 * n:], gather)
            first, last = _first_last_step(hh, t // bq)
            pl.when(first)(start)
        qi = pl.program_id(1)

        @pl.when(qi == 0)
        def _():
            dk_ref[...] = jnp.zeros_like(dk_ref)
            dv_ref[...] = jnp.zeros_like(dv_ref)

        qtv, qv, dov, dotv = qt_ref[0], q_ref[0], do_ref[0], dot_ref[0]
        neg_later, upto = _tri_ext(bk, "neg_later"), _tri_ext(bk, "upto")

        def step(kb, carry, masked, lo):
            pre_all, dqt_all = carry
            pre, dqt = _tail(pre_all, lo), _tail(dqt_all, lo)
            ks = pl.multiple_of(kb * bk, bk)
            z = jnp.dot(k_ref[0, pl.ds(ks, bk), :], _tail(qtv, lo), preferred_element_type=F32)
            sp = _softplus(z)
            la = z - sp
            if masked:
                krow, qcol = _att_positions(kb, qi, bq, bk, lo)
                mask = krow < qcol
                sp = jnp.where(mask, sp, 0.0)
            a = jnp.exp(la + _split_dot(neg_later, sp)[:bk] + _tail(rs_ref[0, kb], lo))
            if masked:
                a = jnp.where(mask, a, 0.0)
            beta = jnp.exp(la)
            g = a * jnp.dot(v_ref[0, pl.ds(ks, bk), :], _tail(dotv, lo), preferred_element_type=F32)
            sums = jnp.dot(upto, g.astype(BF16), preferred_element_type=F32)
            dz = g - beta * (pre + sums[:bk])
            if masked:
                dz = jnp.where(mask, dz, 0.0)
            dzb = dz.astype(BF16)
            dk_ref[0, pl.ds(ks, bk), :] += jnp.dot(dzb, qv[lo:], preferred_element_type=F32)
            dv_ref[0, pl.ds(ks, bk), :] += jnp.dot(a.astype(BF16), dov[lo:], preferred_element_type=F32)
            dqt = dqt + jnp.dot(kt_ref[0, kb], dzb, preferred_element_type=F32)
            return _join(pre_all, pre + sums[bk:bk + 1], lo), _join(dqt_all, dqt, lo)

        init = (jnp.zeros((1, bq), F32), jnp.zeros((dk, bq), F32))
        dqt_ref[0] = _walk(qi, bq // bk, bk, step, init)[1] * scale
        if n:
            pl.when(last)(wait)

    qts = pl.BlockSpec((1, dk, bq), lambda h, i: (h, 0, i))
    qs = pl.BlockSpec((1, bq, dk), lambda h, i: (h, i, 0))
    kfull = pl.BlockSpec((1, t, dk), lambda h, i: (h, 0, 0))
    vfull = pl.BlockSpec((1, t, dv), lambda h, i: (h, 0, 0))
    outs = pl.pallas_call(
        body, grid=(hh, t // bq),
        in_specs=[qts, qs, kfull, pl.BlockSpec((1, nkb, dk, bk), lambda h, i: (h, 0, 0, 0)), vfull,
                  pl.BlockSpec((1, nkb, 1, bq), lambda h, i: (h, 0, 0, i)), pl.BlockSpec((1, bq, dv), lambda h, i: (h, i, 0)),
                  pl.BlockSpec((1, dv, bq), lambda h, i: (h, 0, i))] + c_specs,
        out_specs=[qts, kfull, vfull] + c_specs,
        out_shape=[jax.ShapeDtypeStruct((hh, dk, t), F32), jax.ShapeDtypeStruct((hh, t, dk), F32),
                   jax.ShapeDtypeStruct((hh, t, dv), F32)] + c_shapes,
        scratch_shapes=c_sems, name="stick_att_bwd_exchange" if n else "stick_att_bwd",
        compiler_params=pltpu.CompilerParams(dimension_semantics=("arbitrary", "arbitrary") if n else ("parallel", "arbitrary"),
                                             vmem_limit_bytes=VMEM_LIMIT_V7X, has_side_effects=bool(n)),
    )(qt, q, k, ktb, v, rs, do, dot_, *srcs)
    return outs[0], outs[1], outs[2], outs[3:]


def mod_fwd(c_all, ada_w, ada_b_cols):
    nl, d, n = ada_w.shape

    def body(c_ref, w_ref, b_ref, o_ref):
        act = jax.nn.silu(c_ref[...])
        o_ref[0] = jnp.dot(act, w_ref[0], precision=lax.Precision.HIGHEST, preferred_element_type=F32) + b_ref[0]

    return pl.pallas_call(
        body, grid=(nl,),
        in_specs=[pl.BlockSpec((N_DEV, d), lambda l: (0, 0)), pl.BlockSpec((1, d, n), lambda l: (l, 0, 0)),
                  pl.BlockSpec((1, 1, n), lambda l: (l, 0, 0))],
        out_specs=pl.BlockSpec((1, N_DEV, n), lambda l: (l, 0, 0)), out_shape=jax.ShapeDtypeStruct((nl, N_DEV, n), F32),
        name="mod_fwd", compiler_params=_cparams(("parallel",)),
    )(c_all, ada_w, ada_b_cols)


def ada_w_grad(c_all_t, dmod_cols):
    d = c_all_t.shape[0]
    nl, _, n = dmod_cols.shape

    def body(c_ref, dm_ref, o_ref):
        act = jax.nn.silu(c_ref[...])
        dm = dm_ref[0]
        acc = act[:, 0:1] * dm[0:1, :]
        for e in range(1, N_DEV):
            acc = acc + act[:, e:e + 1] * dm[e:e + 1, :]
        o_ref[0] = acc

    return pl.pallas_call(
        body, grid=(nl,),
        in_specs=[pl.BlockSpec((d, N_DEV), lambda l: (0, 0)), pl.BlockSpec((1, N_DEV, n), lambda l: (l, 0, 0))],
        out_specs=pl.BlockSpec((1, d, n), lambda l: (l, 0, 0)), out_shape=jax.ShapeDtypeStruct((nl, d, n), F32),
        name="ada_w_grad", compiler_params=_cparams(("parallel",)),
    )(c_all_t, dmod_cols)


ADAM_BLOCK_BYTES = 6 * 1024 * 1024


def _rows2d(a, lead=0):
    return a.reshape(a.shape[:lead] + (-1, a.shape[-1])) if a.ndim > lead + 1 else a.reshape(a.shape[:lead] + (1, -1))


def adamw(slots, w, m, v, *, name):
    shape = w.shape
    s2 = _rows2d(slots, 1)
    w2, m2, v2 = _rows2d(w), _rows2d(m), _rows2d(v)
    ns, r, c = s2.shape
    per_row = c * (ns * s2.dtype.itemsize + 7 * 4)
    tr = r
    if r * per_row > ADAM_BLOCK_BYTES:
        tr = max(t for t in range(8, r, 8) if r % t == 0 and t * per_row <= ADAM_BLOCK_BYTES)

    def body(s_ref, w_ref, m_ref, v_ref, g_ref, d_ref, mo_ref, vo_ref):
        g = s_ref[0].astype(F32)
        for i in range(1, ns):
            g = g + s_ref[i].astype(F32)
        mn = ADAM_B1 * m_ref[...] + (1.0 - ADAM_B1) * g
        vn = ADAM_B2 * v_ref[...] + (1.0 - ADAM_B2) * jnp.square(g)
        m_hat = mn / (1.0 - ADAM_B1 ** ADAM_STEP)
        v_hat = vn / (1.0 - ADAM_B2 ** ADAM_STEP)
        g_ref[...] = g
        d_ref[...] = -ADAM_LR * (m_hat / (jnp.sqrt(v_hat) + ADAM_EPS) + ADAM_WD * w_ref[...])
        mo_ref[...] = mn
        vo_ref[...] = vn

    row = pl.BlockSpec((tr, c), lambda i: (i, 0))
    outs = pl.pallas_call(
        body, grid=(r // tr,), in_specs=[pl.BlockSpec((ns, tr, c), lambda i: (0, i, 0)), row, row, row],
        out_specs=[row] * 4, out_shape=[jax.ShapeDtypeStruct((r, c), F32)] * 4, name=name,
        compiler_params=_cparams(("parallel",)),
    )(s2, w2, m2, v2)
    return [o.reshape(shape) for o in outs]


def _exchange(srcs, *, gather, name):
    n = len(srcs)

    def body(*refs):
        start, wait = _exchange_ops(refs[:n], refs[n:2 * n], *refs[2 * n:], gather)
        start()
        wait()

    return pl.pallas_call(
        body, in_specs=[ANY_SPEC] * n, out_specs=[ANY_SPEC] * n, out_shape=_exchange_out_shapes(srcs, gather),
        scratch_shapes=_exchange_sems(n), name=name, compiler_params=pltpu.CompilerParams(has_side_effects=True),
    )(*srcs)


ANY_SPEC = pl.BlockSpec(memory_space=pl.ANY)


def _exchange_out_shapes(srcs, gather):
    return [jax.ShapeDtypeStruct((N_DEV,) + (tuple(s.shape) if gather else tuple(s.shape[1:])), s.dtype) for s in srcs]


def _exchange_sems(n):
    return [pltpu.SemaphoreType.DMA((N_DEV - 1, n)), pltpu.SemaphoreType.DMA((N_DEV - 1, n)), pltpu.SemaphoreType.DMA((n,))]


def _exchange_ops(src_refs, out_refs, send_sems, recv_sems, local_sems, gather):
    n = len(src_refs)
    mx, my, mc = lax.axis_index("x"), lax.axis_index("y"), lax.axis_index("c")
    me = 4 * mx + 2 * my + mc
    peers = []
    for k in range(1, N_DEV):
        px = 1 - mx if k & 4 else mx
        py = 1 - my if k & 2 else my
        pc = 1 - mc if k & 1 else mc
        peers.append(((px, py, pc), 4 * px + 2 * py + pc))

    def part(a, idx):
        return src_refs[a] if gather else src_refs[a].at[idx]

    def copy(k, a, slot):
        dev, pid = peers[k]
        return pltpu.make_async_remote_copy(src_ref=part(a, pid), dst_ref=out_refs[a].at[slot], send_sem=send_sems.at[k, a],
                                            recv_sem=recv_sems.at[k, a], device_id=dev, device_id_type=MESH)

    def local(a):
        return pltpu.make_async_copy(part(a, me), out_refs[a].at[me], local_sems.at[a])

    def start():
        for a in range(n):
            local(a).start()
            for k in range(N_DEV - 1):
                copy(k, a, me).start()

    def wait():
        for a in range(n):
            for k in range(N_DEV - 1):
                copy(k, a, me).wait_send()
                copy(k, a, peers[k][1]).wait_recv()
            local(a).wait()

    return start, wait


def _with_exchange(comm):
    srcs, gather = comm if comm is not None else ((), True)
    n = len(srcs)
    return (list(srcs), gather, [ANY_SPEC] * n, _exchange_out_shapes(srcs, gather) if n else [], _exchange_sems(n) if n else [])


def _to_blocks(full, ax):
    shp = full.shape
    g = full.reshape(shp[:ax] + (N_DEV, shp[ax] // N_DEV) + shp[ax + 1:])
    return jnp.moveaxis(g, ax, 0)


def _from_blocks(seg, ax):
    g = jnp.moveaxis(seg, 0, ax)
    shp = g.shape
    return g.reshape(shp[:ax] + (shp[ax] * shp[ax + 1],) + shp[ax + 2:])


def _to_heads(a, dh):
    return a.reshape(a.shape[0], -1, dh).transpose(1, 0, 2)


def _from_heads(a):
    return a.transpose(1, 0, 2).reshape(a.shape[1], -1)


def _from_heads_t(a):
    return a.transpose(2, 0, 1).reshape(a.shape[2], -1)


MLA_SCALE = (MLA_NOPE + MLA_ROPE) ** -0.5
SB_SCALE = SB_DIM ** -0.5
assert math.frexp(SB_SCALE)[0] == 0.5


def _resid_epi(acc, x, g):
    return x + g * acc, acc


def _layer_fwd(x, mod, p, cos, sin, comm=None, late=None):
    t = x.shape[0]
    bk = _att_blocks(t)[1]
    sh1, sc1, g1, sh2, sc2, g2 = [mod[i:i + 1] for i in range(6)]
    h = normmod_fwd(x, p["norm1_g"], sc1, sh1)
    z = matmul(h, p["w_in"], name="mm_in")
    y_sg = sg_fwd(z, p["sg_norm_g"], p["sg_w"], p["sg_bt"])
    y_cv = conv_fwd(z, p["conv_w"])
    mq, mk, mv = mla_prep_fwd(z, cos, sin, p["gq"], p["gkv"], p["wuq"], p["wukv"])
    mot, lse = softmax_att_fwd(mq.transpose(0, 2, 1), mk, _key_blocks(mv, bk), MLA_SCALE)
    y_mla = _from_heads_t(mot).astype(BF16)
    zsb = sb_cast(z)
    sq, sk, sv = [_to_heads(zsb[:, i * WIDTH:(i + 1) * WIDTH], SB_DIM) for i in range(3)]
    sq = sq * SB_SCALE
    sot, rs, got = stick_att_fwd(sq.transpose(0, 2, 1), sk, _key_blocks(sv, bk), comm)
    if late is not None:
        p = {**p, **late(got)}
    y_sb = _from_heads_t(sot).astype(BF16)
    ys = (y_sg, y_cv, y_mla, y_sb)
    up, merged = merge_fwd(ys, p["w_branch"], z)
    x1, out = matmul(merged, p["w_out"], name="mm_out", out_dtypes=(F32, F32), epi=_resid_epi, epi_in=((x, "mn"), (g1, "n")))
    h2 = normmod_fwd(x1, p["norm2_g"], sc2, sh2)
    a, r = matmul(h2, p["w1"], name="mm_up", out_dtypes=(F32, BF16),
                  epi=lambda acc: (acc, jnp.square(jnp.maximum(acc, 0.0))))
    x2, mo = matmul(r, p["w2"], name="mm_down", out_dtypes=(F32, F32), epi=_resid_epi, epi_in=((x1, "mn"), (g2, "n")))
    saved = dict(x=x, h=h, z=z, ys=ys, mq=mq, mk=mk, mv=mv, mot=mot, lse=lse, sq=sq, sk=sk, sv=sv, rs=rs, up=up,
                 merged=merged, out=out, x1=x1, h2=h2, a=a, r=r, mo=mo)
    return x2, saved, got, p


def _grad_blocks(g, name):
    return _to_blocks(g[name][None], dict(SHARDED)[name]).astype(BF16)


def _layer_bwd(dx2, s, mod, p, cos, sin, carried):
    t = dx2.shape[0]
    bk = _att_blocks(t)[1]
    sh1, sc1, g1, sh2, sc2, g2 = [mod[i:i + 1] for i in range(6)]
    g = {}
    dm, dg2 = resid_bwd(dx2, s["mo"], g2)
    da = matmul(dm, p["w2_t"], name="mm_down_dx", out_dtypes=(BF16,), epi=lambda acc, a: (acc * (2.0 * jnp.maximum(a, 0.0)),),
                epi_in=((s["a"], "mn"),))
    g["mlp_w2"] = matmul_tn(s["r"], dm, name="mm_down_dw")
    dh2 = matmul(da, p["w1_t"], name="mm_up_dx")
    g["mlp_w1"] = matmul_tn(s["h2"], da, name="mm_up_dw")
    dx1, g["norm2_g"], dsc2, dsh2 = normmod_bwd(s["x1"], p["norm2_g"], sc2, sh2, dh2, dx2)
    dout, dg1 = resid_bwd(dx1, s["out"], g1)
    dmerged = matmul(dout, p["w_out_t"], name="mm_out_dx")
    g["w_out"] = matmul_tn(s["merged"], dout, name="mm_out_dw")
    dzg, dup = merge_bwd(dmerged, s["up"], s["z"])
    dys = [matmul(dup[n], p["w_branch_t"][n], name="mm_branch_dx") for n in range(N_BRANCH)]
    g["w_branch"] = jnp.stack([matmul_tn(s["ys"][n], dup[n], name="mm_branch_dw") for n in range(N_BRANCH)])
    z = s["z"]
    dz_sg, g["sg_norm_g"], g["sg_w"], dbt = sg_bwd(z, p["sg_norm_g"], p["sg_w"], p["sg_bt"], dys[0])
    g["sg_b"] = dbt.T
    dz_cv, g["conv_w"] = conv_bwd(z, p["conv_w"], dys[1])
    do = _to_heads(dys[2], MLA_V).astype(BF16)
    dqt, dk, dv = softmax_att_bwd(s["mq"].transpose(0, 2, 1), s["mq"], s["mk"], _key_blocks(s["mk"], bk), s["mv"], s["mot"],
                                  s["lse"], do, do.transpose(0, 2, 1), MLA_SCALE)
    dz_mla, g["mla_q_norm_g"], g["mla_kv_norm_g"], g["mla_w_uq"], g["mla_w_ukv"] = mla_prep_bwd(
        z, cos, sin, p["gq"], p["gkv"], p["wuq"], p["wukv"], dqt.transpose(0, 2, 1), dk, dv)
    do = _to_heads(dys[3], SB_DIM).astype(BF16)
    comm = (list(carried) + [_grad_blocks(g, n) for n in OWN_GRADS], False)
    dqt, dk, dv, got = stick_att_bwd(s["sq"].transpose(0, 2, 1), s["sq"], s["sk"], _key_blocks(s["sk"], bk), s["sv"], s["rs"],
                                     do, do.transpose(0, 2, 1), SB_SCALE, comm)
    dz_sb = jnp.concatenate([_from_heads_t(dqt), _from_heads(dk), _from_heads(dv)], axis=1).astype(BF16)
    dz = jnp.concatenate([dz_sg, dz_cv, dz_mla, dz_sb, dzg], axis=1)
    dh = matmul(dz, p["w_in_t"], name="mm_in_dx", tk=2176)
    dw_in = matmul_tn(s["h"], dz, name="mm_in_dw")
    g["w_in"] = jnp.concatenate([dw_in[:, :Z_MLA + MLA_COLS], dw_in[:, Z_SB:]], axis=1)
    dx, g["norm1_g"], dsc1, dsh1 = normmod_bwd(s["x"], p["norm1_g"], sc1, sh1, dh, dx1)
    g["mla_q_norm_g"], g["mla_kv_norm_g"] = g["mla_q_norm_g"][0], g["mla_kv_norm_g"][0]
    g["norm1_g"], g["norm2_g"], g["sg_norm_g"] = g["norm1_g"][0], g["norm2_g"][0], g["sg_norm_g"][0]
    g["ada_b"] = jnp.concatenate([dsh1, dsc1, dg1, dsh2, dsc2, dg2], axis=1)[0]
    return dx, g, got[:len(carried)], got[len(carried):]


WEIGHT_NAMES = ("ada_w", "ada_b", "norm1_g", "norm2_g", "w_in", "sg_norm_g", "sg_w", "sg_b", "conv_w", "mla_q_norm_g",
                "mla_w_uq", "mla_kv_norm_g", "mla_w_ukv", "w_branch", "w_out", "mlp_w1", "mlp_w2", "final_norm_g")
SHARDED = (("w_in", 2), ("mla_w_uq", 2), ("mla_w_ukv", 2), ("w_branch", 3), ("w_out", 1), ("mlp_w1", 2), ("mlp_w2", 1),
           ("conv_w", 2))
EARLY_WEIGHTS = ("w_in", "mla_w_uq", "mla_w_ukv")
LATE_WEIGHTS = ("w_branch", "w_out", "mlp_w1", "mlp_w2")
OWN_GRADS = tuple(n for n, _ in SHARDED if n != "w_in")
REPLICATED = ("ada_b", "norm1_g", "norm2_g", "sg_norm_g", "sg_w", "sg_b", "mla_q_norm_g", "mla_kv_norm_g", "final_norm_g")


def kernel(x, c, positions, ada_w, ada_b, norm1_g, norm2_g, w_in, sg_norm_g, sg_w, sg_b, conv_w, mla_q_norm_g, mla_w_uq, mla_kv_norm_g, mla_w_ukv, w_branch, w_out, mlp_w1, mlp_w2, final_norm_g, loss_target, m_ada_w, m_ada_b, m_norm1_g, m_norm2_g, m_w_in, m_sg_norm_g, m_sg_w, m_sg_b, m_conv_w, m_mla_q_norm_g, m_mla_w_uq, m_mla_kv_norm_g, m_mla_w_ukv, m_w_branch, m_w_out, m_mlp_w1, m_mlp_w2, m_final_norm_g, v_ada_w, v_ada_b, v_norm1_g, v_norm2_g, v_w_in, v_sg_norm_g, v_sg_w, v_sg_b, v_conv_w, v_mla_q_norm_g, v_mla_w_uq, v_mla_kv_norm_g, v_mla_w_ukv, v_w_branch, v_w_out, v_mlp_w1, v_mlp_w2, v_final_norm_g):
    given = dict(locals())
    w = {n: given[n] for n in WEIGHT_NAMES}
    m = {n: given["m_" + n] for n in WEIGHT_NAMES}
    v = {n: given["v_" + n] for n in WEIGHT_NAMES}
    xs, tgt = x[0], loss_target[0]
    nl = ada_w.shape[0]
    me = 4 * lax.axis_index("x") + 2 * lax.axis_index("y") + lax.axis_index("c")

    inv_freq = ROPE_THETA ** (-jnp.arange(0, MLA_ROPE, 2, dtype=F32) / MLA_ROPE)
    ang = positions[0].astype(F32)[:, None] * inv_freq
    cos, sin = jnp.cos(ang), jnp.sin(ang)

    c_all, conv_blocks = _exchange([c, conv_w], gather=True, name="gather_c_conv")
    c_all = c_all[:, 0]
    conv_full = _from_blocks(conv_blocks, 2)
    ncol = ada_w.shape[2]
    ada_b_cols = lax.dynamic_slice_in_dim(ada_b, me * ncol, ncol, axis=1)[:, None, :]
    mod_cols = mod_fwd(c_all, ada_w, ada_b_cols)
    got = _exchange([mod_cols], gather=True, name="gather_mod")[0]
    mod = jnp.moveaxis(lax.dynamic_index_in_dim(got, me, axis=2, keepdims=False), 0, 1).reshape(nl, 6, D_MODEL)

    cut = Z_MLA + MLA_COLS
    axis_of = dict(SHARDED)

    def shards(names, lo, hi):
        return [w[n][lo:hi].astype(BF16) for n in names]

    def early_weights(names, got, l):
        full = {n: _from_blocks(seg, axis_of[n])[0] for n, seg in zip(names, got)}
        w_in_pad = jnp.concatenate([full["w_in"][:, :cut], jnp.zeros((D_MODEL, Z_SB - cut), BF16), full["w_in"][:, cut:]], axis=1)
        return dict(norm1_g=norm1_g[l][None], norm2_g=norm2_g[l][None], sg_norm_g=sg_norm_g[l][None], sg_w=sg_w[l],
                    sg_bt=sg_b[l].T, conv_w=conv_full[l], gq=mla_q_norm_g[l][None], gkv=mla_kv_norm_g[l][None],
                    wuq=full["mla_w_uq"], wukv=full["mla_w_ukv"], w_in=w_in_pad, w_in_t=w_in_pad.T)

    def late_weights(names, got):
        full = {n: _from_blocks(seg, axis_of[n])[0] for n, seg in zip(names, got)}
        return dict(w_branch=full["w_branch"], w_branch_t=full["w_branch"].transpose(0, 2, 1), w_out=full["w_out"],
                    w_out_t=full["w_out"].T, w1=full["mlp_w1"], w1_t=full["mlp_w1"].T, w2=full["mlp_w2"], w2_t=full["mlp_w2"].T)

    every = EARLY_WEIGHTS + LATE_WEIGHTS
    n_early, n_all = len(EARLY_WEIGHTS), len(every)
    layers = [None] * nl
    layers[0] = early_weights(EARLY_WEIGHTS, _exchange(shards(EARLY_WEIGHTS, 0, 1), gather=True, name="gather_weights_first"), 0)
    rides = {0: list(range(1, min(2, nl))), 1: list(range(2, nl))}

    saved = []
    xc = xs
    for l in range(nl):
        srcs = shards(LATE_WEIGHTS, 0, 1) if l == 0 else []
        for r in rides.get(l, []):
            srcs += shards(every, r, r + 1)
        late = (lambda got: late_weights(LATE_WEIGHTS, got[:len(LATE_WEIGHTS)])) if l == 0 else None
        xc, s, got, layers[l] = _layer_fwd(xc, mod[l], layers[l], cos, sin, (srcs, True) if srcs else None, late)
        saved.append(s)
        got = got[len(LATE_WEIGHTS):] if l == 0 else got
        for i, r in enumerate(rides.get(l, [])):
            part = got[i * n_all:(i + 1) * n_all]
            layers[r] = {**early_weights(EARLY_WEIGHTS, part[:n_early], r), **late_weights(LATE_WEIGHTS, part[n_early:])}
    loss_part, dx, dgf = loss_head(xc, final_norm_g[None], tgt)
    loss = lax.psum(loss_part[0, 0], AXES)

    grads = [None] * nl
    landed = {n: [None] * nl for n, _ in SHARDED}
    carried = []
    for l in reversed(range(nl)):
        dx, grads[l], got_carried, got_own = _layer_bwd(dx, saved[l], mod[l], layers[l], cos, sin, carried)
        if carried:
            landed["w_in"][l + 1] = got_carried[0]
        for n, arr in zip(OWN_GRADS, got_own):
            landed[n][l] = arr
        carried = [_grad_blocks(grads[l], "w_in")]
    landed["w_in"][0] = _exchange(carried, gather=False, name="exchange_grads_first")[0]
    gfull = {n: jnp.stack([grads[l][n] for l in range(nl)]) for n in REPLICATED if n != "final_norm_g"}
    gfull["final_norm_g"] = dgf[0]

    out = {}

    def update(n, slots):
        for kind, arr in zip(("grad", "delta", "new_m", "new_v"), adamw(slots, w[n], m[n], v[n], name="adamw_" + n)):
            out[kind, n] = arr

    for n, _ in SHARDED:
        update(n, jnp.concatenate(landed[n], axis=1))
    got = _exchange([gfull[n] for n in REPLICATED], gather=True, name="gather_small_grads")
    for n, slots in zip(REPLICATED, got):
        update(n, slots)
    dmod_cols = jnp.moveaxis(lax.dynamic_slice_in_dim(got[0], me * ncol, ncol, axis=2), 0, 1)
    update("ada_w", ada_w_grad(c_all.T, dmod_cols)[None])

    res = [loss, dx[None]]
    for kind in ("grad", "delta", "new_m", "new_v"):
        res += [out[kind, n] for n in WEIGHT_NAMES]
    return tuple(res)
```

```python
import functools
import math

import jax
import jax.numpy as jnp
from jax import lax
from jax.experimental import pallas as pl
from jax.experimental.pallas import tpu as pltpu

F32 = jnp.float32
BF16 = jnp.bfloat16

D_MODEL = 1024
WIDTH = 512
CHUNK = 128
SG_GROUPS = 4
HEADS = 8
MLA_NOPE = 64
MLA_ROPE = 32
MLA_V = 64
MLA_Q_RANK = 256
MLA_KV_RANK = 128
MLA_QK_PAD = 128
SB_DIM = 64
ROPE_THETA = 10000.0
NORM_EPS = 1e-6
N_BRANCH = 4
D_FF = 4096
Z_SG, Z_CONV, Z_MLA, Z_SB, Z_GATE, Z_COLS = 0, 1024, 2560, 3072, 4608, 8704
IN_COLS = 8608
MLA_COLS = MLA_Q_RANK + MLA_KV_RANK + MLA_ROPE

ADAM_LR, ADAM_B1, ADAM_B2, ADAM_EPS, ADAM_WD, ADAM_STEP = 0.001, 0.9, 0.999, 1e-08, 0.01, 10

N_DEV = 8
AXES = ("x", "y", "c")
MESH = pl.DeviceIdType.MESH
VMEM_LIMIT_V7X = 56 * 1024 * 1024
ATT_BQ = 2048
ATT_BK = 256
NEG_BIG = -1e30


def _cparams(sem=None):
    return pltpu.CompilerParams(dimension_semantics=sem, vmem_limit_bytes=VMEM_LIMIT_V7X)


def _div_tile(n, pref):
    if n <= pref:
        return n
    t = (pref // 128) * 128
    while t >= 128:
        if n % t == 0:
            return t
        t -= 128
    raise ValueError(f"no tile for {n}")


def _row_tile(t):
    return min(512, t)


def _dot(a, b, dims):
    return lax.dot_general(a.astype(BF16), b.astype(BF16), (dims, ((), ())), preferred_element_type=F32)


@jax.custom_vjp
def _mm(a, b):
    return _dot(a, b, ((1,), (0,)))


def _mm_fwd(a, b):
    return _mm(a, b), (a, b)


def _mm_bwd(res, g):
    a, b = res
    return _dot(g, b, ((1,), (1,))).astype(a.dtype), _dot(a, g, ((0,), (0,))).astype(b.dtype)


_mm.defvjp(_mm_fwd, _mm_bwd)


def _rms(x, g):
    return (x * lax.rsqrt(jnp.mean(x * x, axis=-1, keepdims=True) + NORM_EPS)) * g


def _normmod(x, g, sc, sh):
    return _rms(x, g) * (1.0 + sc) + sh


def normmod_fwd(x, g, sc, sh):
    t, d = x.shape
    tt = _row_tile(t)

    def body(x_ref, g_ref, sc_ref, sh_ref, h_ref):
        h_ref[...] = _normmod(x_ref[...], g_ref[...], sc_ref[...], sh_ref[...]).astype(BF16)

    row = pl.BlockSpec((tt, d), lambda i: (i, 0))
    vec = pl.BlockSpec((1, d), lambda i: (0, 0))
    return pl.pallas_call(
        body, grid=(t // tt,), in_specs=[row, vec, vec, vec], out_specs=row,
        out_shape=jax.ShapeDtypeStruct((t, d), BF16), name="normmod_fwd", compiler_params=_cparams(("parallel",)),
    )(x, g, sc, sh)


def normmod_bwd(x, g, sc, sh, dh, dres):
    t, d = x.shape
    tt = _row_tile(t)

    def body(x_ref, g_ref, sc_ref, sh_ref, dh_ref, dres_ref, dx_ref, dg_ref, dsc_ref, dsh_ref):
        _, vjp = jax.vjp(_normmod, x_ref[...], g_ref[...], sc_ref[...], sh_ref[...])
        dx, dg, dsc, dsh = vjp(dh_ref[...])
        dx_ref[...] = dx + dres_ref[...]

        @pl.when(pl.program_id(0) == 0)
        def _():
            dg_ref[...] = jnp.zeros_like(dg_ref)
            dsc_ref[...] = jnp.zeros_like(dsc_ref)
            dsh_ref[...] = jnp.zeros_like(dsh_ref)

        dg_ref[...] += dg
        dsc_ref[...] += dsc
        dsh_ref[...] += dsh

    row = pl.BlockSpec((tt, d), lambda i: (i, 0))
    vec = pl.BlockSpec((1, d), lambda i: (0, 0))
    vs = jax.ShapeDtypeStruct((1, d), F32)
    return pl.pallas_call(
        body, grid=(t // tt,), in_specs=[row, vec, vec, vec, row, row], out_specs=[row, vec, vec, vec],
        out_shape=[jax.ShapeDtypeStruct((t, d), F32), vs, vs, vs], name="normmod_bwd",
        compiler_params=_cparams(("arbitrary",)),
    )(x, g, sc, sh, dh, dres)


def resid_bwd(dxn, m, g):
    t, d = dxn.shape
    tt = _row_tile(t)

    def body(dx_ref, m_ref, g_ref, dm_ref, dg_ref):
        dx = dx_ref[...]
        dm_ref[...] = (dx * g_ref[...]).astype(BF16)

        @pl.when(pl.program_id(0) == 0)
        def _():
            dg_ref[...] = jnp.zeros_like(dg_ref)

        dg_ref[...] += jnp.sum(dx * m_ref[...], axis=0, keepdims=True)

    row = pl.BlockSpec((tt, d), lambda i: (i, 0))
    vec = pl.BlockSpec((1, d), lambda i: (0, 0))
    return pl.pallas_call(
        body, grid=(t // tt,), in_specs=[row, row, vec], out_specs=[row, vec],
        out_shape=[jax.ShapeDtypeStruct((t, d), BF16), jax.ShapeDtypeStruct((1, d), F32)], name="resid_bwd",
        compiler_params=_cparams(("arbitrary",)),
    )(dxn, m, g)


def loss_head(x, gf, tgt):
    t, d = x.shape
    tt = _row_tile(t)

    def f(xv, gv, tv):
        y = _rms(xv, gv)
        return 0.5 * jnp.sum(jnp.mean(jnp.square(y - tv), axis=-1))

    def body(x_ref, g_ref, t_ref, loss_ref, dx_ref, dg_ref):
        val, vjp = jax.vjp(lambda xv, gv: f(xv, gv, t_ref[...]), x_ref[...], g_ref[...])
        dx, dg = vjp(jnp.ones((), F32))
        dx_ref[...] = dx

        @pl.when(pl.program_id(0) == 0)
        def _():
            dg_ref[...] = jnp.zeros_like(dg_ref)
            loss_ref[...] = jnp.zeros_like(loss_ref)

        dg_ref[...] += dg
        loss_ref[...] += jnp.full(loss_ref.shape, val, F32)

    row = pl.BlockSpec((tt, d), lambda i: (i, 0))
    vec = pl.BlockSpec((1, d), lambda i: (0, 0))
    lsp = pl.BlockSpec((1, 128), lambda i: (0, 0))
    return pl.pallas_call(
        body, grid=(t // tt,), in_specs=[row, vec, row], out_specs=[lsp, row, vec],
        out_shape=[jax.ShapeDtypeStruct((1, 128), F32), jax.ShapeDtypeStruct((t, d), F32),
                   jax.ShapeDtypeStruct((1, d), F32)],
        name="loss_head", compiler_params=_cparams(("arbitrary",)),
    )(x, gf, tgt)


def matmul(a, b, *, name, out_dtypes=(F32,), epi=None, epi_in=(), tm=2048, tn=512, tk=1024):
    m, k = a.shape
    n = b.shape[1]
    tm, tn, tk = min(tm, m), _div_tile(n, tn), _div_tile(k, tk)
    nk = k // tk
    n_epi, n_out = len(epi_in), len(out_dtypes)

    def body(*refs):
        a_ref, b_ref = refs[:2]
        e_refs = refs[2:2 + n_epi]
        o_refs = refs[2 + n_epi:2 + n_epi + n_out]
        kk = pl.program_id(2)
        part = jnp.dot(a_ref[...].astype(BF16), b_ref[...].astype(BF16), preferred_element_type=F32)

        def finish(acc):
            outs = (acc,) if epi is None else epi(acc, *[r[...] for r in e_refs])
            for o_ref, o in zip(o_refs, outs):
                o_ref[...] = o.astype(o_ref.dtype)

        if nk == 1:
            finish(part)
            return
        acc_ref = refs[-1]

        @pl.when(kk == 0)
        def _():
            acc_ref[...] = part

        @pl.when((kk > 0) & (kk < nk - 1))
        def _():
            acc_ref[...] += part

        @pl.when(kk == nk - 1)
        def _():
            finish(acc_ref[...] + part)

    in_specs = [pl.BlockSpec((tm, tk), lambda i, j, kk: (i, kk)), pl.BlockSpec((tk, tn), lambda i, j, kk: (kk, j))]
    for _, kind in epi_in:
        in_specs.append(pl.BlockSpec((tm, tn), lambda i, j, kk: (i, j)) if kind == "mn"
                        else pl.BlockSpec((1, tn), lambda i, j, kk: (0, j)))
    out_spec = pl.BlockSpec((tm, tn), lambda i, j, kk: (i, j))
    outs = pl.pallas_call(
        body, grid=(m // tm, n // tn, nk), in_specs=in_specs, out_specs=[out_spec] * n_out,
        out_shape=[jax.ShapeDtypeStruct((m, n), dt) for dt in out_dtypes],
        scratch_shapes=[pltpu.VMEM((tm, tn), F32)] if nk > 1 else [], name=name,
        compiler_params=_cparams(("parallel", "parallel", "arbitrary")),
    )(a, b, *[arr for arr, _ in epi_in])
    return outs[0] if n_out == 1 else outs


def matmul_tn(a, b, *, name, tko=1024, tn=512, tt=2048):
    t, ko = a.shape
    n = b.shape[1]
    tko, tn, tt = _div_tile(ko, tko), _div_tile(n, tn), min(tt, t)
    nt = t // tt

    def body(a_ref, b_ref, o_ref):
        s = pl.program_id(2)
        part = _dot(a_ref[...], b_ref[...], ((0,), (0,)))

        @pl.when(s == 0)
        def _():
            o_ref[...] = part

        @pl.when(s > 0)
        def _():
            o_ref[...] += part

    return pl.pallas_call(
        body, grid=(ko // tko, n // tn, nt),
        in_specs=[pl.BlockSpec((tt, tko), lambda i, j, s: (s, i)), pl.BlockSpec((tt, tn), lambda i, j, s: (s, j))],
        out_specs=pl.BlockSpec((tko, tn), lambda i, j, s: (i, j)), out_shape=jax.ShapeDtypeStruct((ko, n), F32),
        name=name, compiler_params=_cparams(("parallel", "parallel", "arbitrary")),
    )(a, b)


def merge_fwd(ys, w_branch, z):
    t = ys[0].shape[0]
    d = D_MODEL
    tm, tn = min(1024, t), 512
    gate0 = Z_GATE // tn

    def body(y0, y1, y2, y3, w_ref, g0, g1, g2, g3, up_ref, mg_ref):
        acc = jnp.zeros((tm, tn), F32)
        for n, (y_ref, g_ref) in enumerate(zip((y0, y1, y2, y3), (g0, g1, g2, g3))):
            up = jnp.dot(y_ref[...], w_ref[n], preferred_element_type=F32)
            up_ref[n] = up.astype(BF16)
            acc = acc + jax.nn.sigmoid(g_ref[...]) * up
        mg_ref[...] = acc.astype(BF16)

    ysp = pl.BlockSpec((tm, WIDTH), lambda i, j: (i, 0))
    gate_specs = [pl.BlockSpec((tm, tn), functools.partial(lambda i, j, n: (i, gate0 + n * (d // tn) + j), n=n))
                  for n in range(N_BRANCH)]
    return pl.pallas_call(
        body, grid=(t // tm, d // tn),
        in_specs=[ysp, ysp, ysp, ysp, pl.BlockSpec((N_BRANCH, WIDTH, tn), lambda i, j: (0, 0, j))] + gate_specs,
        out_specs=[pl.BlockSpec((N_BRANCH, tm, tn), lambda i, j: (0, i, j)), pl.BlockSpec((tm, tn), lambda i, j: (i, j))],
        out_shape=[jax.ShapeDtypeStruct((N_BRANCH, t, d), BF16), jax.ShapeDtypeStruct((t, d), BF16)],
        name="merge_fwd", compiler_params=_cparams(("parallel", "parallel")),
    )(*ys, w_branch, z, z, z, z)


def merge_bwd(dmerged, up, z):
    t, d = dmerged.shape
    tm, tn = min(1024, t), 512
    gate0 = Z_GATE // tn
    nj = d // tn

    def body(dm_ref, up_ref, g_ref, dzg_ref, dup_ref):
        dm = dm_ref[...]
        s = jax.nn.sigmoid(g_ref[...])
        dzg_ref[...] = (dm * up_ref[0] * s * (1.0 - s)).astype(BF16)
        dup_ref[0] = (dm * s).astype(BF16)

    blk = pl.BlockSpec((1, tm, tn), lambda i, j, n: (n, i, j))
    return pl.pallas_call(
        body, grid=(t // tm, nj, N_BRANCH),
        in_specs=[pl.BlockSpec((tm, tn), lambda i, j, n: (i, j)), blk,
                  pl.BlockSpec((tm, tn), lambda i, j, n: (i, gate0 + n * nj + j))],
        out_specs=[pl.BlockSpec((tm, tn), lambda i, j, n: (i, n * nj + j)), blk],
        out_shape=[jax.ShapeDtypeStruct((t, N_BRANCH * d), BF16), jax.ShapeDtypeStruct((N_BRANCH, t, d), BF16)],
        name="merge_bwd", compiler_params=_cparams(("parallel", "parallel", "arbitrary")),
    )(dmerged, up, z)


def _sg_tile(zsg, g, w, bt):
    tt = zsg.shape[0]
    a = jax.nn.gelu(zsg)
    u, v = a[:, :WIDTH], a[:, WIDTH:]
    vc = v - jnp.mean(v, axis=-1, keepdims=True)
    v = (vc * lax.rsqrt(jnp.mean(vc * vc, axis=-1, keepdims=True) + NORM_EPS)) * g
    row = lax.broadcasted_iota(jnp.int32, (CHUNK, CHUNK), 0)
    col = lax.broadcasted_iota(jnp.int32, (CHUNK, CHUNK), 1)
    cols = []
    for gi in range(SG_GROUPS):
        wc = jnp.where(row >= col, w[gi], 0.0)
        bias = bt[:, gi:gi + 1]
        rows = [_mm(wc, v[ch * CHUNK:(ch + 1) * CHUNK, gi * CHUNK:(gi + 1) * CHUNK]) + bias for ch in range(tt // CHUNK)]
        cols.append(jnp.concatenate(rows, axis=0) if len(rows) > 1 else rows[0])
    return u * jnp.concatenate(cols, axis=1)


def sg_fwd(z, g, w, bt):
    t = z.shape[0]
    tt = _row_tile(t)

    def body(z_ref, g_ref, w_ref, b_ref, y_ref):
        y_ref[...] = _sg_tile(z_ref[...], g_ref[...], w_ref[...], b_ref[...]).astype(BF16)

    return pl.pallas_call(
        body, grid=(t // tt,),
        in_specs=[pl.BlockSpec((tt, 2 * WIDTH), lambda i: (i, 0)), pl.BlockSpec((1, WIDTH), lambda i: (0, 0)),
                  pl.BlockSpec((SG_GROUPS, CHUNK, CHUNK), lambda i: (0, 0, 0)), pl.BlockSpec((CHUNK, SG_GROUPS), lambda i: (0, 0))],
        out_specs=pl.BlockSpec((tt, WIDTH), lambda i: (i, 0)), out_shape=jax.ShapeDtypeStruct((t, WIDTH), BF16),
        name="sg_fwd", compiler_params=_cparams(("parallel",)),
    )(z, g, w, bt)


def sg_bwd(z, g, w, bt, dy):
    t = z.shape[0]
    tt = _row_tile(t)

    def body(z_ref, g_ref, w_ref, b_ref, dy_ref, dz_ref, dg_ref, dw_ref, db_ref):
        _, vjp = jax.vjp(_sg_tile, z_ref[...], g_ref[...], w_ref[...], b_ref[...])
        dz, dg, dw, db = vjp(dy_ref[...])
        dz_ref[...] = dz.astype(BF16)

        @pl.when(pl.program_id(0) == 0)
        def _():
            dg_ref[...] = jnp.zeros_like(dg_ref)
            dw_ref[...] = jnp.zeros_like(dw_ref)
            db_ref[...] = jnp.zeros_like(db_ref)

        dg_ref[...] += dg
        dw_ref[...] += dw
        db_ref[...] += db

    gs = pl.BlockSpec((1, WIDTH), lambda i: (0, 0))
    ws = pl.BlockSpec((SG_GROUPS, CHUNK, CHUNK), lambda i: (0, 0, 0))
    bs = pl.BlockSpec((CHUNK, SG_GROUPS), lambda i: (0, 0))
    return pl.pallas_call(
        body, grid=(t // tt,),
        in_specs=[pl.BlockSpec((tt, 2 * WIDTH), lambda i: (i, 0)), gs, ws, bs, pl.BlockSpec((tt, WIDTH), lambda i: (i, 0))],
        out_specs=[pl.BlockSpec((tt, 2 * WIDTH), lambda i: (i, 0)), gs, ws, bs],
        out_shape=[jax.ShapeDtypeStruct((t, 2 * WIDTH), BF16), jax.ShapeDtypeStruct((1, WIDTH), F32),
                   jax.ShapeDtypeStruct((SG_GROUPS, CHUNK, CHUNK), F32), jax.ShapeDtypeStruct((CHUNK, SG_GROUPS), F32)],
        name="sg_bwd", compiler_params=_cparams(("arbitrary",)),
    )(z, g, w, bt, dy)


HALO = 8


def _shift_down(halo, cur, k):
    tt = cur.shape[0]
    ext = jnp.concatenate([halo, cur], axis=0)
    return ext[HALO - k:HALO - k + tt]


def _shift_up(cur, halo, k):
    tt = cur.shape[0]
    ext = jnp.concatenate([cur, halo], axis=0)
    return ext[k:k + tt]


def conv_fwd(z, cw):
    t = z.shape[0]
    tt = _row_tile(t)
    cb = Z_CONV // WIDTH
    hb = tt // HALO

    def body(gb_ref, gc_ref, xv_ref, hgc_ref, hxv_ref, w_ref, y_ref):
        first = (pl.program_id(0) == 0)
        tcur = gc_ref[...] * xv_ref[...]
        thalo = jnp.where(first, 0.0, hgc_ref[...] * hxv_ref[...])
        w = w_ref[...]
        y = w[2:3] * tcur + w[1:2] * _shift_down(thalo, tcur, 1) + w[0:1] * _shift_down(thalo, tcur, 2)
        y_ref[...] = (gb_ref[...] * y).astype(BF16)

    def cur(off):
        return pl.BlockSpec((tt, WIDTH), lambda i: (i, cb + off))

    def prev(off):
        return pl.BlockSpec((HALO, WIDTH), lambda i: (jnp.maximum(i * hb - 1, 0), cb + off))

    return pl.pallas_call(
        body, grid=(t // tt,),
        in_specs=[cur(0), cur(1), cur(2), prev(1), prev(2), pl.BlockSpec((3, WIDTH), lambda i: (0, 0))],
        out_specs=pl.BlockSpec((tt, WIDTH), lambda i: (i, 0)), out_shape=jax.ShapeDtypeStruct((t, WIDTH), BF16),
        name="conv_fwd", compiler_params=_cparams(("parallel",)),
    )(z, z, z, z, z, cw)


def conv_bwd(z, cw, dy):
    t = z.shape[0]
    tt = _row_tile(t)
    nt = t // tt
    cb = Z_CONV // WIDTH
    hb = tt // HALO

    def body(gb_ref, gc_ref, xv_ref, hgc_ref, hxv_ref, ngb_ref, dy_ref, ndy_ref, w_ref, dz_ref, dw_ref):
        i = pl.program_id(0)
        gb, gc, xv = gb_ref[...], gc_ref[...], xv_ref[...]
        tcur = gc * xv
        thalo = jnp.where(i == 0, 0.0, hgc_ref[...] * hxv_ref[...])
        t1, t2 = _shift_down(thalo, tcur, 1), _shift_down(thalo, tcur, 2)
        w = w_ref[...]
        y = w[2:3] * tcur + w[1:2] * t1 + w[0:1] * t2
        dy = dy_ref[...]
        dyc = dy * gb
        dyn = jnp.where(i == nt - 1, 0.0, ndy_ref[...] * ngb_ref[...])
        dt = w[2:3] * dyc + w[1:2] * _shift_up(dyc, dyn, 1) + w[0:1] * _shift_up(dyc, dyn, 2)
        dz_ref[...] = jnp.concatenate([dy * y, dt * xv, dt * gc], axis=1).astype(BF16)

        @pl.when(i == 0)
        def _():
            dw_ref[...] = jnp.zeros_like(dw_ref)

        dw_ref[...] += jnp.concatenate([jnp.sum(dyc * t2, axis=0, keepdims=True), jnp.sum(dyc * t1, axis=0, keepdims=True),
                                        jnp.sum(dyc * tcur, axis=0, keepdims=True)], axis=0)

    def cur(off):
        return pl.BlockSpec((tt, WIDTH), lambda i: (i, cb + off))

    def prev(off):
        return pl.BlockSpec((HALO, WIDTH), lambda i: (jnp.maximum(i * hb - 1, 0), cb + off))

    last = t // HALO - 1
    nxt_z = pl.BlockSpec((HALO, WIDTH), lambda i: (jnp.minimum((i + 1) * hb, last), cb))
    nxt_dy = pl.BlockSpec((HALO, WIDTH), lambda i: (jnp.minimum((i + 1) * hb, last), 0))
    return pl.pallas_call(
        body, grid=(nt,),
        in_specs=[cur(0), cur(1), cur(2), prev(1), prev(2), nxt_z, pl.BlockSpec((tt, WIDTH), lambda i: (i, 0)), nxt_dy,
                  pl.BlockSpec((3, WIDTH), lambda i: (0, 0))],
        out_specs=[pl.BlockSpec((tt, 3 * WIDTH), lambda i: (i, 0)), pl.BlockSpec((3, WIDTH), lambda i: (0, 0))],
        out_shape=[jax.ShapeDtypeStruct((t, 3 * WIDTH), BF16), jax.ShapeDtypeStruct((3, WIDTH), F32)],
        name="conv_bwd", compiler_params=_cparams(("arbitrary",)),
    )(z, z, z, z, z, z, dy, dy, cw)


def _mla_prep(zm, cos, sin, gq, gkv, wuq, wukv):
    tt = zm.shape[0]
    cq, ckv, kpe = zm[:, :MLA_Q_RANK], zm[:, MLA_Q_RANK:MLA_Q_RANK + MLA_KV_RANK], zm[:, MLA_Q_RANK + MLA_KV_RANK:MLA_COLS]
    q = _mm(_rms(cq, gq), wuq)
    kv = _mm(_rms(ckv, gkv), wukv)
    half = MLA_ROPE // 2

    def rope(xr):
        x1, x2 = xr[:, :half], xr[:, half:]
        return jnp.concatenate([x1 * cos - x2 * sin, x2 * cos + x1 * sin], axis=-1)

    kpe = rope(kpe)
    pad = jnp.zeros((tt, MLA_QK_PAD - MLA_NOPE - MLA_ROPE), F32)
    dq, dkv = MLA_NOPE + MLA_ROPE, MLA_NOPE + MLA_V
    qs, ks, vs = [], [], []
    for h in range(HEADS):
        qs.append(jnp.concatenate([q[:, h * dq:h * dq + MLA_NOPE], rope(q[:, h * dq + MLA_NOPE:(h + 1) * dq]), pad], axis=-1))
        ks.append(jnp.concatenate([kv[:, h * dkv:h * dkv + MLA_NOPE], kpe, pad], axis=-1))
        vs.append(kv[:, h * dkv + MLA_NOPE:(h + 1) * dkv])
    return qs, ks, vs


def _mla_prep_specs(tt):
    zs = pl.BlockSpec((tt, WIDTH), lambda i: (i, Z_MLA // WIDTH))
    cs = pl.BlockSpec((tt, MLA_ROPE // 2), lambda i: (i, 0))
    return [zs, cs, cs, pl.BlockSpec((1, MLA_Q_RANK), lambda i: (0, 0)), pl.BlockSpec((1, MLA_KV_RANK), lambda i: (0, 0)),
            pl.BlockSpec((MLA_Q_RANK, HEADS * (MLA_NOPE + MLA_ROPE)), lambda i: (0, 0)),
            pl.BlockSpec((MLA_KV_RANK, HEADS * (MLA_NOPE + MLA_V)), lambda i: (0, 0))]


def mla_prep_fwd(z, cos, sin, gq, gkv, wuq, wukv):
    t = z.shape[0]
    tt = _row_tile(t)

    def body(z_ref, c_ref, s_ref, gq_ref, gkv_ref, wq_ref, wkv_ref, q_ref, k_ref, v_ref):
        qs, ks, vs = _mla_prep(z_ref[...], c_ref[...], s_ref[...], gq_ref[...], gkv_ref[...],
                               wq_ref[...].astype(F32), wkv_ref[...].astype(F32))
        for h in range(HEADS):
            q_ref[h] = qs[h].astype(BF16)
            k_ref[h] = ks[h].astype(BF16)
            v_ref[h] = vs[h].astype(BF16)

    hs = pl.BlockSpec((HEADS, tt, MLA_QK_PAD), lambda i: (0, i, 0))
    vsp = pl.BlockSpec((HEADS, tt, MLA_V), lambda i: (0, i, 0))
    return pl.pallas_call(
        body, grid=(t // tt,), in_specs=_mla_prep_specs(tt), out_specs=[hs, hs, vsp],
        out_shape=[jax.ShapeDtypeStruct((HEADS, t, MLA_QK_PAD), BF16), jax.ShapeDtypeStruct((HEADS, t, MLA_QK_PAD), BF16),
                   jax.ShapeDtypeStruct((HEADS, t, MLA_V), BF16)],
        name="mla_prep_fwd", compiler_params=_cparams(("parallel",)),
    )(z, cos, sin, gq, gkv, wuq, wukv)


def mla_prep_bwd(z, cos, sin, gq, gkv, wuq, wukv, dq, dk, dv):
    t = z.shape[0]
    tt = _row_tile(t)

    def body(z_ref, c_ref, s_ref, gq_ref, gkv_ref, wq_ref, wkv_ref, dq_ref, dk_ref, dv_ref,
             dz_ref, dgq_ref, dgkv_ref, dwq_ref, dwkv_ref):
        cos_v, sin_v = c_ref[...], s_ref[...]
        _, vjp = jax.vjp(lambda zm, a, b, wq, wkv: _mla_prep(zm, cos_v, sin_v, a, b, wq, wkv),
                         z_ref[...], gq_ref[...], gkv_ref[...], wq_ref[...].astype(F32), wkv_ref[...].astype(F32))
        cot = ([dq_ref[h] for h in range(HEADS)], [dk_ref[h] for h in range(HEADS)], [dv_ref[h] for h in range(HEADS)])
        dz, dgq, dgkv, dwq, dwkv = vjp(cot)
        dz_ref[...] = dz.astype(BF16)

        @pl.when(pl.program_id(0) == 0)
        def _():
            for r in (dgq_ref, dgkv_ref, dwq_ref, dwkv_ref):
                r[...] = jnp.zeros_like(r)

        dgq_ref[...] += dgq
        dgkv_ref[...] += dgkv
        dwq_ref[...] += dwq
        dwkv_ref[...] += dwkv

    specs = _mla_prep_specs(tt)
    hs = pl.BlockSpec((HEADS, tt, MLA_QK_PAD), lambda i: (0, i, 0))
    vsp = pl.BlockSpec((HEADS, tt, MLA_V), lambda i: (0, i, 0))
    nq, nkv = HEADS * (MLA_NOPE + MLA_ROPE), HEADS * (MLA_NOPE + MLA_V)
    return pl.pallas_call(
        body, grid=(t // tt,), in_specs=specs + [hs, hs, vsp],
        out_specs=[pl.BlockSpec((tt, WIDTH), lambda i: (i, 0)), specs[3], specs[4], specs[5], specs[6]],
        out_shape=[jax.ShapeDtypeStruct((t, WIDTH), BF16), jax.ShapeDtypeStruct((1, MLA_Q_RANK), F32),
                   jax.ShapeDtypeStruct((1, MLA_KV_RANK), F32), jax.ShapeDtypeStruct((MLA_Q_RANK, nq), F32),
                   jax.ShapeDtypeStruct((MLA_KV_RANK, nkv), F32)],
        name="mla_prep_bwd", compiler_params=_cparams(("arbitrary",)),
    )(z, cos, sin, gq, gkv, wuq, wukv, dq, dk, dv)


def _att_blocks(t):
    return min(ATT_BQ, t), min(ATT_BK, t)


def _key_blocks(a, bk):
    h, t, d = a.shape
    return a.reshape(h, t // bk, bk, d).transpose(0, 1, 3, 2)


def _att_positions(kb, qi, bq, bk, lo):
    krow = kb * bk + lax.broadcasted_iota(jnp.int32, (bk, bq - lo), 0)
    qcol = qi * bq + lo + lax.broadcasted_iota(jnp.int32, (bk, bq - lo), 1)
    return krow, qcol


def _tail(x, lo):
    return x if lo == 0 else x[:, lo:]


def _join(old, new_tail, lo):
    return new_tail if lo == 0 else jnp.concatenate([old[:, :lo], new_tail], axis=1)


def _walk(qi, nd, bk, step, carry, diagonal_first=False):
    n_full = qi * nd
    diagonal = [(n_full + d, d * bk) for d in range(nd)]
    if diagonal_first:
        for kb, lo in reversed(diagonal):
            carry = step(kb, carry, True, lo)
        return lax.fori_loop(0, n_full, lambda i, c: step(n_full - 1 - i, c, False, 0), carry)
    carry = lax.fori_loop(0, n_full, lambda kb, c: step(kb, c, False, 0), carry)
    for kb, lo in diagonal:
        carry = step(kb, carry, True, lo)
    return carry


def softmax_att_fwd(qt, k, vtb, scale):
    hh, dk, t = qt.shape
    dv = vtb.shape[2]
    bq, bk = _att_blocks(t)

    def body(qt_ref, k_ref, vt_ref, o_ref, lse_ref):
        qi = pl.program_id(1)
        qtv = qt_ref[0]

        def step(kb, carry, masked, lo):
            m_all, l_all, acc_all = carry
            m, l, acc = _tail(m_all, lo), _tail(l_all, lo), _tail(acc_all, lo)
            ks = pl.multiple_of(kb * bk, bk)
            s = jnp.dot(k_ref[0, pl.ds(ks, bk), :], _tail(qtv, lo), preferred_element_type=F32) * scale
            if masked:
                krow, qcol = _att_positions(kb, qi, bq, bk, lo)
                s = jnp.where(krow <= qcol, s, NEG_BIG)
            m_new = jnp.maximum(m, jnp.max(s, axis=0, keepdims=True))
            p = jnp.exp(s - m_new)
            alpha = jnp.exp(m - m_new)
            l = alpha * l + jnp.sum(p, axis=0, keepdims=True)
            acc = alpha * acc + jnp.dot(vt_ref[0, kb], p.astype(BF16), preferred_element_type=F32)
            return _join(m_all, m_new, lo), _join(l_all, l, lo), _join(acc_all, acc, lo)

        init = (jnp.full((1, bq), NEG_BIG, F32), jnp.zeros((1, bq), F32), jnp.zeros((dv, bq), F32))
        m, l, acc = _walk(qi, bq // bk, bk, step, init)
        o_ref[0] = acc / l
        lse_ref[0] = m + jnp.log(l)

    return pl.pallas_call(
        body, grid=(hh, t // bq),
        in_specs=[pl.BlockSpec((1, dk, bq), lambda h, i: (h, 0, i)), pl.BlockSpec((1, t, dk), lambda h, i: (h, 0, 0)),
                  pl.BlockSpec((1, t // bk, dv, bk), lambda h, i: (h, 0, 0, 0))],
        out_specs=[pl.BlockSpec((1, dv, bq), lambda h, i: (h, 0, i)), pl.BlockSpec((1, 1, bq), lambda h, i: (h, 0, i))],
        out_shape=[jax.ShapeDtypeStruct((hh, dv, t), F32), jax.ShapeDtypeStruct((hh, 1, t), F32)],
        name="softmax_att_fwd", compiler_params=_cparams(("parallel", "arbitrary")),
    )(qt, k, vtb)


def softmax_att_bwd(qt, q, k, ktb, v, ot, lse, do, dot_, scale):
    hh, dk, t = qt.shape
    dv = v.shape[2]
    bq, bk = _att_blocks(t)

    def body(qt_ref, q_ref, k_ref, kt_ref, v_ref, ot_ref, lse_ref, do_ref, dot_ref, dqt_ref, dk_ref, dv_ref):
        qi = pl.program_id(1)

        @pl.when(qi == 0)
        def _():
            dk_ref[...] = jnp.zeros_like(dk_ref)
            dv_ref[...] = jnp.zeros_like(dv_ref)

        qtv, qv, dov, dotv = qt_ref[0], q_ref[0], do_ref[0], dot_ref[0]
        lse_v = lse_ref[0]
        delta = jnp.sum(dotv.astype(F32) * ot_ref[0], axis=0, keepdims=True)

        def step(kb, dqt, masked, lo):
            ks = pl.multiple_of(kb * bk, bk)
            s = jnp.dot(k_ref[0, pl.ds(ks, bk), :], _tail(qtv, lo), preferred_element_type=F32) * scale
            if masked:
                krow, qcol = _att_positions(kb, qi, bq, bk, lo)
                s = jnp.where(krow <= qcol, s, NEG_BIG)
            p = jnp.exp(s - _tail(lse_v, lo))
            dp = jnp.dot(v_ref[0, pl.ds(ks, bk), :], _tail(dotv, lo), preferred_element_type=F32)
            ds = (p * (dp - _tail(delta, lo)) * scale).astype(BF16)
            dk_ref[0, pl.ds(ks, bk), :] += jnp.dot(ds, qv[lo:], preferred_element_type=F32)
            dv_ref[0, pl.ds(ks, bk), :] += jnp.dot(p.astype(BF16), dov[lo:], preferred_element_type=F32)
            return _join(dqt, _tail(dqt, lo) + jnp.dot(kt_ref[0, kb], ds, preferred_element_type=F32), lo)

        dqt_ref[0] = _walk(qi, bq // bk, bk, step, jnp.zeros((dk, bq), F32))

    nkb = t // bk
    qts = pl.BlockSpec((1, dk, bq), lambda h, i: (h, 0, i))
    qs = pl.BlockSpec((1, bq, dk), lambda h, i: (h, i, 0))
    kfull = pl.BlockSpec((1, t, dk), lambda h, i: (h, 0, 0))
    vfull = pl.BlockSpec((1, t, dv), lambda h, i: (h, 0, 0))
    vts = pl.BlockSpec((1, dv, bq), lambda h, i: (h, 0, i))
    return pl.pallas_call(
        body, grid=(hh, t // bq),
        in_specs=[qts, qs, kfull, pl.BlockSpec((1, nkb, dk, bk), lambda h, i: (h, 0, 0, 0)), vfull, vts,
                  pl.BlockSpec((1, 1, bq), lambda h, i: (h, 0, i)), pl.BlockSpec((1, bq, dv), lambda h, i: (h, i, 0)), vts],
        out_specs=[qts, kfull, vfull],
        out_shape=[jax.ShapeDtypeStruct((hh, dk, t), F32), jax.ShapeDtypeStruct((hh, t, dk), F32),
                   jax.ShapeDtypeStruct((hh, t, dv), F32)],
        name="softmax_att_bwd", compiler_params=_cparams(("parallel", "arbitrary")),
    )(qt, q, k, ktb, v, ot, lse, do, dot_)


SUM_ROWS = 8


def _softplus(z):
    return jnp.maximum(z, 0.0) + jnp.log(1.0 + jnp.exp(-jnp.abs(z)))


def _tri_ext(bk, kind):
    r = lax.broadcasted_iota(jnp.int32, (bk + SUM_ROWS, bk), 0)
    c = lax.broadcasted_iota(jnp.int32, (bk + SUM_ROWS, bk), 1)
    if kind == "neg_later":
        return jnp.where((c > r) | (r >= bk), -1.0, 0.0).astype(BF16)
    return jnp.where((c <= r) | (r >= bk), 1.0, 0.0).astype(BF16)


def _split_dot(m01, x):
    hi = lax.bitcast_convert_type(lax.bitcast_convert_type(x, jnp.uint32) & jnp.uint32(0xFFFF0000), F32)
    lo = x - hi
    return (jnp.dot(m01, hi.astype(BF16), preferred_element_type=F32) + jnp.dot(m01, lo.astype(BF16), preferred_element_type=F32))


def sb_cast(z):
    t = z.shape[0]
    tt = _row_tile(t)
    cols = 3 * WIDTH
    assert Z_SB % cols == 0

    def body(z_ref, o_ref):
        o_ref[...] = z_ref[...].astype(BF16)

    return pl.pallas_call(
        body, grid=(t // tt,), in_specs=[pl.BlockSpec((tt, cols), lambda i: (i, Z_SB // cols))],
        out_specs=pl.BlockSpec((tt, cols), lambda i: (i, 0)), out_shape=jax.ShapeDtypeStruct((t, cols), BF16),
        name="sb_cast", compiler_params=_cparams(("parallel",)),
    )(z)


def _first_last_step(nh, nq):
    h, i = pl.program_id(0), pl.program_id(1)
    return (h == 0) & (i == 0), (h == nh - 1) & (i == nq - 1)


def stick_att_fwd(qt, k, vtb, comm=None):
    hh, dk, t = qt.shape
    dv = vtb.shape[2]
    bq, bk = _att_blocks(t)
    nkb = t // bk
    srcs, gather, c_specs, c_shapes, c_sems = _with_exchange(comm)
    n = len(srcs)

    def body(*refs):
        qt_ref, k_ref, vt_ref = refs[:3]
        o_ref, rs_ref = refs[3 + n:5 + n]
        if n:
            start, wait = _exchange_ops(refs[3:3 + n], refs[5 + n:5 + 2 * n], *refs[5 + 2 * n:], gather)
            first, last = _first_last_step(hh, t // bq)
            pl.when(first)(start)
        qi = pl.program_id(1)
        qtv = qt_ref[0]
        neg_later = _tri_ext(bk, "neg_later")

        def step(kb, carry, masked, lo):
            r_all, acc_all = carry
            r, acc = _tail(r_all, lo), _tail(acc_all, lo)
            ks = pl.multiple_of(kb * bk, bk)
            z = jnp.dot(k_ref[0, pl.ds(ks, bk), :], _tail(qtv, lo), preferred_element_type=F32)
            sp = _softplus(z)
            la = z - sp
            if masked:
                krow, qcol = _att_positions(kb, qi, bq, bk, lo)
                mask = krow < qcol
                sp = jnp.where(mask, sp, 0.0)
            rs_ref[0, kb] = r_all
            sums = _split_dot(neg_later, sp)
            a = jnp.exp(la + sums[:bk])
            if masked:
                a = jnp.where(mask, a, 0.0)
            acc = acc + jnp.dot(vt_ref[0, kb], a.astype(BF16), preferred_element_type=F32) * jnp.exp(r)
            return _join(r_all, r + sums[bk:bk + 1], lo), _join(acc_all, acc, lo)

        init = (jnp.zeros((1, bq), F32), jnp.zeros((dv, bq), F32))
        o_ref[0] = _walk(qi, bq // bk, bk, step, init, diagonal_first=True)[1]
        if n:
            pl.when(last)(wait)

    outs = pl.pallas_call(
        body, grid=(hh, t // bq),
        in_specs=[pl.BlockSpec((1, dk, bq), lambda h, i: (h, 0, i)), pl.BlockSpec((1, t, dk), lambda h, i: (h, 0, 0)),
                  pl.BlockSpec((1, nkb, dv, bk), lambda h, i: (h, 0, 0, 0))] + c_specs,
        out_specs=[pl.BlockSpec((1, dv, bq), lambda h, i: (h, 0, i)),
                   pl.BlockSpec((1, nkb, 1, bq), lambda h, i: (h, 0, 0, i))] + c_specs,
        out_shape=[jax.ShapeDtypeStruct((hh, dv, t), F32), jax.ShapeDtypeStruct((hh, nkb, 1, t), F32)] + c_shapes,
        scratch_shapes=c_sems, name="stick_att_fwd_exchange" if n else "stick_att_fwd",
        compiler_params=pltpu.CompilerParams(dimension_semantics=("arbitrary", "arbitrary") if n else ("parallel", "arbitrary"),
                                             vmem_limit_bytes=VMEM_LIMIT_V7X, has_side_effects=bool(n)),
    )(qt, k, vtb, *srcs)
    return outs[0], outs[1], outs[2:]


def stick_att_bwd(qt, q, k, ktb, v, rs, do, dot_, scale, comm=None):
    hh, dk, t = qt.shape
    dv = v.shape[2]
    bq, bk = _att_blocks(t)
    nkb = t // bk
    srcs, gather, c_specs, c_shapes, c_sems = _with_exchange(comm)
    n = len(srcs)

    def body(*refs):
        qt_ref, q_ref, k_ref, kt_ref, v_ref, rs_ref, do_ref, dot_ref = refs[:8]
        dqt_ref, dk_ref, dv_ref = refs[8 + n:11 + n]
        if n:
            start, wait = _exchange_ops(refs[8:8 + n], refs[11 + n:11 + 2 * n], *refs[11 + 2 * n:], gather)
            first, last = _first_last_step(hh, t // bq)
            pl.when(first)(start)
        qi = pl.program_id(1)

        @pl.when(qi == 0)
        def _():
            dk_ref[...] = jnp.zeros_like(dk_ref)
            dv_ref[...] = jnp.zeros_like(dv_ref)

        qtv, qv, dov, dotv = qt_ref[0], q_ref[0], do_ref[0], dot_ref[0]
        neg_later, upto = _tri_ext(bk, "neg_later"), _tri_ext(bk, "upto")

        def step(kb, carry, masked, lo):
            pre_all, dqt_all = carry
            pre, dqt = _tail(pre_all, lo), _tail(dqt_all, lo)
            ks = pl.multiple_of(kb * bk, bk)
            z = jnp.dot(k_ref[0, pl.ds(ks, bk), :], _tail(qtv, lo), preferred_element_type=F32)
            sp = _softplus(z)
            la = z - sp
            if masked:
                krow, qcol = _att_positions(kb, qi, bq, bk, lo)
                mask = krow < qcol
                sp = jnp.where(mask, sp, 0.0)
            a = jnp.exp(la + _split_dot(neg_later, sp)[:bk] + _tail(rs_ref[0, kb], lo))
            if masked:
                a = jnp.where(mask, a, 0.0)
            beta = jnp.exp(la)
            g = a * jnp.dot(v_ref[0, pl.ds(ks, bk), :], _tail(dotv, lo), preferred_element_type=F32)
            sums = jnp.dot(upto, g.astype(BF16), preferred_element_type=F32)
            dz = g - beta * (pre + sums[:bk])
            if masked:
                dz = jnp.where(mask, dz, 0.0)
            dzb = dz.astype(BF16)
            dk_ref[0, pl.ds(ks, bk), :] += jnp.dot(dzb, qv[lo:], preferred_element_type=F32)
            dv_ref[0, pl.ds(ks, bk), :] += jnp.dot(a.astype(BF16), dov[lo:], preferred_element_type=F32)
            dqt = dqt + jnp.dot(kt_ref[0, kb], dzb, preferred_element_type=F32)
            return _join(pre_all, pre + sums[bk:bk + 1], lo), _join(dqt_all, dqt, lo)

        init = (jnp.zeros((1, bq), F32), jnp.zeros((dk, bq), F32))
        dqt_ref[0] = _walk(qi, bq // bk, bk, step, init)[1] * scale
        if n:
            pl.when(last)(wait)

    qts = pl.BlockSpec((1, dk, bq), lambda h, i: (h, 0, i))
    qs = pl.BlockSpec((1, bq, dk), lambda h, i: (h, i, 0))
    kfull = pl.BlockSpec((1, t, dk), lambda h, i: (h, 0, 0))
    vfull = pl.BlockSpec((1, t, dv), lambda h, i: (h, 0, 0))
    outs = pl.pallas_call(
        body, grid=(hh, t // bq),
        in_specs=[qts, qs, kfull, pl.BlockSpec((1, nkb, dk, bk), lambda h, i: (h, 0, 0, 0)), vfull,
                  pl.BlockSpec((1, nkb, 1, bq), lambda h, i: (h, 0, 0, i)), pl.BlockSpec((1, bq, dv), lambda h, i: (h, i, 0)),
                  pl.BlockSpec((1, dv, bq), lambda h, i: (h, 0, i))] + c_specs,
        out_specs=[qts, kfull, vfull] + c_specs,
        out_shape=[jax.ShapeDtypeStruct((hh, dk, t), F32), jax.ShapeDtypeStruct((hh, t, dk), F32),
                   jax.ShapeDtypeStruct((hh, t, dv), F32)] + c_shapes,
        scratch_shapes=c_sems, name="stick_att_bwd_exchange" if n else "stick_att_bwd",
        compiler_params=pltpu.CompilerParams(dimension_semantics=("arbitrary", "arbitrary") if n else ("parallel", "arbitrary"),
                                             vmem_limit_bytes=VMEM_LIMIT_V7X, has_side_effects=bool(n)),
    )(qt, q, k, ktb, v, rs, do, dot_, *srcs)
    return outs[0], outs[1], outs[2], outs[3:]


def mod_fwd(c_all, ada_w, ada_b_cols):
    nl, d, n = ada_w.shape

    def body(c_ref, w_ref, b_ref, o_ref):
        act = jax.nn.silu(c_ref[...])
        o_ref[0] = jnp.dot(act, w_ref[0], precision=lax.Precision.HIGHEST, preferred_element_type=F32) + b_ref[0]

    return pl.pallas_call(
        body, grid=(nl,),
        in_specs=[pl.BlockSpec((N_DEV, d), lambda l: (0, 0)), pl.BlockSpec((1, d, n), lambda l: (l, 0, 0)),
                  pl.BlockSpec((1, 1, n), lambda l: (l, 0, 0))],
        out_specs=pl.BlockSpec((1, N_DEV, n), lambda l: (l, 0, 0)), out_shape=jax.ShapeDtypeStruct((nl, N_DEV, n), F32),
        name="mod_fwd", compiler_params=_cparams(("parallel",)),
    )(c_all, ada_w, ada_b_cols)


def ada_w_grad(c_all_t, dmod_cols):
    d = c_all_t.shape[0]
    nl, _, n = dmod_cols.shape

    def body(c_ref, dm_ref, o_ref):
        act = jax.nn.silu(c_ref[...])
        dm = dm_ref[0]
        acc = act[:, 0:1] * dm[0:1, :]
        for e in range(1, N_DEV):
            acc = acc + act[:, e:e + 1] * dm[e:e + 1, :]
        o_ref[0] = acc

    return pl.pallas_call(
        body, grid=(nl,),
        in_specs=[pl.BlockSpec((d, N_DEV), lambda l: (0, 0)), pl.BlockSpec((1, N_DEV, n), lambda l: (l, 0, 0))],
        out_specs=pl.BlockSpec((1, d, n), lambda l: (l, 0, 0)), out_shape=jax.ShapeDtypeStruct((nl, d, n), F32),
        name="ada_w_grad", compiler_params=_cparams(("parallel",)),
    )(c_all_t, dmod_cols)


ADAM_BLOCK_BYTES = 6 * 1024 * 1024


def _rows2d(a, lead=0):
    return a.reshape(a.shape[:lead] + (-1, a.shape[-1])) if a.ndim > lead + 1 else a.reshape(a.shape[:lead] + (1, -1))


def adamw(slots, w, m, v, *, name):
    shape = w.shape
    s2 = _rows2d(slots, 1)
    w2, m2, v2 = _rows2d(w), _rows2d(m), _rows2d(v)
    ns, r, c = s2.shape
    per_row = c * (ns * s2.dtype.itemsize + 7 * 4)
    tr = r
    if r * per_row > ADAM_BLOCK_BYTES:
        tr = max(t for t in range(8, r, 8) if r % t == 0 and t * per_row <= ADAM_BLOCK_BYTES)

    def body(s_ref, w_ref, m_ref, v_ref, g_ref, d_ref, mo_ref, vo_ref):
        g = s_ref[0].astype(F32)
        for i in range(1, ns):
            g = g + s_ref[i].astype(F32)
        mn = ADAM_B1 * m_ref[...] + (1.0 - ADAM_B1) * g
        vn = ADAM_B2 * v_ref[...] + (1.0 - ADAM_B2) * jnp.square(g)
        m_hat = mn / (1.0 - ADAM_B1 ** ADAM_STEP)
        v_hat = vn / (1.0 - ADAM_B2 ** ADAM_STEP)
        g_ref[...] = g
        d_ref[...] = -ADAM_LR * (m_hat / (jnp.sqrt(v_hat) + ADAM_EPS) + ADAM_WD * w_ref[...])
        mo_ref[...] = mn
        vo_ref[...] = vn

    row = pl.BlockSpec((tr, c), lambda i: (i, 0))
    outs = pl.pallas_call(
        body, grid=(r // tr,), in_specs=[pl.BlockSpec((ns, tr, c), lambda i: (0, i, 0)), row, row, row],
        out_specs=[row] * 4, out_shape=[jax.ShapeDtypeStruct((r, c), F32)] * 4, name=name,
        compiler_params=_cparams(("parallel",)),
    )(s2, w2, m2, v2)
    return [o.reshape(shape) for o in outs]


def _exchange(srcs, *, gather, name):
    n = len(srcs)

    def body(*refs):
        start, wait = _exchange_ops(refs[:n], refs[n:2 * n], *refs[2 * n:], gather)
        start()
        wait()

    return pl.pallas_call(
        body, in_specs=[ANY_SPEC] * n, out_specs=[ANY_SPEC] * n, out_shape=_exchange_out_shapes(srcs, gather),
        scratch_shapes=_exchange_sems(n), name=name, compiler_params=pltpu.CompilerParams(has_side_effects=True),
    )(*srcs)


ANY_SPEC = pl.BlockSpec(memory_space=pl.ANY)


def _exchange_out_shapes(srcs, gather):
    return [jax.ShapeDtypeStruct((N_DEV,) + (tuple(s.shape) if gather else tuple(s.shape[1:])), s.dtype) for s in srcs]


def _exchange_sems(n):
    return [pltpu.SemaphoreType.DMA((N_DEV - 1, n)), pltpu.SemaphoreType.DMA((N_DEV - 1, n)), pltpu.SemaphoreType.DMA((n,))]


def _exchange_ops(src_refs, out_refs, send_sems, recv_sems, local_sems, gather):
    n = len(src_refs)
    mx, my, mc = lax.axis_index("x"), lax.axis_index("y"), lax.axis_index("c")
    me = 4 * mx + 2 * my + mc
    peers = []
    for k in range(1, N_DEV):
        px = 1 - mx if k & 4 else mx
        py = 1 - my if k & 2 else my
        pc = 1 - mc if k & 1 else mc
        peers.append(((px, py, pc), 4 * px + 2 * py + pc))

    def part(a, idx):
        return src_refs[a] if gather else src_refs[a].at[idx]

    def copy(k, a, slot):
        dev, pid = peers[k]
        return pltpu.make_async_remote_copy(src_ref=part(a, pid), dst_ref=out_refs[a].at[slot], send_sem=send_sems.at[k, a],
                                            recv_sem=recv_sems.at[k, a], device_id=dev, device_id_type=MESH)

    def local(a):
        return pltpu.make_async_copy(part(a, me), out_refs[a].at[me], local_sems.at[a])

    def start():
        for a in range(n):
            local(a).start()
            for k in range(N_DEV - 1):
                copy(k, a, me).start()

    def wait():
        for a in range(n):
            for k in range(N_DEV - 1):
                copy(k, a, me).wait_send()
                copy(k, a, peers[k][1]).wait_recv()
            local(a).wait()

    return start, wait


def _with_exchange(comm):
    srcs, gather = comm if comm is not None else ((), True)
    n = len(srcs)
    return (list(srcs), gather, [ANY_SPEC] * n, _exchange_out_shapes(srcs, gather) if n else [], _exchange_sems(n) if n else [])


def _to_blocks(full, ax):
    shp = full.shape
    g = full.reshape(shp[:ax] + (N_DEV, shp[ax] // N_DEV) + shp[ax + 1:])
    return jnp.moveaxis(g, ax, 0)


def _from_blocks(seg, ax):
    g = jnp.moveaxis(seg, 0, ax)
    shp = g.shape
    return g.reshape(shp[:ax] + (shp[ax] * shp[ax + 1],) + shp[ax + 2:])


def _to_heads(a, dh):
    return a.reshape(a.shape[0], -1, dh).transpose(1, 0, 2)


def _from_heads(a):
    return a.transpose(1, 0, 2).reshape(a.shape[1], -1)


def _from_heads_t(a):
    return a.transpose(2, 0, 1).reshape(a.shape[2], -1)


MLA_SCALE = (MLA_NOPE + MLA_ROPE) ** -0.5
SB_SCALE = SB_DIM ** -0.5
assert math.frexp(SB_SCALE)[0] == 0.5


def _resid_epi(acc, x, g):
    return x + g * acc, acc


def _layer_fwd(x, mod, p, cos, sin, comm=None, late=None):
    t = x.shape[0]
    bk = _att_blocks(t)[1]
    sh1, sc1, g1, sh2, sc2, g2 = [mod[i:i + 1] for i in range(6)]
    h = normmod_fwd(x, p["norm1_g"], sc1, sh1)
    z = matmul(h, p["w_in"], name="mm_in")
    y_sg = sg_fwd(z, p["sg_norm_g"], p["sg_w"], p["sg_bt"])
    y_cv = conv_fwd(z, p["conv_w"])
    mq, mk, mv = mla_prep_fwd(z, cos, sin, p["gq"], p["gkv"], p["wuq"], p["wukv"])
    mot, lse = softmax_att_fwd(mq.transpose(0, 2, 1), mk, _key_blocks(mv, bk), MLA_SCALE)
    y_mla = _from_heads_t(mot).astype(BF16)
    zsb = sb_cast(z)
    sq, sk, sv = [_to_heads(zsb[:, i * WIDTH:(i + 1) * WIDTH], SB_DIM) for i in range(3)]
    sq = sq * SB_SCALE
    sot, rs, got = stick_att_fwd(sq.transpose(0, 2, 1), sk, _key_blocks(sv, bk), comm)
    if late is not None:
        p = {**p, **late(got)}
    y_sb = _from_heads_t(sot).astype(BF16)
    ys = (y_sg, y_cv, y_mla, y_sb)
    up, merged = merge_fwd(ys, p["w_branch"], z)
    x1, out = matmul(merged, p["w_out"], name="mm_out", out_dtypes=(F32, F32), epi=_resid_epi, epi_in=((x, "mn"), (g1, "n")))
    h2 = normmod_fwd(x1, p["norm2_g"], sc2, sh2)
    a, r = matmul(h2, p["w1"], name="mm_up", out_dtypes=(F32, BF16),
                  epi=lambda acc: (acc, jnp.square(jnp.maximum(acc, 0.0))))
    x2, mo = matmul(r, p["w2"], name="mm_down", out_dtypes=(F32, F32), epi=_resid_epi, epi_in=((x1, "mn"), (g2, "n")))
    saved = dict(x=x, h=h, z=z, ys=ys, mq=mq, mk=mk, mv=mv, mot=mot, lse=lse, sq=sq, sk=sk, sv=sv, rs=rs, up=up,
                 merged=merged, out=out, x1=x1, h2=h2, a=a, r=r, mo=mo)
    return x2, saved, got, p


def _grad_blocks(g, name):
    return _to_blocks(g[name][None], dict(SHARDED)[name]).astype(BF16)


def _layer_bwd(dx2, s, mod, p, cos, sin, carried):
    t = dx2.shape[0]
    bk = _att_blocks(t)[1]
    sh1, sc1, g1, sh2, sc2, g2 = [mod[i:i + 1] for i in range(6)]
    g = {}
    dm, dg2 = resid_bwd(dx2, s["mo"], g2)
    da = matmul(dm, p["w2_t"], name="mm_down_dx", out_dtypes=(BF16,), epi=lambda acc, a: (acc * (2.0 * jnp.maximum(a, 0.0)),),
                epi_in=((s["a"], "mn"),))
    g["mlp_w2"] = matmul_tn(s["r"], dm, name="mm_down_dw")
    dh2 = matmul(da, p["w1_t"], name="mm_up_dx")
    g["mlp_w1"] = matmul_tn(s["h2"], da, name="mm_up_dw")
    dx1, g["norm2_g"], dsc2, dsh2 = normmod_bwd(s["x1"], p["norm2_g"], sc2, sh2, dh2, dx2)
    dout, dg1 = resid_bwd(dx1, s["out"], g1)
    dmerged = matmul(dout, p["w_out_t"], name="mm_out_dx")
    g["w_out"] = matmul_tn(s["merged"], dout, name="mm_out_dw")
    dzg, dup = merge_bwd(dmerged, s["up"], s["z"])
    dys = [matmul(dup[n], p["w_branch_t"][n], name="mm_branch_dx") for n in range(N_BRANCH)]
    g["w_branch"] = jnp.stack([matmul_tn(s["ys"][n], dup[n], name="mm_branch_dw") for n in range(N_BRANCH)])
    z = s["z"]
    dz_sg, g["sg_norm_g"], g["sg_w"], dbt = sg_bwd(z, p["sg_norm_g"], p["sg_w"], p["sg_bt"], dys[0])
    g["sg_b"] = dbt.T
    dz_cv, g["conv_w"] = conv_bwd(z, p["conv_w"], dys[1])
    do = _to_heads(dys[2], MLA_V).astype(BF16)
    dqt, dk, dv = softmax_att_bwd(s["mq"].transpose(0, 2, 1), s["mq"], s["mk"], _key_blocks(s["mk"], bk), s["mv"], s["mot"],
                                  s["lse"], do, do.transpose(0, 2, 1), MLA_SCALE)
    dz_mla, g["mla_q_norm_g"], g["mla_kv_norm_g"], g["mla_w_uq"], g["mla_w_ukv"] = mla_prep_bwd(
        z, cos, sin, p["gq"], p["gkv"], p["wuq"], p["wukv"], dqt.transpose(0, 2, 1), dk, dv)
    do = _to_heads(dys[3], SB_DIM).astype(BF16)
    comm = (list(carried) + [_grad_blocks(g, n) for n in OWN_GRADS], False)
    dqt, dk, dv, got = stick_att_bwd(s["sq"].transpose(0, 2, 1), s["sq"], s["sk"], _key_blocks(s["sk"], bk), s["sv"], s["rs"],
                                     do, do.transpose(0, 2, 1), SB_SCALE, comm)
    dz_sb = jnp.concatenate([_from_heads_t(dqt), _from_heads(dk), _from_heads(dv)], axis=1).astype(BF16)
    dz = jnp.concatenate([dz_sg, dz_cv, dz_mla, dz_sb, dzg], axis=1)
    dh = matmul(dz, p["w_in_t"], name="mm_in_dx", tk=2176)
    dw_in = matmul_tn(s["h"], dz, name="mm_in_dw")
    g["w_in"] = jnp.concatenate([dw_in[:, :Z_MLA + MLA_COLS], dw_in[:, Z_SB:]], axis=1)
    dx, g["norm1_g"], dsc1, dsh1 = normmod_bwd(s["x"], p["norm1_g"], sc1, sh1, dh, dx1)
    g["mla_q_norm_g"], g["mla_kv_norm_g"] = g["mla_q_norm_g"][0], g["mla_kv_norm_g"][0]
    g["norm1_g"], g["norm2_g"], g["sg_norm_g"] = g["norm1_g"][0], g["norm2_g"][0], g["sg_norm_g"][0]
    g["ada_b"] = jnp.concatenate([dsh1, dsc1, dg1, dsh2, dsc2, dg2], axis=1)[0]
    return dx, g, got[:len(carried)], got[len(carried):]


WEIGHT_NAMES = ("ada_w", "ada_b", "norm1_g", "norm2_g", "w_in", "sg_norm_g", "sg_w", "sg_b", "conv_w", "mla_q_norm_g",
                "mla_w_uq", "mla_kv_norm_g", "mla_w_ukv", "w_branch", "w_out", "mlp_w1", "mlp_w2", "final_norm_g")
SHARDED = (("w_in", 2), ("mla_w_uq", 2), ("mla_w_ukv", 2), ("w_branch", 3), ("w_out", 1), ("mlp_w1", 2), ("mlp_w2", 1),
           ("conv_w", 2))
EARLY_WEIGHTS = ("w_in", "mla_w_uq", "mla_w_ukv")
LATE_WEIGHTS = ("w_branch", "w_out", "mlp_w1", "mlp_w2")
OWN_GRADS = tuple(n for n, _ in SHARDED if n != "w_in")
REPLICATED = ("ada_b", "norm1_g", "norm2_g", "sg_norm_g", "sg_w", "sg_b", "mla_q_norm_g", "mla_kv_norm_g", "final_norm_g")


def kernel(x, c, positions, ada_w, ada_b, norm1_g, norm2_g, w_in, sg_norm_g, sg_w, sg_b, conv_w, mla_q_norm_g, mla_w_uq, mla_kv_norm_g, mla_w_ukv, w_branch, w_out, mlp_w1, mlp_w2, final_norm_g, loss_target, m_ada_w, m_ada_b, m_norm1_g, m_norm2_g, m_w_in, m_sg_norm_g, m_sg_w, m_sg_b, m_conv_w, m_mla_q_norm_g, m_mla_w_uq, m_mla_kv_norm_g, m_mla_w_ukv, m_w_branch, m_w_out, m_mlp_w1, m_mlp_w2, m_final_norm_g, v_ada_w, v_ada_b, v_norm1_g, v_norm2_g, v_w_in, v_sg_norm_g, v_sg_w, v_sg_b, v_conv_w, v_mla_q_norm_g, v_mla_w_uq, v_mla_kv_norm_g, v_mla_w_ukv, v_w_branch, v_w_out, v_mlp_w1, v_mlp_w2, v_final_norm_g):
    given = dict(locals())
    w = {n: given[n] for n in WEIGHT_NAMES}
    m = {n: given["m_" + n] for n in WEIGHT_NAMES}
    v = {n: given["v_" + n] for n in WEIGHT_NAMES}
    xs, tgt = x[0], loss_target[0]
    nl = ada_w.shape[0]
    me = 4 * lax.axis_index("x") + 2 * lax.axis_index("y") + lax.axis_index("c")

    inv_freq = ROPE_THETA ** (-jnp.arange(0, MLA_ROPE, 2, dtype=F32) / MLA_ROPE)
    ang = positions[0].astype(F32)[:, None] * inv_freq
    cos, sin = jnp.cos(ang), jnp.sin(ang)

    c_all, conv_blocks = _exchange([c, conv_w], gather=True, name="gather_c_conv")
    c_all = c_all[:, 0]
    conv_full = _from_blocks(conv_blocks, 2)
    ncol = ada_w.shape[2]
    ada_b_cols = lax.dynamic_slice_in_dim(ada_b, me * ncol, ncol, axis=1)[:, None, :]
    mod_cols = mod_fwd(c_all, ada_w, ada_b_cols)
    got = _exchange([mod_cols], gather=True, name="gather_mod")[0]
    mod = jnp.moveaxis(lax.dynamic_index_in_dim(got, me, axis=2, keepdims=False), 0, 1).reshape(nl, 6, D_MODEL)

    cut = Z_MLA + MLA_COLS
    axis_of = dict(SHARDED)

    def shards(names, lo, hi):
        return [w[n][lo:hi].astype(BF16) for n in names]

    def early_weights(names, got, l):
        full = {n: _from_blocks(seg, axis_of[n])[0] for n, seg in zip(names, got)}
        w_in_pad = jnp.concatenate([full["w_in"][:, :cut], jnp.zeros((D_MODEL, Z_SB - cut), BF16), full["w_in"][:, cut:]], axis=1)
        return dict(norm1_g=norm1_g[l][None], norm2_g=norm2_g[l][None], sg_norm_g=sg_norm_g[l][None], sg_w=sg_w[l],
                    sg_bt=sg_b[l].T, conv_w=conv_full[l], gq=mla_q_norm_g[l][None], gkv=mla_kv_norm_g[l][None],
                    wuq=full["mla_w_uq"], wukv=full["mla_w_ukv"], w_in=w_in_pad, w_in_t=w_in_pad.T)

    def late_weights(names, got):
        full = {n: _from_blocks(seg, axis_of[n])[0] for n, seg in zip(names, got)}
        return dict(w_branch=full["w_branch"], w_branch_t=full["w_branch"].transpose(0, 2, 1), w_out=full["w_out"],
                    w_out_t=full["w_out"].T, w1=full["mlp_w1"], w1_t=full["mlp_w1"].T, w2=full["mlp_w2"], w2_t=full["mlp_w2"].T)

    every = EARLY_WEIGHTS + LATE_WEIGHTS
    n_early, n_all = len(EARLY_WEIGHTS), len(every)
    layers = [None] * nl
    layers[0] = early_weights(EARLY_WEIGHTS, _exchange(shards(EARLY_WEIGHTS, 0, 1), gather=True, name="gather_weights_first"), 0)
    rides = {0: list(range(1, min(2, nl))), 1: list(range(2, nl))}

    saved = []
    xc = xs
    for l in range(nl):
        srcs = shards(LATE_WEIGHTS, 0, 1) if l == 0 else []
        for r in rides.get(l, []):
            srcs += shards(every, r, r + 1)
        late = (lambda got: late_weights(LATE_WEIGHTS, got[:len(LATE_WEIGHTS)])) if l == 0 else None
        xc, s, got, layers[l] = _layer_fwd(xc, mod[l], layers[l], cos, sin, (srcs, True) if srcs else None, late)
        saved.append(s)
        got = got[len(LATE_WEIGHTS):] if l == 0 else got
        for i, r in enumerate(rides.get(l, [])):
            part = got[i * n_all:(i + 1) * n_all]
            layers[r] = {**early_weights(EARLY_WEIGHTS, part[:n_early], r), **late_weights(LATE_WEIGHTS, part[n_early:])}
    loss_part, dx, dgf = loss_head(xc, final_norm_g[None], tgt)
    loss = lax.psum(loss_part[0, 0], AXES)

    grads = [None] * nl
    landed = {n: [None] * nl for n, _ in SHARDED}
    carried = []
    for l in reversed(range(nl)):
        dx, grads[l], got_carried, got_own = _layer_bwd(dx, saved[l], mod[l], layers[l], cos, sin, carried)
        if carried:
            landed["w_in"][l + 1] = got_carried[0]
        for n, arr in zip(OWN_GRADS, got_own):
            landed[n][l] = arr
        carried = [_grad_blocks(grads[l], "w_in")]
    landed["w_in"][0] = _exchange(carried, gather=False, name="exchange_grads_first")[0]
    gfull = {n: jnp.stack([grads[l][n] for l in range(nl)]) for n in REPLICATED if n != "final_norm_g"}
    gfull["final_norm_g"] = dgf[0]

    out = {}

    def update(n, slots):
        for kind, arr in zip(("grad", "delta", "new_m", "new_v"), adamw(slots, w[n], m[n], v[n], name="adamw_" + n)):
            out[kind, n] = arr

    for n, _ in SHARDED:
        update(n, jnp.concatenate(landed[n], axis=1))
    got = _exchange([gfull[n] for n in REPLICATED], gather=True, name="gather_small_grads")
    for n, slots in zip(REPLICATED, got):
        update(n, slots)
    dmod_cols = jnp.moveaxis(lax.dynamic_slice_in_dim(got[0], me * ncol, ncol, axis=2), 0, 1)
    update("ada_w", ada_w_grad(c_all.T, dmod_cols)[None])

    res = [loss, dx[None]]
    for kind in ("grad", "delta", "new_m", "new_v"):
        res += [out[kind, n] for n in WEIGHT_NAMES]
    return tuple(res)
```

```python
import functools
import math

import jax
import jax.numpy as jnp
from jax import lax
from jax.experimental import pallas as pl
from jax.experimental.pallas import tpu as pltpu

F32 = jnp.float32
BF16 = jnp.bfloat16

D_MODEL = 1024
WIDTH = 512
CHUNK = 128
SG_GROUPS = 4
HEADS = 8
MLA_NOPE = 64
MLA_ROPE = 32
MLA_V = 64
MLA_Q_RANK = 256
MLA_KV_RANK = 128
MLA_QK_PAD = 128
SB_DIM = 64
ROPE_THETA = 10000.0
NORM_EPS = 1e-6
N_BRANCH = 4
D_FF = 4096
Z_SG, Z_CONV, Z_MLA, Z_SB, Z_GATE, Z_COLS = 0, 1024, 2560, 3072, 4608, 8704
IN_COLS = 8608
MLA_COLS = MLA_Q_RANK + MLA_KV_RANK + MLA_ROPE

ADAM_LR, ADAM_B1, ADAM_B2, ADAM_EPS, ADAM_WD, ADAM_STEP = 0.001, 0.9, 0.999, 1e-08, 0.01, 10

N_DEV = 8
AXES = ("x", "y", "c")
MESH = pl.DeviceIdType.MESH
VMEM_LIMIT_V7X = 56 * 1024 * 1024
ATT_BQ = 2048
ATT_BK = 256
NEG_BIG = -1e30


def _cparams(sem=None):
    return pltpu.CompilerParams(dimension_semantics=sem, vmem_limit_bytes=VMEM_LIMIT_V7X)


def _div_tile(n, pref):
    if n <= pref:
        return n
    t = (pref // 128) * 128
    while t >= 128:
        if n % t == 0:
            return t
        t -= 128
    raise ValueError(f"no tile for {n}")


def _row_tile(t):
    return min(512, t)


def _dot(a, b, dims):
    return lax.dot_general(a.astype(BF16), b.astype(BF16), (dims, ((), ())), preferred_element_type=F32)


@jax.custom_vjp
def _mm(a, b):
    return _dot(a, b, ((1,), (0,)))


def _mm_fwd(a, b):
    return _mm(a, b), (a, b)


def _mm_bwd(res, g):
    a, b = res
    return _dot(g, b, ((1,), (1,))).astype(a.dtype), _dot(a, g, ((0,), (0,))).astype(b.dtype)


_mm.defvjp(_mm_fwd, _mm_bwd)


def _rms(x, g):
    return (x * lax.rsqrt(jnp.mean(x * x, axis=-1, keepdims=True) + NORM_EPS)) * g


def _normmod(x, g, sc, sh):
    return _rms(x, g) * (1.0 + sc) + sh


def normmod_fwd(x, g, sc, sh):
    t, d = x.shape
    tt = _row_tile(t)

    def body(x_ref, g_ref, sc_ref, sh_ref, h_ref):
        h_ref[...] = _normmod(x_ref[...], g_ref[...], sc_ref[...], sh_ref[...]).astype(BF16)

    row = pl.BlockSpec((tt, d), lambda i: (i, 0))
    vec = pl.BlockSpec((1, d), lambda i: (0, 0))
    return pl.pallas_call(
        body, grid=(t // tt,), in_specs=[row, vec, vec, vec], out_specs=row,
        out_shape=jax.ShapeDtypeStruct((t, d), BF16), name="normmod_fwd", compiler_params=_cparams(("parallel",)),
    )(x, g, sc, sh)


def normmod_bwd(x, g, sc, sh, dh, dres):
    t, d = x.shape
    tt = _row_tile(t)

    def body(x_ref, g_ref, sc_ref, sh_ref, dh_ref, dres_ref, dx_ref, dg_ref, dsc_ref, dsh_ref):
        _, vjp = jax.vjp(_normmod, x_ref[...], g_ref[...], sc_ref[...], sh_ref[...])
        dx, dg, dsc, dsh = vjp(dh_ref[...])
        dx_ref[...] = dx + dres_ref[...]

        @pl.when(pl.program_id(0) == 0)
        def _():
            dg_ref[...] = jnp.zeros_like(dg_ref)
            dsc_ref[...] = jnp.zeros_like(dsc_ref)
            dsh_ref[...] = jnp.zeros_like(dsh_ref)

        dg_ref[...] += dg
        dsc_ref[...] += dsc
        dsh_ref[...] += dsh

    row = pl.BlockSpec((tt, d), lambda i: (i, 0))
    vec = pl.BlockSpec((1, d), lambda i: (0, 0))
    vs = jax.ShapeDtypeStruct((1, d), F32)
    return pl.pallas_call(
        body, grid=(t // tt,), in_specs=[row, vec, vec, vec, row, row], out_specs=[row, vec, vec, vec],
        out_shape=[jax.ShapeDtypeStruct((t, d), F32), vs, vs, vs], name="normmod_bwd",
        compiler_params=_cparams(("arbitrary",)),
    )(x, g, sc, sh, dh, dres)


def resid_bwd(dxn, m, g):
    t, d = dxn.shape
    tt = _row_tile(t)

    def body(dx_ref, m_ref, g_ref, dm_ref, dg_ref):
        dx = dx_ref[...]
        dm_ref[...] = (dx * g_ref[...]).astype(BF16)

        @pl.when(pl.program_id(0) == 0)
        def _():
            dg_ref[...] = jnp.zeros_like(dg_ref)

        dg_ref[...] += jnp.sum(dx * m_ref[...], axis=0, keepdims=True)

    row = pl.BlockSpec((tt, d), lambda i: (i, 0))
    vec = pl.BlockSpec((1, d), lambda i: (0, 0))
    return pl.pallas_call(
        body, grid=(t // tt,), in_specs=[row, row, vec], out_specs=[row, vec],
        out_shape=[jax.ShapeDtypeStruct((t, d), BF16), jax.ShapeDtypeStruct((1, d), F32)], name="resid_bwd",
        compiler_params=_cparams(("arbitrary",)),
    )(dxn, m, g)


def loss_head(x, gf, tgt):
    t, d = x.shape
    tt = _row_tile(t)

    def f(xv, gv, tv):
        y = _rms(xv, gv)
        return 0.5 * jnp.sum(jnp.mean(jnp.square(y - tv), axis=-1))

    def body(x_ref, g_ref, t_ref, loss_ref, dx_ref, dg_ref):
        val, vjp = jax.vjp(lambda xv, gv: f(xv, gv, t_ref[...]), x_ref[...], g_ref[...])
        dx, dg = vjp(jnp.ones((), F32))
        dx_ref[...] = dx

        @pl.when(pl.program_id(0) == 0)
        def _():
            dg_ref[...] = jnp.zeros_like(dg_ref)
            loss_ref[...] = jnp.zeros_like(loss_ref)

        dg_ref[...] += dg
        loss_ref[...] += jnp.full(loss_ref.shape, val, F32)

    row = pl.BlockSpec((tt, d), lambda i: (i, 0))
    vec = pl.BlockSpec((1, d), lambda i: (0, 0))
    lsp = pl.BlockSpec((1, 128), lambda i: (0, 0))
    return pl.pallas_call(
        body, grid=(t // tt,), in_specs=[row, vec, row], out_specs=[lsp, row, vec],
        out_shape=[jax.ShapeDtypeStruct((1, 128), F32), jax.ShapeDtypeStruct((t, d), F32),
                   jax.ShapeDtypeStruct((1, d), F32)],
        name="loss_head", compiler_params=_cparams(("arbitrary",)),
    )(x, gf, tgt)


def matmul(a, b, *, name, out_dtypes=(F32,), epi=None, epi_in=(), tm=2048, tn=512, tk=1024):
    m, k = a.shape
    n = b.shape[1]
    tm, tn, tk = min(tm, m), _div_tile(n, tn), _div_tile(k, tk)
    nk = k // tk
    n_epi, n_out = len(epi_in), len(out_dtypes)

    def body(*refs):
        a_ref, b_ref = refs[:2]
        e_refs = refs[2:2 + n_epi]
        o_refs = refs[2 + n_epi:2 + n_epi + n_out]
        kk = pl.program_id(2)
        part = jnp.dot(a_ref[...].astype(BF16), b_ref[...].astype(BF16), preferred_element_type=F32)

        def finish(acc):
            outs = (acc,) if epi is None else epi(acc, *[r[...] for r in e_refs])
            for o_ref, o in zip(o_refs, outs):
                o_ref[...] = o.astype(o_ref.dtype)

        if nk == 1:
            finish(part)
            return
        acc_ref = refs[-1]

        @pl.when(kk == 0)
        def _():
            acc_ref[...] = part

        @pl.when((kk > 0) & (kk < nk - 1))
        def _():
            acc_ref[...] += part

        @pl.when(kk == nk - 1)
        def _():
            finish(acc_ref[...] + part)

    in_specs = [pl.BlockSpec((tm, tk), lambda i, j, kk: (i, kk)), pl.BlockSpec((tk, tn), lambda i, j, kk: (kk, j))]
    for _, kind in epi_in:
        in_specs.append(pl.BlockSpec((tm, tn), lambda i, j, kk: (i, j)) if kind == "mn"
                        else pl.BlockSpec((1, tn), lambda i, j, kk: (0, j)))
    out_spec = pl.BlockSpec((tm, tn), lambda i, j, kk: (i, j))
    outs = pl.pallas_call(
        body, grid=(m // tm, n // tn, nk), in_specs=in_specs, out_specs=[out_spec] * n_out,
        out_shape=[jax.ShapeDtypeStruct((m, n), dt) for dt in out_dtypes],
        scratch_shapes=[pltpu.VMEM((tm, tn), F32)] if nk > 1 else [], name=name,
        compiler_params=_cparams(("parallel", "parallel", "arbitrary")),
    )(a, b, *[arr for arr, _ in epi_in])
    return outs[0] if n_out == 1 else outs


def matmul_tn(a, b, *, name, tko=1024, tn=512, tt=2048):
    t, ko = a.shape
    n = b.shape[1]
    tko, tn, tt = _div_tile(ko, tko), _div_tile(n, tn), min(tt, t)
    nt = t // tt

    def body(a_ref, b_ref, o_ref):
        s = pl.program_id(2)
        part = _dot(a_ref[...], b_ref[...], ((0,), (0,)))

        @pl.when(s == 0)
        def _():
            o_ref[...] = part

        @pl.when(s > 0)
        def _():
            o_ref[...] += part

    return pl.pallas_call(
        body, grid=(ko // tko, n // tn, nt),
        in_specs=[pl.BlockSpec((tt, tko), lambda i, j, s: (s, i)), pl.BlockSpec((tt, tn), lambda i, j, s: (s, j))],
        out_specs=pl.BlockSpec((tko, tn), lambda i, j, s: (i, j)), out_shape=jax.ShapeDtypeStruct((ko, n), F32),
        name=name, compiler_params=_cparams(("parallel", "parallel", "arbitrary")),
    )(a, b)


def merge_fwd(ys, w_branch, z):
    t = ys[0].shape[0]
    d = D_MODEL
    tm, tn = min(1024, t), 512
    gate0 = Z_GATE // tn

    def body(y0, y1, y2, y3, w_ref, g0, g1, g2, g3, up_ref, mg_ref):
        acc = jnp.zeros((tm, tn), F32)
        for n, (y_ref, g_ref) in enumerate(zip((y0, y1, y2, y3), (g0, g1, g2, g3))):
            up = jnp.dot(y_ref[...], w_ref[n], preferred_element_type=F32)
            up_ref[n] = up.astype(BF16)
            acc = acc + jax.nn.sigmoid(g_ref[...]) * up
        mg_ref[...] = acc.astype(BF16)

    ysp = pl.BlockSpec((tm, WIDTH), lambda i, j: (i, 0))
    gate_specs = [pl.BlockSpec((tm, tn), functools.partial(lambda i, j, n: (i, gate0 + n * (d // tn) + j), n=n))
                  for n in range(N_BRANCH)]
    return pl.pallas_call(
        body, grid=(t // tm, d // tn),
        in_specs=[ysp, ysp, ysp, ysp, pl.BlockSpec((N_BRANCH, WIDTH, tn), lambda i, j: (0, 0, j))] + gate_specs,
        out_specs=[pl.BlockSpec((N_BRANCH, tm, tn), lambda i, j: (0, i, j)), pl.BlockSpec((tm, tn), lambda i, j: (i, j))],
        out_shape=[jax.ShapeDtypeStruct((N_BRANCH, t, d), BF16), jax.ShapeDtypeStruct((t, d), BF16)],
        name="merge_fwd", compiler_params=_cparams(("parallel", "parallel")),
    )(*ys, w_branch, z, z, z, z)


def merge_bwd(dmerged, up, z):
    t, d = dmerged.shape
    tm, tn = min(1024, t), 512
    gate0 = Z_GATE // tn
    nj = d // tn

    def body(dm_ref, up_ref, g_ref, dzg_ref, dup_ref):
        dm = dm_ref[...]
        s = jax.nn.sigmoid(g_ref[...])
        dzg_ref[...] = (dm * up_ref[0] * s * (1.0 - s)).astype(BF16)
        dup_ref[0] = (dm * s).astype(BF16)

    blk = pl.BlockSpec((1, tm, tn), lambda i, j, n: (n, i, j))
    return pl.pallas_call(
        body, grid=(t // tm, nj, N_BRANCH),
        in_specs=[pl.BlockSpec((tm, tn), lambda i, j, n: (i, j)), blk,
                  pl.BlockSpec((tm, tn), lambda i, j, n: (i, gate0 + n * nj + j))],
        out_specs=[pl.BlockSpec((tm, tn), lambda i, j, n: (i, n * nj + j)), blk],
        out_shape=[jax.ShapeDtypeStruct((t, N_BRANCH * d), BF16), jax.ShapeDtypeStruct((N_BRANCH, t, d), BF16)],
        name="merge_bwd", compiler_params=_cparams(("parallel", "parallel", "arbitrary")),
    )(dmerged, up, z)


def _sg_tile(zsg, g, w, bt):
    tt = zsg.shape[0]
    a = jax.nn.gelu(zsg)
    u, v = a[:, :WIDTH], a[:, WIDTH:]
    vc = v - jnp.mean(v, axis=-1, keepdims=True)
    v = (vc * lax.rsqrt(jnp.mean(vc * vc, axis=-1, keepdims=True) + NORM_EPS)) * g
    row = lax.broadcasted_iota(jnp.int32, (CHUNK, CHUNK), 0)
    col = lax.broadcasted_iota(jnp.int32, (CHUNK, CHUNK), 1)
    cols = []
    for gi in range(SG_GROUPS):
        wc = jnp.where(row >= col, w[gi], 0.0)
        bias = bt[:, gi:gi + 1]
        rows = [_mm(wc, v[ch * CHUNK:(ch + 1) * CHUNK, gi * CHUNK:(gi + 1) * CHUNK]) + bias for ch in range(tt // CHUNK)]
        cols.append(jnp.concatenate(rows, axis=0) if len(rows) > 1 else rows[0])
    return u * jnp.concatenate(cols, axis=1)


def sg_fwd(z, g, w, bt):
    t = z.shape[0]
    tt = _row_tile(t)

    def body(z_ref, g_ref, w_ref, b_ref, y_ref):
        y_ref[...] = _sg_tile(z_ref[...], g_ref[...], w_ref[...], b_ref[...]).astype(BF16)

    return pl.pallas_call(
        body, grid=(t // tt,),
        in_specs=[pl.BlockSpec((tt, 2 * WIDTH), lambda i: (i, 0)), pl.BlockSpec((1, WIDTH), lambda i: (0, 0)),
                  pl.BlockSpec((SG_GROUPS, CHUNK, CHUNK), lambda i: (0, 0, 0)), pl.BlockSpec((CHUNK, SG_GROUPS), lambda i: (0, 0))],
        out_specs=pl.BlockSpec((tt, WIDTH), lambda i: (i, 0)), out_shape=jax.ShapeDtypeStruct((t, WIDTH), BF16),
        name="sg_fwd", compiler_params=_cparams(("parallel",)),
    )(z, g, w, bt)


def sg_bwd(z, g, w, bt, dy):
    t = z.shape[0]
    tt = _row_tile(t)

    def body(z_ref, g_ref, w_ref, b_ref, dy_ref, dz_ref, dg_ref, dw_ref, db_ref):
        _, vjp = jax.vjp(_sg_tile, z_ref[...], g_ref[...], w_ref[...], b_ref[...])
        dz, dg, dw, db = vjp(dy_ref[...])
        dz_ref[...] = dz.astype(BF16)

        @pl.when(pl.program_id(0) == 0)
        def _():
            dg_ref[...] = jnp.zeros_like(dg_ref)
            dw_ref[...] = jnp.zeros_like(dw_ref)
            db_ref[...] = jnp.zeros_like(db_ref)

        dg_ref[...] += dg
        dw_ref[...] += dw
        db_ref[...] += db

    gs = pl.BlockSpec((1, WIDTH), lambda i: (0, 0))
    ws = pl.BlockSpec((SG_GROUPS, CHUNK, CHUNK), lambda i: (0, 0, 0))
    bs = pl.BlockSpec((CHUNK, SG_GROUPS), lambda i: (0, 0))
    return pl.pallas_call(
        body, grid=(t // tt,),
        in_specs=[pl.BlockSpec((tt, 2 * WIDTH), lambda i: (i, 0)), gs, ws, bs, pl.BlockSpec((tt, WIDTH), lambda i: (i, 0))],
        out_specs=[pl.BlockSpec((tt, 2 * WIDTH), lambda i: (i, 0)), gs, ws, bs],
        out_shape=[jax.ShapeDtypeStruct((t, 2 * WIDTH), BF16), jax.ShapeDtypeStruct((1, WIDTH), F32),
                   jax.ShapeDtypeStruct((SG_GROUPS, CHUNK, CHUNK), F32), jax.ShapeDtypeStruct((CHUNK, SG_GROUPS), F32)],
        name="sg_bwd", compiler_params=_cparams(("arbitrary",)),
    )(z, g, w, bt, dy)


HALO = 8


def _shift_down(halo, cur, k):
    tt = cur.shape[0]
    ext = jnp.concatenate([halo, cur], axis=0)
    return ext[HALO - k:HALO - k + tt]


def _shift_up(cur, halo, k):
    tt = cur.shape[0]
    ext = jnp.concatenate([cur, halo], axis=0)
    return ext[k:k + tt]


def conv_fwd(z, cw):
    t = z.shape[0]
    tt = _row_tile(t)
    cb = Z_CONV // WIDTH
    hb = tt // HALO

    def body(gb_ref, gc_ref, xv_ref, hgc_ref, hxv_ref, w_ref, y_ref):
        first = (pl.program_id(0) == 0)
        tcur = gc_ref[...] * xv_ref[...]
        thalo = jnp.where(first, 0.0, hgc_ref[...] * hxv_ref[...])
        w = w_ref[...]
        y = w[2:3] * tcur + w[1:2] * _shift_down(thalo, tcur, 1) + w[0:1] * _shift_down(thalo, tcur, 2)
        y_ref[...] = (gb_ref[...] * y).astype(BF16)

    def cur(off):
        return pl.BlockSpec((tt, WIDTH), lambda i: (i, cb + off))

    def prev(off):
        return pl.BlockSpec((HALO, WIDTH), lambda i: (jnp.maximum(i * hb - 1, 0), cb + off))

    return pl.pallas_call(
        body, grid=(t // tt,),
        in_specs=[cur(0), cur(1), cur(2), prev(1), prev(2), pl.BlockSpec((3, WIDTH), lambda i: (0, 0))],
        out_specs=pl.BlockSpec((tt, WIDTH), lambda i: (i, 0)), out_shape=jax.ShapeDtypeStruct((t, WIDTH), BF16),
        name="conv_fwd", compiler_params=_cparams(("parallel",)),
    )(z, z, z, z, z, cw)


def conv_bwd(z, cw, dy):
    t = z.shape[0]
    tt = _row_tile(t)
    nt = t // tt
    cb = Z_CONV // WIDTH
    hb = tt // HALO

    def body(gb_ref, gc_ref, xv_ref, hgc_ref, hxv_ref, ngb_ref, dy_ref, ndy_ref, w_ref, dz_ref, dw_ref):
        i = pl.program_id(0)
        gb, gc, xv = gb_ref[...], gc_ref[...], xv_ref[...]
        tcur = gc * xv
        thalo = jnp.where(i == 0, 0.0, hgc_ref[...] * hxv_ref[...])
        t1, t2 = _shift_down(thalo, tcur, 1), _shift_down(thalo, tcur, 2)
        w = w_ref[...]
        y = w[2:3] * tcur + w[1:2] * t1 + w[0:1] * t2
        dy = dy_ref[...]
        dyc = dy * gb
        dyn = jnp.where(i == nt - 1, 0.0, ndy_ref[...] * ngb_ref[...])
        dt = w[2:3] * dyc + w[1:2] * _shift_up(dyc, dyn, 1) + w[0:1] * _shift_up(dyc, dyn, 2)
        dz_ref[...] = jnp.concatenate([dy * y, dt * xv, dt * gc], axis=1).astype(BF16)

        @pl.when(i == 0)
        def _():
            dw_ref[...] = jnp.zeros_like(dw_ref)

        dw_ref[...] += jnp.concatenate([jnp.sum(dyc * t2, axis=0, keepdims=True), jnp.sum(dyc * t1, axis=0, keepdims=True),
                                        jnp.sum(dyc * tcur, axis=0, keepdims=True)], axis=0)

    def cur(off):
        return pl.BlockSpec((tt, WIDTH), lambda i: (i, cb + off))

    def prev(off):
        return pl.BlockSpec((HALO, WIDTH), lambda i: (jnp.maximum(i * hb - 1, 0), cb + off))

    last = t // HALO - 1
    nxt_z = pl.BlockSpec((HALO, WIDTH), lambda i: (jnp.minimum((i + 1) * hb, last), cb))
    nxt_dy = pl.BlockSpec((HALO, WIDTH), lambda i: (jnp.minimum((i + 1) * hb, last), 0))
    return pl.pallas_call(
        body, grid=(nt,),
        in_specs=[cur(0), cur(1), cur(2), prev(1), prev(2), nxt_z, pl.BlockSpec((tt, WIDTH), lambda i: (i, 0)), nxt_dy,
                  pl.BlockSpec((3, WIDTH), lambda i: (0, 0))],
        out_specs=[pl.BlockSpec((tt, 3 * WIDTH), lambda i: (i, 0)), pl.BlockSpec((3, WIDTH), lambda i: (0, 0))],
        out_shape=[jax.ShapeDtypeStruct((t, 3 * WIDTH), BF16), jax.ShapeDtypeStruct((3, WIDTH), F32)],
        name="conv_bwd", compiler_params=_cparams(("arbitrary",)),
    )(z, z, z, z, z, z, dy, dy, cw)


def _mla_prep(zm, cos, sin, gq, gkv, wuq, wukv):
    tt = zm.shape[0]
    cq, ckv, kpe = zm[:, :MLA_Q_RANK], zm[:, MLA_Q_RANK:MLA_Q_RANK + MLA_KV_RANK], zm[:, MLA_Q_RANK + MLA_KV_RANK:MLA_COLS]
    q = _mm(_rms(cq, gq), wuq) * MLA_SCALE
    kv = _mm(_rms(ckv, gkv), wukv)
    half = MLA_ROPE // 2

    def rope(xr):
        x1, x2 = xr[:, :half], xr[:, half:]
        return jnp.concatenate([x1 * cos - x2 * sin, x2 * cos + x1 * sin], axis=-1)

    kpe = rope(kpe)
    pad = jnp.zeros((tt, MLA_QK_PAD - MLA_NOPE - MLA_ROPE), F32)
    dq, dkv = MLA_NOPE + MLA_ROPE, MLA_NOPE + MLA_V
    qs, ks, vs = [], [], []
    for h in range(HEADS):
        qs.append(jnp.concatenate([q[:, h * dq:h * dq + MLA_NOPE], rope(q[:, h * dq + MLA_NOPE:(h + 1) * dq]), pad], axis=-1))
        ks.append(jnp.concatenate([kv[:, h * dkv:h * dkv + MLA_NOPE], kpe, pad], axis=-1))
        vs.append(kv[:, h * dkv + MLA_NOPE:(h + 1) * dkv])
    return qs, ks, vs


def _mla_prep_specs(tt):
    zs = pl.BlockSpec((tt, WIDTH), lambda i: (i, Z_MLA // WIDTH))
    cs = pl.BlockSpec((tt, MLA_ROPE // 2), lambda i: (i, 0))
    return [zs, cs, cs, pl.BlockSpec((1, MLA_Q_RANK), lambda i: (0, 0)), pl.BlockSpec((1, MLA_KV_RANK), lambda i: (0, 0)),
            pl.BlockSpec((MLA_Q_RANK, HEADS * (MLA_NOPE + MLA_ROPE)), lambda i: (0, 0)),
            pl.BlockSpec((MLA_KV_RANK, HEADS * (MLA_NOPE + MLA_V)), lambda i: (0, 0))]


def mla_prep_fwd(z, cos, sin, gq, gkv, wuq, wukv):
    t = z.shape[0]
    tt = _row_tile(t)

    def body(z_ref, c_ref, s_ref, gq_ref, gkv_ref, wq_ref, wkv_ref, q_ref, k_ref, v_ref):
        qs, ks, vs = _mla_prep(z_ref[...], c_ref[...], s_ref[...], gq_ref[...], gkv_ref[...],
                               wq_ref[...].astype(F32), wkv_ref[...].astype(F32))
        for h in range(HEADS):
            q_ref[h] = qs[h].astype(BF16)
            k_ref[h] = ks[h].astype(BF16)
            v_ref[h] = vs[h].astype(BF16)

    hs = pl.BlockSpec((HEADS, tt, MLA_QK_PAD), lambda i: (0, i, 0))
    vsp = pl.BlockSpec((HEADS, tt, MLA_V), lambda i: (0, i, 0))
    return pl.pallas_call(
        body, grid=(t // tt,), in_specs=_mla_prep_specs(tt), out_specs=[hs, hs, vsp],
        out_shape=[jax.ShapeDtypeStruct((HEADS, t, MLA_QK_PAD), BF16), jax.ShapeDtypeStruct((HEADS, t, MLA_QK_PAD), BF16),
                   jax.ShapeDtypeStruct((HEADS, t, MLA_V), BF16)],
        name="mla_prep_fwd", compiler_params=_cparams(("parallel",)),
    )(z, cos, sin, gq, gkv, wuq, wukv)


def mla_prep_bwd(z, cos, sin, gq, gkv, wuq, wukv, dq, dk, dv):
    t = z.shape[0]
    tt = _row_tile(t)

    def body(z_ref, c_ref, s_ref, gq_ref, gkv_ref, wq_ref, wkv_ref, dq_ref, dk_ref, dv_ref,
             dz_ref, dgq_ref, dgkv_ref, dwq_ref, dwkv_ref):
        cos_v, sin_v = c_ref[...], s_ref[...]
        _, vjp = jax.vjp(lambda zm, a, b, wq, wkv: _mla_prep(zm, cos_v, sin_v, a, b, wq, wkv),
                         z_ref[...], gq_ref[...], gkv_ref[...], wq_ref[...].astype(F32), wkv_ref[...].astype(F32))
        cot = ([dq_ref[h] for h in range(HEADS)], [dk_ref[h] for h in range(HEADS)], [dv_ref[h] for h in range(HEADS)])
        dz, dgq, dgkv, dwq, dwkv = vjp(cot)
        dz_ref[...] = dz.astype(BF16)

        @pl.when(pl.program_id(0) == 0)
        def _():
            for r in (dgq_ref, dgkv_ref, dwq_ref, dwkv_ref):
                r[...] = jnp.zeros_like(r)

        dgq_ref[...] += dgq
        dgkv_ref[...] += dgkv
        dwq_ref[...] += dwq
        dwkv_ref[...] += dwkv

    specs = _mla_prep_specs(tt)
    hs = pl.BlockSpec((HEADS, tt, MLA_QK_PAD), lambda i: (0, i, 0))
    vsp = pl.BlockSpec((HEADS, tt, MLA_V), lambda i: (0, i, 0))
    nq, nkv = HEADS * (MLA_NOPE + MLA_ROPE), HEADS * (MLA_NOPE + MLA_V)
    return pl.pallas_call(
        body, grid=(t // tt,), in_specs=specs + [hs, hs, vsp],
        out_specs=[pl.BlockSpec((tt, WIDTH), lambda i: (i, 0)), specs[3], specs[4], specs[5], specs[6]],
        out_shape=[jax.ShapeDtypeStruct((t, WIDTH), BF16), jax.ShapeDtypeStruct((1, MLA_Q_RANK), F32),
                   jax.ShapeDtypeStruct((1, MLA_KV_RANK), F32), jax.ShapeDtypeStruct((MLA_Q_RANK, nq), F32),
                   jax.ShapeDtypeStruct((MLA_KV_RANK, nkv), F32)],
        name="mla_prep_bwd", compiler_params=_cparams(("arbitrary",)),
    )(z, cos, sin, gq, gkv, wuq, wukv, dq, dk, dv)


def _att_blocks(t):
    return min(ATT_BQ, t), min(ATT_BK, t)


def _key_blocks(a, bk):
    h, t, d = a.shape
    return a.reshape(h, t // bk, bk, d).transpose(0, 1, 3, 2)


def _att_positions(kb, qi, bq, bk, lo):
    krow = kb * bk + lax.broadcasted_iota(jnp.int32, (bk, bq - lo), 0)
    qcol = qi * bq + lo + lax.broadcasted_iota(jnp.int32, (bk, bq - lo), 1)
    return krow, qcol


def _tail(x, lo):
    return x if lo == 0 else x[:, lo:]


def _join(old, new_tail, lo):
    return new_tail if lo == 0 else jnp.concatenate([old[:, :lo], new_tail], axis=1)


def _walk(qi, nd, bk, step, carry, diagonal_first=False):
    n_full = qi * nd
    diagonal = [(n_full + d, d * bk) for d in range(nd)]
    if diagonal_first:
        for kb, lo in reversed(diagonal):
            carry = step(kb, carry, True, lo)
        return lax.fori_loop(0, n_full, lambda i, c: step(n_full - 1 - i, c, False, 0), carry)
    carry = lax.fori_loop(0, n_full, lambda kb, c: step(kb, c, False, 0), carry)
    for kb, lo in diagonal:
        carry = step(kb, carry, True, lo)
    return carry


def softmax_att_fwd(qt, k, vtb, scale):
    hh, dk, t = qt.shape
    dv = vtb.shape[2]
    bq, bk = _att_blocks(t)

    def body(qt_ref, k_ref, vt_ref, o_ref, lse_ref):
        qi = pl.program_id(1)
        qtv = qt_ref[0]

        def step(kb, carry, masked, lo):
            m_all, l_all, acc_all = carry
            m, l, acc = _tail(m_all, lo), _tail(l_all, lo), _tail(acc_all, lo)
            ks = pl.multiple_of(kb * bk, bk)
            s = jnp.dot(k_ref[0, pl.ds(ks, bk), :], _tail(qtv, lo), preferred_element_type=F32)
            if scale != 1.0:
                s = s * scale
            if masked:
                krow, qcol = _att_positions(kb, qi, bq, bk, lo)
                s = jnp.where(krow <= qcol, s, NEG_BIG)
            m_new = jnp.maximum(m, jnp.max(s, axis=0, keepdims=True))
            p = jnp.exp(s - m_new)
            alpha = jnp.exp(m - m_new)
            l = alpha * l + jnp.sum(p, axis=0, keepdims=True)
            acc = alpha * acc + jnp.dot(vt_ref[0, kb], p.astype(BF16), preferred_element_type=F32)
            return _join(m_all, m_new, lo), _join(l_all, l, lo), _join(acc_all, acc, lo)

        init = (jnp.full((1, bq), NEG_BIG, F32), jnp.zeros((1, bq), F32), jnp.zeros((dv, bq), F32))
        m, l, acc = _walk(qi, bq // bk, bk, step, init)
        o_ref[0] = acc / l
        lse_ref[0] = m + jnp.log(l)

    return pl.pallas_call(
        body, grid=(hh, t // bq),
        in_specs=[pl.BlockSpec((1, dk, bq), lambda h, i: (h, 0, i)), pl.BlockSpec((1, t, dk), lambda h, i: (h, 0, 0)),
                  pl.BlockSpec((1, t // bk, dv, bk), lambda h, i: (h, 0, 0, 0))],
        out_specs=[pl.BlockSpec((1, dv, bq), lambda h, i: (h, 0, i)), pl.BlockSpec((1, 1, bq), lambda h, i: (h, 0, i))],
        out_shape=[jax.ShapeDtypeStruct((hh, dv, t), F32), jax.ShapeDtypeStruct((hh, 1, t), F32)],
        name="softmax_att_fwd", compiler_params=_cparams(("parallel", "arbitrary")),
    )(qt, k, vtb)


def softmax_att_bwd(qt, q, k, ktb, v, ot, lse, do, dot_, scale):
    hh, dk, t = qt.shape
    dv = v.shape[2]
    bq, bk = _att_blocks(t)

    def body(qt_ref, q_ref, k_ref, kt_ref, v_ref, ot_ref, lse_ref, do_ref, dot_ref, dqt_ref, dk_ref, dv_ref):
        qi = pl.program_id(1)

        @pl.when(qi == 0)
        def _():
            dk_ref[...] = jnp.zeros_like(dk_ref)
            dv_ref[...] = jnp.zeros_like(dv_ref)

        qtv, qv, dov, dotv = qt_ref[0], q_ref[0], do_ref[0], dot_ref[0]
        lse_v = lse_ref[0]
        delta = jnp.sum(dotv.astype(F32) * ot_ref[0], axis=0, keepdims=True)

        def step(kb, dqt, masked, lo):
            ks = pl.multiple_of(kb * bk, bk)
            s = jnp.dot(k_ref[0, pl.ds(ks, bk), :], _tail(qtv, lo), preferred_element_type=F32)
            if scale != 1.0:
                s = s * scale
            if masked:
                krow, qcol = _att_positions(kb, qi, bq, bk, lo)
                s = jnp.where(krow <= qcol, s, NEG_BIG)
            p = jnp.exp(s - _tail(lse_v, lo))
            dp = jnp.dot(v_ref[0, pl.ds(ks, bk), :], _tail(dotv, lo), preferred_element_type=F32)
            ds = p * (dp - _tail(delta, lo))
            if scale != 1.0:
                ds = ds * scale
            ds = ds.astype(BF16)
            dk_ref[0, pl.ds(ks, bk), :] += jnp.dot(ds, qv[lo:], preferred_element_type=F32)
            dv_ref[0, pl.ds(ks, bk), :] += jnp.dot(p.astype(BF16), dov[lo:], preferred_element_type=F32)
            return _join(dqt, _tail(dqt, lo) + jnp.dot(kt_ref[0, kb], ds, preferred_element_type=F32), lo)

        dqt_ref[0] = _walk(qi, bq // bk, bk, step, jnp.zeros((dk, bq), F32))

    nkb = t // bk
    qts = pl.BlockSpec((1, dk, bq), lambda h, i: (h, 0, i))
    qs = pl.BlockSpec((1, bq, dk), lambda h, i: (h, i, 0))
    kfull = pl.BlockSpec((1, t, dk), lambda h, i: (h, 0, 0))
    vfull = pl.BlockSpec((1, t, dv), lambda h, i: (h, 0, 0))
    vts = pl.BlockSpec((1, dv, bq), lambda h, i: (h, 0, i))
    return pl.pallas_call(
        body, grid=(hh, t // bq),
        in_specs=[qts, qs, kfull, pl.BlockSpec((1, nkb, dk, bk), lambda h, i: (h, 0, 0, 0)), vfull, vts,
                  pl.BlockSpec((1, 1, bq), lambda h, i: (h, 0, i)), pl.BlockSpec((1, bq, dv), lambda h, i: (h, i, 0)), vts],
        out_specs=[qts, kfull, vfull],
        out_shape=[jax.ShapeDtypeStruct((hh, dk, t), F32), jax.ShapeDtypeStruct((hh, t, dk), F32),
                   jax.ShapeDtypeStruct((hh, t, dv), F32)],
        name="softmax_att_bwd", compiler_params=_cparams(("parallel", "arbitrary")),
    )(qt, q, k, ktb, v, ot, lse, do, dot_)


SUM_ROWS = 8


def _softplus(z):
    return jnp.maximum(z, 0.0) + jnp.log(1.0 + jnp.exp(-jnp.abs(z)))


def _tri_ext(bk, kind):
    r = lax.broadcasted_iota(jnp.int32, (bk + SUM_ROWS, bk), 0)
    c = lax.broadcasted_iota(jnp.int32, (bk + SUM_ROWS, bk), 1)
    if kind == "neg_later":
        return jnp.where((c > r) | (r >= bk), -1.0, 0.0).astype(BF16)
    return jnp.where((c <= r) | (r >= bk), 1.0, 0.0).astype(BF16)


def _split_dot(m01, x):
    hi = lax.bitcast_convert_type(lax.bitcast_convert_type(x, jnp.uint32) & jnp.uint32(0xFFFF0000), F32)
    lo = x - hi
    return (jnp.dot(m01, hi.astype(BF16), preferred_element_type=F32) + jnp.dot(m01, lo.astype(BF16), preferred_element_type=F32))


def sb_cast(z):
    t = z.shape[0]
    tt = _row_tile(t)
    cols = 3 * WIDTH
    assert Z_SB % cols == 0

    def body(z_ref, o_ref):
        o_ref[...] = z_ref[...].astype(BF16)

    return pl.pallas_call(
        body, grid=(t // tt,), in_specs=[pl.BlockSpec((tt, cols), lambda i: (i, Z_SB // cols))],
        out_specs=pl.BlockSpec((tt, cols), lambda i: (i, 0)), out_shape=jax.ShapeDtypeStruct((t, cols), BF16),
        name="sb_cast", compiler_params=_cparams(("parallel",)),
    )(z)


def _first_last_step(nh, nq):
    h, i = pl.program_id(0), pl.program_id(1)
    return (h == 0) & (i == 0), (h == nh - 1) & (i == nq - 1)


def stick_att_fwd(qt, k, vtb, comm=None):
    hh, dk, t = qt.shape
    dv = vtb.shape[2]
    bq, bk = _att_blocks(t)
    nkb = t // bk
    srcs, gather, c_specs, c_shapes, c_sems = _with_exchange(comm)
    n = len(srcs)

    def body(*refs):
        qt_ref, k_ref, vt_ref = refs[:3]
        o_ref, rs_ref = refs[3 + n:5 + n]
        if n:
            start, wait = _exchange_ops(refs[3:3 + n], refs[5 + n:5 + 2 * n], *refs[5 + 2 * n:], gather)
            first, last = _first_last_step(hh, t // bq)
            pl.when(first)(start)
        qi = pl.program_id(1)
        qtv = qt_ref[0]
        neg_later = _tri_ext(bk, "neg_later")

        def step(kb, carry, masked, lo):
            r_all, acc_all = carry
            r, acc = _tail(r_all, lo), _tail(acc_all, lo)
            ks = pl.multiple_of(kb * bk, bk)
            z = jnp.dot(k_ref[0, pl.ds(ks, bk), :], _tail(qtv, lo), preferred_element_type=F32)
            sp = _softplus(z)
            la = z - sp
            if masked:
                krow, qcol = _att_positions(kb, qi, bq, bk, lo)
                mask = krow < qcol
                sp = jnp.where(mask, sp, 0.0)
            rs_ref[0, kb] = r_all
            sums = _split_dot(neg_later, sp)
            a = jnp.exp(la + sums[:bk])
            if masked:
                a = jnp.where(mask, a, 0.0)
            acc = acc + jnp.dot(vt_ref[0, kb], a.astype(BF16), preferred_element_type=F32) * jnp.exp(r)
            return _join(r_all, r + sums[bk:bk + 1], lo), _join(acc_all, acc, lo)

        init = (jnp.zeros((1, bq), F32), jnp.zeros((dv, bq), F32))
        o_ref[0] = _walk(qi, bq // bk, bk, step, init, diagonal_first=True)[1]
        if n:
            pl.when(last)(wait)

    outs = pl.pallas_call(
        body, grid=(hh, t // bq),
        in_specs=[pl.BlockSpec((1, dk, bq), lambda h, i: (h, 0, i)), pl.BlockSpec((1, t, dk), lambda h, i: (h, 0, 0)),
                  pl.BlockSpec((1, nkb, dv, bk), lambda h, i: (h, 0, 0, 0))] + c_specs,
        out_specs=[pl.BlockSpec((1, dv, bq), lambda h, i: (h, 0, i)),
                   pl.BlockSpec((1, nkb, 1, bq), lambda h, i: (h, 0, 0, i))] + c_specs,
        out_shape=[jax.ShapeDtypeStruct((hh, dv, t), F32), jax.ShapeDtypeStruct((hh, nkb, 1, t), F32)] + c_shapes,
        scratch_shapes=c_sems, name="stick_att_fwd_exchange" if n else "stick_att_fwd",
        compiler_params=pltpu.CompilerParams(dimension_semantics=("arbitrary", "arbitrary") if n else ("parallel", "arbitrary"),
                                             vmem_limit_bytes=VMEM_LIMIT_V7X, has_side_effects=bool(n)),
    )(qt, k, vtb, *srcs)
    return outs[0], outs[1], outs[2:]


def stick_att_bwd(qt, q, k, ktb, v, rs, do, dot_, scale, comm=None):
    hh, dk, t = qt.shape
    dv = v.shape[2]
    bq, bk = _att_blocks(t)
    nkb = t // bk
    srcs, gather, c_specs, c_shapes, c_sems = _with_exchange(comm)
    n = len(srcs)

    def body(*refs):
        qt_ref, q_ref, k_ref, kt_ref, v_ref, rs_ref, do_ref, dot_ref = refs[:8]
        dqt_ref, dk_ref, dv_ref = refs[8 + n:11 + n]
        if n:
            start, wait = _exchange_ops(refs[8:8 + n], refs[11 + n:11 + 2 * n], *refs[11 + 2 * n:], gather)
            first, last = _first_last_step(hh, t // bq)
            pl.when(first)(start)
        qi = pl.program_id(1)

        @pl.when(qi == 0)
        def _():
            dk_ref[...] = jnp.zeros_like(dk_ref)
            dv_ref[...] = jnp.zeros_like(dv_ref)

        qtv, qv, dov, dotv = qt_ref[0], q_ref[0], do_ref[0], dot_ref[0]
        neg_later, upto = _tri_ext(bk, "neg_later"), _tri_ext(bk, "upto")

        def step(kb, carry, masked, lo):
            pre_all, dqt_all = carry
            pre, dqt = _tail(pre_all, lo), _tail(dqt_all, lo)
            ks = pl.multiple_of(kb * bk, bk)
            z = jnp.dot(k_ref[0, pl.ds(ks, bk), :], _tail(qtv, lo), preferred_element_type=F32)
            sp = _softplus(z)
            la = z - sp
            if masked:
                krow, qcol = _att_positions(kb, qi, bq, bk, lo)
                mask = krow < qcol
                sp = jnp.where(mask, sp, 0.0)
            a = jnp.exp(la + _split_dot(neg_later, sp)[:bk] + _tail(rs_ref[0, kb], lo))
            if masked:
                a = jnp.where(mask, a, 0.0)
            beta = jnp.exp(la)
            g = a * jnp.dot(v_ref[0, pl.ds(ks, bk), :], _tail(dotv, lo), preferred_element_type=F32)
            sums = jnp.dot(upto, g.astype(BF16), preferred_element_type=F32)
            dz = g - beta * (pre + sums[:bk])
            if masked:
                dz = jnp.where(mask, dz, 0.0)
            dzb = dz.astype(BF16)
            dk_ref[0, pl.ds(ks, bk), :] += jnp.dot(dzb, qv[lo:], preferred_element_type=F32)
            dv_ref[0, pl.ds(ks, bk), :] += jnp.dot(a.astype(BF16), dov[lo:], preferred_element_type=F32)
            dqt = dqt + jnp.dot(kt_ref[0, kb], dzb, preferred_element_type=F32)
            return _join(pre_all, pre + sums[bk:bk + 1], lo), _join(dqt_all, dqt, lo)

        init = (jnp.zeros((1, bq), F32), jnp.zeros((dk, bq), F32))
        dqt_ref[0] = _walk(qi, bq // bk, bk, step, init)[1] * scale
        if n:
            pl.when(last)(wait)

    qts = pl.BlockSpec((1, dk, bq), lambda h, i: (h, 0, i))
    qs = pl.BlockSpec((1, bq, dk), lambda h, i: (h, i, 0))
    kfull = pl.BlockSpec((1, t, dk), lambda h, i: (h, 0, 0))
    vfull = pl.BlockSpec((1, t, dv), lambda h, i: (h, 0, 0))
    outs = pl.pallas_call(
        body, grid=(hh, t // bq),
        in_specs=[qts, qs, kfull, pl.BlockSpec((1, nkb, dk, bk), lambda h, i: (h, 0, 0, 0)), vfull,
                  pl.BlockSpec((1, nkb, 1, bq), lambda h, i: (h, 0, 0, i)), pl.BlockSpec((1, bq, dv), lambda h, i: (h, i, 0)),
                  pl.BlockSpec((1, dv, bq), lambda h, i: (h, 0, i))] + c_specs,
        out_specs=[qts, kfull, vfull] + c_specs,
        out_shape=[jax.ShapeDtypeStruct((hh, dk, t), F32), jax.ShapeDtypeStruct((hh, t, dk), F32),
                   jax.ShapeDtypeStruct((hh, t, dv), F32)] + c_shapes,
        scratch_shapes=c_sems, name="stick_att_bwd_exchange" if n else "stick_att_bwd",
        compiler_params=pltpu.CompilerParams(dimension_semantics=("arbitrary", "arbitrary") if n else ("parallel", "arbitrary"),
                                             vmem_limit_bytes=VMEM_LIMIT_V7X, has_side_effects=bool(n)),
    )(qt, q, k, ktb, v, rs, do, dot_, *srcs)
    return outs[0], outs[1], outs[2], outs[3:]


def mod_fwd(c_all, ada_w, ada_b_cols):
    nl, d, n = ada_w.shape

    def body(c_ref, w_ref, b_ref, o_ref):
        act = jax.nn.silu(c_ref[...])
        o_ref[0] = jnp.dot(act, w_ref[0], precision=lax.Precision.HIGHEST, preferred_element_type=F32) + b_ref[0]

    return pl.pallas_call(
        body, grid=(nl,),
        in_specs=[pl.BlockSpec((N_DEV, d), lambda l: (0, 0)), pl.BlockSpec((1, d, n), lambda l: (l, 0, 0)),
                  pl.BlockSpec((1, 1, n), lambda l: (l, 0, 0))],
        out_specs=pl.BlockSpec((1, N_DEV, n), lambda l: (l, 0, 0)), out_shape=jax.ShapeDtypeStruct((nl, N_DEV, n), F32),
        name="mod_fwd", compiler_params=_cparams(("parallel",)),
    )(c_all, ada_w, ada_b_cols)


def ada_w_grad(c_all_t, dmod_cols):
    d = c_all_t.shape[0]
    nl, _, n = dmod_cols.shape

    def body(c_ref, dm_ref, o_ref):
        act = jax.nn.silu(c_ref[...])
        dm = dm_ref[0]
        acc = act[:, 0:1] * dm[0:1, :]
        for e in range(1, N_DEV):
            acc = acc + act[:, e:e + 1] * dm[e:e + 1, :]
        o_ref[0] = acc

    return pl.pallas_call(
        body, grid=(nl,),
        in_specs=[pl.BlockSpec((d, N_DEV), lambda l: (0, 0)), pl.BlockSpec((1, N_DEV, n), lambda l: (l, 0, 0))],
        out_specs=pl.BlockSpec((1, d, n), lambda l: (l, 0, 0)), out_shape=jax.ShapeDtypeStruct((nl, d, n), F32),
        name="ada_w_grad", compiler_params=_cparams(("parallel",)),
    )(c_all_t, dmod_cols)


ADAM_BLOCK_BYTES = 6 * 1024 * 1024


def _rows2d(a, lead=0):
    return a.reshape(a.shape[:lead] + (-1, a.shape[-1])) if a.ndim > lead + 1 else a.reshape(a.shape[:lead] + (1, -1))


def adamw(slots, w, m, v, *, name):
    shape = w.shape
    s2 = _rows2d(slots, 1)
    w2, m2, v2 = _rows2d(w), _rows2d(m), _rows2d(v)
    ns, r, c = s2.shape
    per_row = c * (ns * s2.dtype.itemsize + 7 * 4)
    tr = r
    if r * per_row > ADAM_BLOCK_BYTES:
        tr = max(t for t in range(8, r, 8) if r % t == 0 and t * per_row <= ADAM_BLOCK_BYTES)

    def body(s_ref, w_ref, m_ref, v_ref, g_ref, d_ref, mo_ref, vo_ref):
        g = s_ref[0].astype(F32)
        for i in range(1, ns):
            g = g + s_ref[i].astype(F32)
        mn = ADAM_B1 * m_ref[...] + (1.0 - ADAM_B1) * g
        vn = ADAM_B2 * v_ref[...] + (1.0 - ADAM_B2) * jnp.square(g)
        m_hat = mn / (1.0 - ADAM_B1 ** ADAM_STEP)
        v_hat = vn / (1.0 - ADAM_B2 ** ADAM_STEP)
        g_ref[...] = g
        d_ref[...] = -ADAM_LR * (m_hat / (jnp.sqrt(v_hat) + ADAM_EPS) + ADAM_WD * w_ref[...])
        mo_ref[...] = mn
        vo_ref[...] = vn

    row = pl.BlockSpec((tr, c), lambda i: (i, 0))
    outs = pl.pallas_call(
        body, grid=(r // tr,), in_specs=[pl.BlockSpec((ns, tr, c), lambda i: (0, i, 0)), row, row, row],
        out_specs=[row] * 4, out_shape=[jax.ShapeDtypeStruct((r, c), F32)] * 4, name=name,
        compiler_params=_cparams(("parallel",)),
    )(s2, w2, m2, v2)
    return [o.reshape(shape) for o in outs]


def _exchange(srcs, *, gather, name):
    n = len(srcs)

    def body(*refs):
        start, wait = _exchange_ops(refs[:n], refs[n:2 * n], *refs[2 * n:], gather)
        start()
        wait()

    return pl.pallas_call(
        body, in_specs=[ANY_SPEC] * n, out_specs=[ANY_SPEC] * n, out_shape=_exchange_out_shapes(srcs, gather),
        scratch_shapes=_exchange_sems(n), name=name, compiler_params=pltpu.CompilerParams(has_side_effects=True),
    )(*srcs)


ANY_SPEC = pl.BlockSpec(memory_space=pl.ANY)


def _exchange_out_shapes(srcs, gather):
    return [jax.ShapeDtypeStruct((N_DEV,) + (tuple(s.shape) if gather else tuple(s.shape[1:])), s.dtype) for s in srcs]


def _exchange_sems(n):
    return [pltpu.SemaphoreType.DMA((N_DEV - 1, n)), pltpu.SemaphoreType.DMA((N_DEV - 1, n)), pltpu.SemaphoreType.DMA((n,))]


def _exchange_ops(src_refs, out_refs, send_sems, recv_sems, local_sems, gather):
    n = len(src_refs)
    mx, my, mc = lax.axis_index("x"), lax.axis_index("y"), lax.axis_index("c")
    me = 4 * mx + 2 * my + mc
    peers = []
    for k in range(1, N_DEV):
        px = 1 - mx if k & 4 else mx
        py = 1 - my if k & 2 else my
        pc = 1 - mc if k & 1 else mc
        peers.append(((px, py, pc), 4 * px + 2 * py + pc))

    def part(a, idx):
        return src_refs[a] if gather else src_refs[a].at[idx]

    def copy(k, a, slot):
        dev, pid = peers[k]
        return pltpu.make_async_remote_copy(src_ref=part(a, pid), dst_ref=out_refs[a].at[slot], send_sem=send_sems.at[k, a],
                                            recv_sem=recv_sems.at[k, a], device_id=dev, device_id_type=MESH)

    def local(a):
        return pltpu.make_async_copy(part(a, me), out_refs[a].at[me], local_sems.at[a])

    def start():
        for a in range(n):
            local(a).start()
            for k in range(N_DEV - 1):
                copy(k, a, me).start()

    def wait():
        for a in range(n):
            for k in range(N_DEV - 1):
                copy(k, a, me).wait_send()
                copy(k, a, peers[k][1]).wait_recv()
            local(a).wait()

    return start, wait


def _with_exchange(comm):
    srcs, gather = comm if comm is not None else ((), True)
    n = len(srcs)
    return (list(srcs), gather, [ANY_SPEC] * n, _exchange_out_shapes(srcs, gather) if n else [], _exchange_sems(n) if n else [])


def _to_blocks(full, ax):
    shp = full.shape
    g = full.reshape(shp[:ax] + (N_DEV, shp[ax] // N_DEV) + shp[ax + 1:])
    return jnp.moveaxis(g, ax, 0)


def _from_blocks(seg, ax):
    g = jnp.moveaxis(seg, 0, ax)
    shp = g.shape
    return g.reshape(shp[:ax] + (shp[ax] * shp[ax + 1],) + shp[ax + 2:])


def _to_heads(a, dh):
    return a.reshape(a.shape[0], -1, dh).transpose(1, 0, 2)


def _from_heads(a):
    return a.transpose(1, 0, 2).reshape(a.shape[1], -1)


def _from_heads_t(a):
    return a.transpose(2, 0, 1).reshape(a.shape[2], -1)


MLA_SCALE = (MLA_NOPE + MLA_ROPE) ** -0.5
SB_SCALE = SB_DIM ** -0.5
assert math.frexp(SB_SCALE)[0] == 0.5


def _resid_epi(acc, x, g):
    return x + g * acc, acc


def _layer_fwd(x, mod, p, cos, sin, comm=None, late=None):
    t = x.shape[0]
    bk = _att_blocks(t)[1]
    sh1, sc1, g1, sh2, sc2, g2 = [mod[i:i + 1] for i in range(6)]
    h = normmod_fwd(x, p["norm1_g"], sc1, sh1)
    z = matmul(h, p["w_in"], name="mm_in")
    y_sg = sg_fwd(z, p["sg_norm_g"], p["sg_w"], p["sg_bt"])
    y_cv = conv_fwd(z, p["conv_w"])
    mq, mk, mv = mla_prep_fwd(z, cos, sin, p["gq"], p["gkv"], p["wuq"], p["wukv"])
    mot, lse = softmax_att_fwd(mq.transpose(0, 2, 1), mk, _key_blocks(mv, bk), 1.0)
    y_mla = _from_heads_t(mot).astype(BF16)
    zsb = sb_cast(z)
    sq, sk, sv = [_to_heads(zsb[:, i * WIDTH:(i + 1) * WIDTH], SB_DIM) for i in range(3)]
    sq = sq * SB_SCALE
    sot, rs, got = stick_att_fwd(sq.transpose(0, 2, 1), sk, _key_blocks(sv, bk), comm)
    if late is not None:
        p = {**p, **late(got)}
    y_sb = _from_heads_t(sot).astype(BF16)
    ys = (y_sg, y_cv, y_mla, y_sb)
    up, merged = merge_fwd(ys, p["w_branch"], z)
    x1, out = matmul(merged, p["w_out"], name="mm_out", out_dtypes=(F32, F32), epi=_resid_epi, epi_in=((x, "mn"), (g1, "n")))
    h2 = normmod_fwd(x1, p["norm2_g"], sc2, sh2)
    a, r = matmul(h2, p["w1"], name="mm_up", out_dtypes=(F32, BF16),
                  epi=lambda acc: (acc, jnp.square(jnp.maximum(acc, 0.0))))
    x2, mo = matmul(r, p["w2"], name="mm_down", out_dtypes=(F32, F32), epi=_resid_epi, epi_in=((x1, "mn"), (g2, "n")))
    saved = dict(x=x, h=h, z=z, ys=ys, mq=mq, mk=mk, mv=mv, mot=mot, lse=lse, sq=sq, sk=sk, sv=sv, rs=rs, up=up,
                 merged=merged, out=out, x1=x1, h2=h2, a=a, r=r, mo=mo)
    return x2, saved, got, p


def _grad_blocks(g, name):
    return _to_blocks(g[name][None], dict(SHARDED)[name]).astype(BF16)


def _layer_bwd(dx2, s, mod, p, cos, sin, carried):
    t = dx2.shape[0]
    bk = _att_blocks(t)[1]
    sh1, sc1, g1, sh2, sc2, g2 = [mod[i:i + 1] for i in range(6)]
    g = {}
    dm, dg2 = resid_bwd(dx2, s["mo"], g2)
    da = matmul(dm, p["w2_t"], name="mm_down_dx", out_dtypes=(BF16,), epi=lambda acc, a: (acc * (2.0 * jnp.maximum(a, 0.0)),),
                epi_in=((s["a"], "mn"),))
    g["mlp_w2"] = matmul_tn(s["r"], dm, name="mm_down_dw")
    dh2 = matmul(da, p["w1_t"], name="mm_up_dx")
    g["mlp_w1"] = matmul_tn(s["h2"], da, name="mm_up_dw")
    dx1, g["norm2_g"], dsc2, dsh2 = normmod_bwd(s["x1"], p["norm2_g"], sc2, sh2, dh2, dx2)
    dout, dg1 = resid_bwd(dx1, s["out"], g1)
    dmerged = matmul(dout, p["w_out_t"], name="mm_out_dx")
    g["w_out"] = matmul_tn(s["merged"], dout, name="mm_out_dw")
    dzg, dup = merge_bwd(dmerged, s["up"], s["z"])
    dys = [matmul(dup[n], p["w_branch_t"][n], name="mm_branch_dx") for n in range(N_BRANCH)]
    g["w_branch"] = jnp.stack([matmul_tn(s["ys"][n], dup[n], name="mm_branch_dw") for n in range(N_BRANCH)])
    z = s["z"]
    dz_sg, g["sg_norm_g"], g["sg_w"], dbt = sg_bwd(z, p["sg_norm_g"], p["sg_w"], p["sg_bt"], dys[0])
    g["sg_b"] = dbt.T
    dz_cv, g["conv_w"] = conv_bwd(z, p["conv_w"], dys[1])
    do = _to_heads(dys[2], MLA_V).astype(BF16)
    dqt, dk, dv = softmax_att_bwd(s["mq"].transpose(0, 2, 1), s["mq"], s["mk"], _key_blocks(s["mk"], bk), s["mv"], s["mot"],
                                  s["lse"], do, do.transpose(0, 2, 1), 1.0)
    dz_mla, g["mla_q_norm_g"], g["mla_kv_norm_g"], g["mla_w_uq"], g["mla_w_ukv"] = mla_prep_bwd(
        z, cos, sin, p["gq"], p["gkv"], p["wuq"], p["wukv"], dqt.transpose(0, 2, 1), dk, dv)
    do = _to_heads(dys[3], SB_DIM).astype(BF16)
    comm = (list(carried) + [_grad_blocks(g, n) for n in OWN_GRADS], False)
    dqt, dk, dv, got = stick_att_bwd(s["sq"].transpose(0, 2, 1), s["sq"], s["sk"], _key_blocks(s["sk"], bk), s["sv"], s["rs"],
                                     do, do.transpose(0, 2, 1), SB_SCALE, comm)
    dz_sb = jnp.concatenate([_from_heads_t(dqt), _from_heads(dk), _from_heads(dv)], axis=1).astype(BF16)
    dz = jnp.concatenate([dz_sg, dz_cv, dz_mla, dz_sb, dzg], axis=1)
    dh = matmul(dz, p["w_in_t"], name="mm_in_dx", tk=2176)
    dw_in = matmul_tn(s["h"], dz, name="mm_in_dw")
    g["w_in"] = jnp.concatenate([dw_in[:, :Z_MLA + MLA_COLS], dw_in[:, Z_SB:]], axis=1)
    dx, g["norm1_g"], dsc1, dsh1 = normmod_bwd(s["x"], p["norm1_g"], sc1, sh1, dh, dx1)
    g["mla_q_norm_g"], g["mla_kv_norm_g"] = g["mla_q_norm_g"][0], g["mla_kv_norm_g"][0]
    g["norm1_g"], g["norm2_g"], g["sg_norm_g"] = g["norm1_g"][0], g["norm2_g"][0], g["sg_norm_g"][0]
    g["ada_b"] = jnp.concatenate([dsh1, dsc1, dg1, dsh2, dsc2, dg2], axis=1)[0]
    return dx, g, got[:len(carried)], got[len(carried):]


WEIGHT_NAMES = ("ada_w", "ada_b", "norm1_g", "norm2_g", "w_in", "sg_norm_g", "sg_w", "sg_b", "conv_w", "mla_q_norm_g",
                "mla_w_uq", "mla_kv_norm_g", "mla_w_ukv", "w_branch", "w_out", "mlp_w1", "mlp_w2", "final_norm_g")
SHARDED = (("w_in", 2), ("mla_w_uq", 2), ("mla_w_ukv", 2), ("w_branch", 3), ("w_out", 1), ("mlp_w1", 2), ("mlp_w2", 1),
           ("conv_w", 2))
EARLY_WEIGHTS = ("w_in", "mla_w_uq", "mla_w_ukv")
LATE_WEIGHTS = ("w_branch", "w_out", "mlp_w1", "mlp_w2")
OWN_GRADS = tuple(n for n, _ in SHARDED if n != "w_in")
REPLICATED = ("ada_b", "norm1_g", "norm2_g", "sg_norm_g", "sg_w", "sg_b", "mla_q_norm_g", "mla_kv_norm_g", "final_norm_g")


def kernel(x, c, positions, ada_w, ada_b, norm1_g, norm2_g, w_in, sg_norm_g, sg_w, sg_b, conv_w, mla_q_norm_g, mla_w_uq, mla_kv_norm_g, mla_w_ukv, w_branch, w_out, mlp_w1, mlp_w2, final_norm_g, loss_target, m_ada_w, m_ada_b, m_norm1_g, m_norm2_g, m_w_in, m_sg_norm_g, m_sg_w, m_sg_b, m_conv_w, m_mla_q_norm_g, m_mla_w_uq, m_mla_kv_norm_g, m_mla_w_ukv, m_w_branch, m_w_out, m_mlp_w1, m_mlp_w2, m_final_norm_g, v_ada_w, v_ada_b, v_norm1_g, v_norm2_g, v_w_in, v_sg_norm_g, v_sg_w, v_sg_b, v_conv_w, v_mla_q_norm_g, v_mla_w_uq, v_mla_kv_norm_g, v_mla_w_ukv, v_w_branch, v_w_out, v_mlp_w1, v_mlp_w2, v_final_norm_g):
    given = dict(locals())
    w = {n: given[n] for n in WEIGHT_NAMES}
    m = {n: given["m_" + n] for n in WEIGHT_NAMES}
    v = {n: given["v_" + n] for n in WEIGHT_NAMES}
    xs, tgt = x[0], loss_target[0]
    nl = ada_w.shape[0]
    me = 4 * lax.axis_index("x") + 2 * lax.axis_index("y") + lax.axis_index("c")

    inv_freq = ROPE_THETA ** (-jnp.arange(0, MLA_ROPE, 2, dtype=F32) / MLA_ROPE)
    ang = positions[0].astype(F32)[:, None] * inv_freq
    cos, sin = jnp.cos(ang), jnp.sin(ang)

    c_all, conv_blocks = _exchange([c, conv_w], gather=True, name="gather_c_conv")
    c_all = c_all[:, 0]
    conv_full = _from_blocks(conv_blocks, 2)
    ncol = ada_w.shape[2]
    ada_b_cols = lax.dynamic_slice_in_dim(ada_b, me * ncol, ncol, axis=1)[:, None, :]
    mod_cols = mod_fwd(c_all, ada_w, ada_b_cols)
    got = _exchange([mod_cols], gather=True, name="gather_mod")[0]
    mod = jnp.moveaxis(lax.dynamic_index_in_dim(got, me, axis=2, keepdims=False), 0, 1).reshape(nl, 6, D_MODEL)

    cut = Z_MLA + MLA_COLS
    axis_of = dict(SHARDED)

    def shards(names, lo, hi):
        return [w[n][lo:hi].astype(BF16) for n in names]

    def early_weights(names, got, l):
        full = {n: _from_blocks(seg, axis_of[n])[0] for n, seg in zip(names, got)}
        w_in_pad = jnp.concatenate([full["w_in"][:, :cut], jnp.zeros((D_MODEL, Z_SB - cut), BF16), full["w_in"][:, cut:]], axis=1)
        return dict(norm1_g=norm1_g[l][None], norm2_g=norm2_g[l][None], sg_norm_g=sg_norm_g[l][None], sg_w=sg_w[l],
                    sg_bt=sg_b[l].T, conv_w=conv_full[l], gq=mla_q_norm_g[l][None], gkv=mla_kv_norm_g[l][None],
                    wuq=full["mla_w_uq"], wukv=full["mla_w_ukv"], w_in=w_in_pad, w_in_t=w_in_pad.T)

    def late_weights(names, got):
        full = {n: _from_blocks(seg, axis_of[n])[0] for n, seg in zip(names, got)}
        return dict(w_branch=full["w_branch"], w_branch_t=full["w_branch"].transpose(0, 2, 1), w_out=full["w_out"],
                    w_out_t=full["w_out"].T, w1=full["mlp_w1"], w1_t=full["mlp_w1"].T, w2=full["mlp_w2"], w2_t=full["mlp_w2"].T)

    every = EARLY_WEIGHTS + LATE_WEIGHTS
    n_early, n_all = len(EARLY_WEIGHTS), len(every)
    layers = [None] * nl
    layers[0] = early_weights(EARLY_WEIGHTS, _exchange(shards(EARLY_WEIGHTS, 0, 1), gather=True, name="gather_weights_first"), 0)
    rides = {0: list(range(1, min(2, nl))), 1: list(range(2, nl))}

    saved = []
    xc = xs
    for l in range(nl):
        srcs = shards(LATE_WEIGHTS, 0, 1) if l == 0 else []
        for r in rides.get(l, []):
            srcs += shards(every, r, r + 1)
        late = (lambda got: late_weights(LATE_WEIGHTS, got[:len(LATE_WEIGHTS)])) if l == 0 else None
        xc, s, got, layers[l] = _layer_fwd(xc, mod[l], layers[l], cos, sin, (srcs, True) if srcs else None, late)
        saved.append(s)
        got = got[len(LATE_WEIGHTS):] if l == 0 else got
        for i, r in enumerate(rides.get(l, [])):
            part = got[i * n_all:(i + 1) * n_all]
            layers[r] = {**early_weights(EARLY_WEIGHTS, part[:n_early], r), **late_weights(LATE_WEIGHTS, part[n_early:])}
    loss_part, dx, dgf = loss_head(xc, final_norm_g[None], tgt)
    loss = lax.psum(loss_part[0, 0], AXES)

    grads = [None] * nl
    landed = {n: [None] * nl for n, _ in SHARDED}
    carried = []
    for l in reversed(range(nl)):
        dx, grads[l], got_carried, got_own = _layer_bwd(dx, saved[l], mod[l], layers[l], cos, sin, carried)
        if carried:
            landed["w_in"][l + 1] = got_carried[0]
        for n, arr in zip(OWN_GRADS, got_own):
            landed[n][l] = arr
        carried = [_grad_blocks(grads[l], "w_in")]
    landed["w_in"][0] = _exchange(carried, gather=False, name="exchange_grads_first")[0]
    gfull = {n: jnp.stack([grads[l][n] for l in range(nl)]) for n in REPLICATED if n != "final_norm_g"}
    gfull["final_norm_g"] = dgf[0]

    out = {}

    def update(n, slots):
        for kind, arr in zip(("grad", "delta", "new_m", "new_v"), adamw(slots, w[n], m[n], v[n], name="adamw_" + n)):
            out[kind, n] = arr

    for n, _ in SHARDED:
        update(n, jnp.concatenate(landed[n], axis=1))
    got = _exchange([gfull[n] for n in REPLICATED], gather=True, name="gather_small_grads")
    for n, slots in zip(REPLICATED, got):
        update(n, slots)
    dmod_cols = jnp.moveaxis(lax.dynamic_slice_in_dim(got[0], me * ncol, ncol, axis=2), 0, 1)
    update("ada_w", ada_w_grad(c_all.T, dmod_cols)[None])

    res = [loss, dx[None]]
    for kind in ("grad", "delta", "new_m", "new_v"):
        res += [out[kind, n] for n in WEIGHT_NAMES]
    return tuple(res)
```
